```python
import math
import jax, jax.numpy as jnp
from jax import lax
import numpy as np

D_MODEL = 2048
BATCH = 1
SEQ = 8192
DEPTH = 4

GRID_W = 64
CTX_LEN = 256
BRANCH_W = 512
N_BRANCH = 4
N_SECTIONS = 10
RET_HEADS = 4
RET_DK = BRANCH_W // RET_HEADS
RET_CHUNK = 128
ROPE_BASE = 10000.0
S5_GROUP = 16
S5_GROUPS = BRANCH_W // S5_GROUP
S5_STATE = 64
CONV_W = 3
CMLP_CHUNK = 128
CMLP_GROUPS = 4
CMLP_GW = BRANCH_W // CMLP_GROUPS
D_FF = ((8 * D_MODEL + 3 * 256 - 1) // (3 * 256)) * 256
EPS = 1e-6

kernel_name = 'hybrid_parallel_flow_block'

F32 = jnp.float32


def rms_norm(x, g):
    xf = x.astype(F32)
    y = xf * lax.rsqrt(jnp.mean(xf * xf, axis=-1, keepdims=True) + EPS)
    return (y * g.astype(F32)).astype(x.dtype)


def layer_norm(x, g, b):
    xf = x.astype(F32)
    mu = jnp.mean(xf, axis=-1, keepdims=True)
    var = jnp.mean(jnp.square(xf - mu), axis=-1, keepdims=True)
    return ((xf - mu) * lax.rsqrt(var + EPS) * g.astype(F32) + b.astype(F32)).astype(x.dtype)


def modulate(x, g, shift, scale):
    return rms_norm(x, g) * (1 + scale) + shift


def swiglu(h, w1, w3, w2):
    return (jax.nn.silu(h @ w1) * (h @ w3)) @ w2


def rope_tables(rows):
    row = jnp.broadcast_to(jnp.arange(rows)[:, None], (rows, GRID_W)).reshape(-1).astype(F32)
    col = jnp.broadcast_to(jnp.arange(GRID_W)[None, :], (rows, GRID_W)).reshape(-1).astype(F32)
    nf = RET_DK // 4
    freqs = ROPE_BASE ** (-jnp.arange(nf, dtype=F32) / nf)
    ang = jnp.concatenate([row[:, None] * freqs, col[:, None] * freqs], axis=-1)
    return jnp.cos(ang), jnp.sin(ang)


def rope(t, cos, sin):
    t1, t2 = t[..., :RET_DK // 2], t[..., RET_DK // 2:]
    return jnp.concatenate([t1 * cos - t2 * sin, t1 * sin + t2 * cos], axis=-1)


def retention_scan(q, k, v, log_gamma, s0, include_diag):
    b, h, n, d = k.shape
    nc = n // RET_CHUNK
    pos = jnp.arange(RET_CHUNK, dtype=F32)
    lg = log_gamma[:, None]
    kc = k.reshape(b, h, nc, RET_CHUNK, d)
    vc = v.reshape(b, h, nc, RET_CHUNK, d)
    k_decay = jnp.exp(lg * (RET_CHUNK - 1 - pos))
    chunk_kv = jnp.einsum('bhncd,bhnce->bhnde', kc * k_decay[None, :, None, :, None], vc)
    g_chunk = jnp.exp(log_gamma * RET_CHUNK)[None, :, None, None]

    def step(s, u):
        return g_chunk * s + u, s

    s_final, s_start = lax.scan(step, s0, jnp.moveaxis(chunk_kv, 2, 0))
    if q is None:
        return None, s_final
    s_start = jnp.moveaxis(s_start, 0, 2)
    qc = q.reshape(b, h, nc, RET_CHUNK, d)
    q_decay = jnp.exp(lg * (pos + 1.0))
    cross = jnp.einsum('bhncd,bhnde->bhnce', qc * q_decay[None, :, None, :, None], s_start)
    diff = pos[:, None] - pos[None, :]
    mask = diff >= 0 if include_diag else diff > 0
    dmat = jnp.where(mask, jnp.exp(lg[:, :, None] * jnp.maximum(diff, 0.0)), 0.0)
    scores = jnp.einsum('bhnid,bhnjd->bhnij', qc, kc) * dmat[None, :, None]
    intra = jnp.einsum('bhnij,bhnje->bhnie', scores, vc)
    return (intra + cross).reshape(b, h, n, d), s_final


def _heads(t):
    if t is None:
        return None
    b, n, _ = t.shape
    return t.reshape(b, n, RET_HEADS, RET_DK).transpose(0, 2, 1, 3).astype(F32)


def _flip(t):
    return None if t is None else t[:, :, ::-1]


def _retention_out(o, g):
    mu = jnp.mean(o, axis=-1, keepdims=True)
    var = jnp.mean(jnp.square(o - mu), axis=-1, keepdims=True)
    o = (o - mu) * lax.rsqrt(var + EPS)
    b, h, n, d = o.shape
    o = o.transpose(0, 2, 1, 3).reshape(b, n, h * d)
    return (o * jax.nn.silu(g.astype(F32))).astype(g.dtype)


def retention_mixer(q, k, v, g, qc, kc, vc, gc, decay_logit, cos, sin):
    lg = jax.nn.log_sigmoid(decay_logit.astype(F32))
    scale = RET_DK ** -0.5
    q = rope(_heads(q), cos, sin) * scale
    k = rope(_heads(k), cos, sin)
    v = _heads(v)
    qc = None if qc is None else _heads(qc) * scale
    kc, vc = _heads(kc), _heads(vc)
    s0 = jnp.zeros((k.shape[0], RET_HEADS, RET_DK, RET_DK), F32)
    oc_f, sc_f = retention_scan(qc, kc, vc, lg[0], s0, True)
    oc_b, sc_b = retention_scan(_flip(qc), _flip(kc), _flip(vc), lg[1], s0, False)
    ox_f, _ = retention_scan(q, k, v, lg[0], sc_f, True)
    ox_b, _ = retention_scan(_flip(q), _flip(k), _flip(v), lg[1], sc_b, False)
    y = _retention_out(ox_f + _flip(ox_b), g)
    yc = None if qc is None else _retention_out(oc_f + _flip(oc_b), gc)
    return y, yc


def s5_discretize(a_re, a_im, b_re, b_im, log_dt):
    dt = jnp.exp(log_dt)[:, None]
    mag = jnp.exp(dt * a_re)
    ang = dt * a_im
    ab_re, ab_im = mag * jnp.cos(ang), mag * jnp.sin(ang)
    nr, ni = ab_re - 1.0, ab_im
    den = a_re * a_re + a_im * a_im
    f_re = (nr * a_re + ni * a_im) / den
    f_im = (ni * a_re - nr * a_im) / den
    bb_re = f_re[..., None] * b_re - f_im[..., None] * b_im
    bb_im = f_re[..., None] * b_im + f_im[..., None] * b_re
    return ab_re, ab_im, bb_re, bb_im


def s5_scan(ug, ab_re, ab_im, bb_re, bb_im, s0_re, s0_im):
    bu_re = jnp.einsum('bngc,gpc->bngp', ug, bb_re)
    bu_im = jnp.einsum('bngc,gpc->bngp', ug, bb_im)
    bu_re = bu_re.at[:, 0].add(ab_re * s0_re - ab_im * s0_im)
    bu_im = bu_im.at[:, 0].add(ab_re * s0_im + ab_im * s0_re)
    a_re = jnp.broadcast_to(ab_re, bu_re.shape)
    a_im = jnp.broadcast_to(ab_im, bu_im.shape)

    def combine(e1, e2):
        a1r, a1i, b1r, b1i = e1
        a2r, a2i, b2r, b2i = e2
        return (a2r * a1r - a2i * a1i, a2r * a1i + a2i * a1r,
                a2r * b1r - a2i * b1i + b2r, a2r * b1i + a2i * b1r + b2i)

    _, _, x_re, x_im = lax.associative_scan(combine, (a_re, a_im, bu_re, bu_im), axis=1)
    return x_re, x_im


def s5_readout(x_re, x_im, ug, c_re, c_im, d, w_glu):
    b, n = ug.shape[:2]
    y = (jnp.einsum('bngp,gcp->bngc', x_re, c_re.astype(F32))
         - jnp.einsum('bngp,gcp->bngc', x_im, c_im.astype(F32))
         + d.astype(F32).reshape(S5_GROUPS, S5_GROUP) * ug)
    y = jax.nn.gelu(y.reshape(b, n, BRANCH_W))
    return y * jax.nn.sigmoid(y @ w_glu.astype(F32))


def s5_mixer(u, uc, a_re, a_im, b_re, b_im, c_re, c_im, d, log_dt, w_glu, need_ctx):
    b, n, _ = u.shape
    ug = u.astype(F32).reshape(b, n, S5_GROUPS, S5_GROUP)
    ucg = uc.astype(F32).reshape(b, uc.shape[1], S5_GROUPS, S5_GROUP)
    a_re, a_im = a_re.astype(F32), a_im.astype(F32)
    disc = [s5_discretize(a_re, a_im, b_re[i].astype(F32), b_im[i].astype(F32),
                          log_dt[i].astype(F32)) for i in range(2)]
    s0 = jnp.zeros((b, S5_GROUPS, S5_STATE), F32)
    cf_re, cf_im = s5_scan(ucg, *disc[0], s0, s0)
    cb_re, cb_im = s5_scan(ucg[:, ::-1], *disc[1], s0, s0)
    xf_re, xf_im = s5_scan(ug, *disc[0], cf_re[:, -1], cf_im[:, -1])
    xb_re, xb_im = s5_scan(ug[:, ::-1], *disc[1], cb_re[:, -1], cb_im[:, -1])
    y = s5_readout(xf_re + xb_re[:, ::-1], xf_im + xb_im[:, ::-1], ug,
                   c_re, c_im, d, w_glu).astype(u.dtype)
    if not need_ctx:
        return y, None
    yc = s5_readout(cf_re + cb_re[:, ::-1], cf_im + cb_im[:, ::-1], ucg,
                    c_re, c_im, d, w_glu).astype(uc.dtype)
    return y, yc


def conv_mixer(xin, bg, cg, conv_w):
    z = cg * xin
    y = lax.conv_general_dilated(z, conv_w[:, None, :], window_strides=(1,),
                                 padding=((CONV_W // 2, CONV_W // 2),),
                                 dimension_numbers=('NWC', 'WIO', 'NWC'),
                                 feature_group_count=BRANCH_W)
    return bg * y


def chunk_mlp_mixer(gu, gv, ln_g, ln_b, w_s, b_s):
    u = jax.nn.gelu(gu)
    v = layer_norm(jax.nn.gelu(gv), ln_g, ln_b)
    b, n, _ = v.shape
    vg = v.reshape(b, n // CMLP_CHUNK, CMLP_CHUNK, CMLP_GROUPS, CMLP_GW)
    mixed = jnp.einsum('bnpgc,gqp->bnqgc', vg, w_s) + b_s.T[None, None, :, :, None]
    return u * mixed.reshape(b, n, BRANCH_W)


def merge_branches(h, outs, w_branch, w_merge, b_merge, w_out):
    o = jnp.stack(outs, axis=2)
    ybr = jnp.einsum('bnke,ked->bnkd', o, w_branch)
    b, n, _ = h.shape
    gates = jax.nn.sigmoid(h @ w_merge + b_merge).reshape(b, n, N_BRANCH, D_MODEL)
    return jnp.sum(gates * ybr, axis=2) @ w_out


def mixing_sublayer(h, hc, cos, sin, need_ctx, w_in, ret_decay_logit, s5_a_re, s5_a_im,
                    s5_b_re, s5_b_im, s5_c_re, s5_c_im, s5_d, s5_log_dt, s5_w_glu, conv_w,
                    cmlp_ln_g, cmlp_ln_b, cmlp_ws, cmlp_bs, w_branch, w_merge, b_merge, w_out):
    q, k, v, g, u, cx, cb, cc, gu, gv = jnp.split(h @ w_in, N_SECTIONS, axis=-1)
    if need_ctx:
        qc, kc, vc, gc, uc, cxc, cbc, ccc, guc, gvc = jnp.split(hc @ w_in, N_SECTIONS, axis=-1)
    else:
        qc = gc = None
        kc, vc = jnp.split(hc @ w_in[:, 1 * BRANCH_W:3 * BRANCH_W], 2, axis=-1)
        uc = hc @ w_in[:, 4 * BRANCH_W:5 * BRANCH_W]
    y_ret, yc_ret = retention_mixer(q, k, v, g, qc, kc, vc, gc, ret_decay_logit, cos, sin)
    y_s5, yc_s5 = s5_mixer(u, uc, s5_a_re, s5_a_im, s5_b_re, s5_b_im, s5_c_re, s5_c_im,
                           s5_d, s5_log_dt, s5_w_glu, need_ctx)
    y = merge_branches(h, [y_ret, y_s5, conv_mixer(cx, cb, cc, conv_w),
                           chunk_mlp_mixer(gu, gv, cmlp_ln_g, cmlp_ln_b, cmlp_ws, cmlp_bs)],
                       w_branch, w_merge, b_merge, w_out)
    if not need_ctx:
        return y, None
    yc = merge_branches(hc, [yc_ret, yc_s5, conv_mixer(cxc, cbc, ccc, conv_w),
                             chunk_mlp_mixer(guc, gvc, cmlp_ln_g, cmlp_ln_b, cmlp_ws, cmlp_bs)],
                        w_branch, w_merge, b_merge, w_out)
    return y, yc


def setup_inputs(seed: int = 0) -> dict:
    key = jax.random.key(seed)
    ks = jax.random.split(key, 32)
    L, D, BW, F = DEPTH, D_MODEL, BRANCH_W, D_FF
    G, P, GC = S5_GROUPS, S5_STATE, S5_GROUP

    def nrm(i, shape, scale):
        return jax.random.normal(ks[i], shape, F32) * scale

    hh = np.arange(RET_HEADS, dtype=np.float32)
    gamma0 = 1.0 - 2.0 ** (-5.0 - hh)
    logit0 = jnp.asarray(np.log(gamma0 / (1.0 - gamma0)), F32)
    return {
        'x': nrm(0, (BATCH, SEQ, D), 1.0),
        'c': nrm(1, (BATCH, D), 1.0),
        'ctx': nrm(2, (BATCH, CTX_LEN, D), 1.0),
        'c_ctx': nrm(3, (D,), 1.0),
        'ada_w': nrm(4, (L, D, 6 * D), 0.5 * D ** -0.5),
        'ada_b': nrm(5, (L, 6 * D), 0.02),
        'norm1_g': 1.0 + nrm(6, (L, D), 0.02),
        'norm2_g': 1.0 + nrm(7, (L, D), 0.02),
        'w_in': nrm(8, (L, D, N_SECTIONS * BW), D ** -0.5),
        'ret_decay_logit': logit0 + nrm(9, (L, 2, RET_HEADS), 0.05),
        's5_a_re': -0.5 + nrm(10, (L, G, P), 0.01),
        's5_a_im': jnp.pi * jnp.arange(P, dtype=F32) + nrm(11, (L, G, P), 0.01),
        's5_b_re': nrm(12, (L, 2, G, P, GC), (2.0 * GC) ** -0.5),
        's5_b_im': nrm(13, (L, 2, G, P, GC), (2.0 * GC) ** -0.5),
        's5_c_re': nrm(14, (L, G, GC, P), (2.0 * P) ** -0.5),
        's5_c_im': nrm(15, (L, G, GC, P), (2.0 * P) ** -0.5),
        's5_d': nrm(16, (L, BW), 1.0),
        's5_log_dt': jax.random.uniform(ks[17], (L, 2, G), F32,
                                        minval=math.log(1e-3), maxval=math.log(1e-1)),
        's5_w_glu': nrm(18, (L, BW, BW), BW ** -0.5),
        'conv_w': nrm(19, (L, CONV_W, BW), CONV_W ** -0.5),
        'cmlp_ln_g': 1.0 + nrm(20, (L, BW), 0.02),
        'cmlp_ln_b': nrm(21, (L, BW), 0.02),
        'cmlp_ws': nrm(22, (L, CMLP_GROUPS, CMLP_CHUNK, CMLP_CHUNK), CMLP_CHUNK ** -0.5),
        'cmlp_bs': 1.0 + nrm(23, (L, CMLP_GROUPS, CMLP_CHUNK), 0.1),
        'w_branch': nrm(24, (L, N_BRANCH, BW, D), BW ** -0.5),
        'w_merge': nrm(25, (L, D, N_BRANCH * D), D ** -0.5),
        'b_merge': nrm(26, (L, N_BRANCH * D), 0.02),
        'w_out': nrm(27, (L, D, D), D ** -0.5),
        'ffn_w1': nrm(28, (L, D, F), D ** -0.5),
        'ffn_w3': nrm(29, (L, D, F), D ** -0.5),
        'ffn_w2': nrm(30, (L, F, D), F ** -0.5),
        'final_norm_g': 1.0 + nrm(31, (D,), 0.02),
    }


def reference(x, c, ctx, c_ctx, ada_w, ada_b, norm1_g, norm2_g, w_in, ret_decay_logit,
              s5_a_re, s5_a_im, s5_b_re, s5_b_im, s5_c_re, s5_c_im, s5_d, s5_log_dt,
              s5_w_glu, conv_w, cmlp_ln_g, cmlp_ln_b, cmlp_ws, cmlp_bs, w_branch, w_merge,
              b_merge, w_out, ffn_w1, ffn_w3, ffn_w2, final_norm_g):
    rows = x.shape[1] // GRID_W
    cos, sin = rope_tables(rows)
    silu_c = jax.nn.silu(c)
    silu_cc = jax.nn.silu(c_ctx)
    xc = ctx
    for l in range(DEPTH):
        need_ctx = l < DEPTH - 1
        mod = silu_c @ ada_w[l] + ada_b[l]
        modc = silu_cc @ ada_w[l] + ada_b[l]
        sh1, sc1, g1, sh2, sc2, g2 = [m[:, None, :] for m in jnp.split(mod, 6, axis=-1)]
        sh1c, sc1c, g1c, sh2c, sc2c, g2c = jnp.split(modc, 6, axis=-1)
        h = modulate(x, norm1_g[l], sh1, sc1)
        hc = modulate(xc, norm1_g[l], sh1c, sc1c)
        y, yc = mixing_sublayer(h, hc, cos, sin, need_ctx, w_in[l], ret_decay_logit[l],
                                s5_a_re[l], s5_a_im[l], s5_b_re[l], s5_b_im[l], s5_c_re[l],
                                s5_c_im[l], s5_d[l], s5_log_dt[l], s5_w_glu[l], conv_w[l],
                                cmlp_ln_g[l], cmlp_ln_b[l], cmlp_ws[l], cmlp_bs[l],
                                w_branch[l], w_merge[l], b_merge[l], w_out[l])
        x = x + g1 * y
        x = x + g2 * swiglu(modulate(x, norm2_g[l], sh2, sc2), ffn_w1[l], ffn_w3[l], ffn_w2[l])
        if need_ctx:
            xc = xc + g1c * yc
            xc = xc + g2c * swiglu(modulate(xc, norm2_g[l], sh2c, sc2c),
                                   ffn_w1[l], ffn_w3[l], ffn_w2[l])
    return rms_norm(x, final_norm_g)
```

```python
import functools
import math

import jax
import jax.numpy as jnp
from jax import lax
from jax.experimental import pallas as pl
from jax.experimental.pallas import tpu as pltpu

F32 = jnp.float32
BF16 = jnp.bfloat16

D_MODEL = 2048
DEPTH = 4
GRID_W = 64
BRANCH_W = 512
N_BRANCH = 4
N_SECTIONS = 10
RET_HEADS = 4
RET_DK = BRANCH_W // RET_HEADS
ROPE_BASE = 10000.0
S5_GROUP = 16
S5_GROUPS = BRANCH_W // S5_GROUP
S5_STATE = 64
CMLP_GROUPS = 4
CMLP_GW = BRANCH_W // CMLP_GROUPS
EPS = 1e-6

CHUNK = 128
LANES = 128
SUBLANES = 8
S5_LANE_BLOCKS = BRANCH_W // LANES
S5_GROUPS_PER_BLOCK = LANES // S5_GROUP
S5_BLOCK_STATE = S5_GROUPS_PER_BLOCK * S5_STATE
S5_STATE_W = S5_GROUPS * S5_STATE
MIB = 1024 * 1024
VMEM_BUDGET_MIB = 56


def _cparams(semantics, vmem_mib=VMEM_BUDGET_MIB):
    return pltpu.CompilerParams(dimension_semantics=semantics,
                                vmem_limit_bytes=vmem_mib * MIB)


def _dot(a, b):
    return jnp.dot(a, b, preferred_element_type=F32)


def _row_tile(t, cap=1056):
    for tm in (1056, 1024, 768, 528, 512, 384, 256, 128):
        if tm <= cap and t % tm == 0:
            return tm
    raise ValueError(f"unsupported token count {t}")


def _is_ctx_rows(tile_idx, tm, n_ctx):
    rows = tile_idx * tm + lax.broadcasted_iota(jnp.int32, (tm, 1), 0)
    return rows < n_ctx


def _weight_spec(block, index_map):
    return pl.BlockSpec(block, index_map, pipeline_mode=pl.Buffered(1))


def _adaln_body(c_ref, w_ref, b_ref, o_ref):
    c = c_ref[...]
    s = c * jax.nn.sigmoid(c)
    w = w_ref[...]
    s_hi = s.astype(BF16)
    s_lo = (s - s_hi.astype(F32)).astype(BF16)
    w_hi = w.astype(BF16)
    w_lo = (w - w_hi.astype(F32)).astype(BF16)
    acc = _dot(s_hi, w_hi) + _dot(s_lo, w_hi) + _dot(s_hi, w_lo)
    o_ref[...] = acc + b_ref[...]


def _adaln(cvec, ada_w, ada_b):
    depth, d, n = ada_w.shape
    tn = 1024
    return pl.pallas_call(
        _adaln_body,
        grid=(depth, n // tn),
        in_specs=[
            pl.BlockSpec((SUBLANES, d), lambda l, j: (0, 0)),
            pl.BlockSpec((None, d, tn), lambda l, j: (l, 0, j)),
            pl.BlockSpec((None, 1, tn), lambda l, j: (l, 0, j)),
        ],
        out_specs=pl.BlockSpec((None, SUBLANES, tn), lambda l, j: (l, 0, j)),
        out_shape=jax.ShapeDtypeStruct((depth, SUBLANES, n), F32),
        compiler_params=_cparams(("parallel", "parallel")),
        name="adaln",
    )(cvec, ada_w, ada_b.reshape(depth, 1, n))


def _modulate_body(x_ref, g_ref, sc_ref, sh_ref, o_ref, *, tm, n_ctx):
    x = x_ref[...]
    y = x * lax.rsqrt(jnp.mean(x * x, axis=-1, keepdims=True) + EPS) * g_ref[...]
    is_ctx = _is_ctx_rows(pl.program_id(0), tm, n_ctx)
    sc = jnp.where(is_ctx, sc_ref[1:2, :], sc_ref[0:1, :])
    sh = jnp.where(is_ctx, sh_ref[1:2, :], sh_ref[0:1, :])
    o_ref[...] = (y * (1.0 + sc) + sh).astype(o_ref.dtype)


def _modulate(x, g, sc, sh, n_ctx):
    t, d = x.shape
    tm = _row_tile(t)
    return pl.pallas_call(
        functools.partial(_modulate_body, tm=tm, n_ctx=n_ctx),
        grid=(t // tm,),
        in_specs=[
            pl.BlockSpec((tm, d), lambda i: (i, 0)),
            pl.BlockSpec((1, d), lambda i: (0, 0)),
            pl.BlockSpec((2, d), lambda i: (0, 0)),
            pl.BlockSpec((2, d), lambda i: (0, 0)),
        ],
        out_specs=pl.BlockSpec((tm, d), lambda i: (i, 0)),
        out_shape=jax.ShapeDtypeStruct((t, d), BF16),
        compiler_params=_cparams(("parallel",)),
        name="modulate",
    )(x, g.reshape(1, d), sc, sh)


def _mm_body(x_ref, w_ref, o_ref, wb_ref):
    @pl.when(pl.program_id(1) == 0)
    def _():
        wb_ref[...] = w_ref[...].astype(BF16)

    o_ref[...] = _dot(x_ref[...], wb_ref[...]).astype(o_ref.dtype)


def _matmul(x, w_stack, layer, tn, out_dtype):
    t, k = x.shape
    n = w_stack.shape[-1]
    tm = _row_tile(t)
    return pl.pallas_call(
        _mm_body,
        grid=(n // tn, t // tm),
        in_specs=[
            pl.BlockSpec((tm, k), lambda j, i: (i, 0)),
            _weight_spec((None, k, tn), lambda j, i: (layer, 0, j)),
        ],
        out_specs=pl.BlockSpec((tm, tn), lambda j, i: (i, j)),
        out_shape=jax.ShapeDtypeStruct((t, n), out_dtype),
        scratch_shapes=[pltpu.VMEM((k, tn), BF16)],
        compiler_params=_cparams(("arbitrary", "arbitrary")),
        name="matmul",
    )(x, w_stack)


def _mm_residual_body(a_ref, w_ref, x_ref, g_ref, o_ref, wb_ref, *, tm, n_ctx):
    @pl.when(pl.program_id(1) == 0)
    def _():
        wb_ref[...] = w_ref[...].astype(BF16)

    y = _dot(a_ref[...], wb_ref[...])
    is_ctx = _is_ctx_rows(pl.program_id(1), tm, n_ctx)
    gate = jnp.where(is_ctx, g_ref[1:2, :], g_ref[0:1, :])
    o_ref[...] = x_ref[...] + gate * y


def _matmul_residual(a, w_stack, layer, x, gate, n_ctx, tn):
    t, k = a.shape
    n = w_stack.shape[-1]
    tm = _row_tile(t)
    return pl.pallas_call(
        functools.partial(_mm_residual_body, tm=tm, n_ctx=n_ctx),
        grid=(n // tn, t // tm),
        in_specs=[
            pl.BlockSpec((tm, k), lambda j, i: (i, 0)),
            _weight_spec((None, k, tn), lambda j, i: (layer, 0, j)),
            pl.BlockSpec((tm, tn), lambda j, i: (i, j)),
            pl.BlockSpec((2, tn), lambda j, i: (0, j)),
        ],
        out_specs=pl.BlockSpec((tm, tn), lambda j, i: (i, j)),
        out_shape=jax.ShapeDtypeStruct((t, n), F32),
        scratch_shapes=[pltpu.VMEM((k, tn), BF16)],
        compiler_params=_cparams(("arbitrary", "arbitrary")),
        name="matmul_residual",
    )(a, w_stack, x, gate)


def _ffn_up_body(h_ref, w1_ref, w3_ref, o_ref, w1b_ref, w3b_ref):
    @pl.when(pl.program_id(1) == 0)
    def _():
        w1b_ref[...] = w1_ref[...].astype(BF16)
        w3b_ref[...] = w3_ref[...].astype(BF16)

    h = h_ref[...]
    a = _dot(h, w1b_ref[...])
    b = _dot(h, w3b_ref[...])
    o_ref[...] = (a * jax.nn.sigmoid(a) * b).astype(o_ref.dtype)


def _ffn_up(h, w1_stack, w3_stack, layer, tn):
    t, k = h.shape
    n = w1_stack.shape[-1]
    tm = _row_tile(t)
    wspec = _weight_spec((None, k, tn), lambda j, i: (layer, 0, j))
    return pl.pallas_call(
        _ffn_up_body,
        grid=(n // tn, t // tm),
        in_specs=[pl.BlockSpec((tm, k), lambda j, i: (i, 0)), wspec, wspec],
        out_specs=pl.BlockSpec((tm, tn), lambda j, i: (i, j)),
        out_shape=jax.ShapeDtypeStruct((t, n), BF16),
        scratch_shapes=[pltpu.VMEM((k, tn), BF16), pltpu.VMEM((k, tn), BF16)],
        compiler_params=_cparams(("arbitrary", "arbitrary")),
        name="ffn_up",
    )(h, w1_stack, w3_stack)


def _merge_body(*refs):
    nb = N_BRANCH
    h_ref = refs[0]
    o_refs = refs[1:1 + nb]
    wm_refs = refs[1 + nb:1 + 2 * nb]
    bm_refs = refs[1 + 2 * nb:1 + 3 * nb]
    wb_refs = refs[1 + 3 * nb:1 + 4 * nb]
    out_ref, wmb_ref, wbb_ref = refs[1 + 4 * nb:]

    @pl.when(pl.program_id(1) == 0)
    def _():
        for k in range(nb):
            wmb_ref[k] = wm_refs[k][...].astype(BF16)
            wbb_ref[k] = wb_refs[k][...].astype(BF16)

    h = h_ref[...]
    acc = None
    for k in range(nb):
        gate = jax.nn.sigmoid(_dot(h, wmb_ref[k]) + bm_refs[k][...])
        term = gate * _dot(o_refs[k][...], wbb_ref[k])
        acc = term if acc is None else acc + term
    out_ref[...] = acc.astype(out_ref.dtype)


def _merge(h, outs, w_merge, b_merge, w_branch, layer, tn):
    t, d = h.shape
    bw = outs[0].shape[1]
    depth = w_merge.shape[0]
    tm = _row_tile(t, cap=528)
    nt = d // tn
    b_merge3 = b_merge.reshape(depth, 1, N_BRANCH * d)
    in_specs = [pl.BlockSpec((tm, d), lambda j, i: (i, 0))]
    in_specs += [pl.BlockSpec((tm, bw), lambda j, i: (i, 0)) for _ in range(N_BRANCH)]
    in_specs += [_weight_spec((None, d, tn), lambda j, i, k=k: (layer, 0, k * nt + j))
                 for k in range(N_BRANCH)]
    in_specs += [pl.BlockSpec((None, 1, tn), lambda j, i, k=k: (layer, 0, k * nt + j))
                 for k in range(N_BRANCH)]
    in_specs += [_weight_spec((None, None, bw, tn), lambda j, i, k=k: (layer, k, 0, j))
                 for k in range(N_BRANCH)]
    return pl.pallas_call(
        _merge_body,
        grid=(nt, t // tm),
        in_specs=in_specs,
        out_specs=pl.BlockSpec((tm, tn), lambda j, i: (i, j)),
        out_shape=jax.ShapeDtypeStruct((t, d), BF16),
        scratch_shapes=[pltpu.VMEM((N_BRANCH, d, tn), BF16),
                        pltpu.VMEM((N_BRANCH, bw, tn), BF16)],
        compiler_params=_cparams(("arbitrary", "arbitrary")),
        name="merge",
    )(h, *outs, *([w_merge] * N_BRANCH), *([b_merge3] * N_BRANCH), *([w_branch] * N_BRANCH))


def _chunk_order(step, n_chunks, n_ctx_chunks, reverse):
    if not reverse:
        return step
    return jnp.where(step < n_ctx_chunks, n_ctx_chunks - 1 - step,
                     n_chunks + n_ctx_chunks - 1 - step)


def _retention_body(*refs, final):
    if final:
        (q_ref, k_ref, v_ref, cos_ref, sin_ref, tab_ref, of_ref, g_ref, o_ref, s_ref) = refs
    else:
        (q_ref, k_ref, v_ref, cos_ref, sin_ref, tab_ref, o_ref, s_ref) = refs

    @pl.when(pl.program_id(0) == 0)
    def _():
        s_ref[...] = jnp.zeros_like(s_ref)

    cos = cos_ref[...]
    sin = sin_ref[...]
    scale = RET_DK ** -0.5
    for h in range(RET_HEADS):
        sl = slice(h * RET_DK, (h + 1) * RET_DK)
        q = q_ref[:, sl]
        k = k_ref[:, sl]
        v = v_ref[:, sl].astype(BF16)
        qr = (q * cos + pltpu.roll(q, RET_DK // 2, 1) * sin) * scale
        kr = k * cos + pltpu.roll(k, RET_DK // 2, 1) * sin
        dmat = tab_ref[0, h]
        q_decay = tab_ref[1, h]
        k_decay = tab_ref[2, h]
        g_chunk = tab_ref[3, h]
        scores = lax.dot_general(qr.astype(BF16), kr.astype(BF16),
                                 (((1,), (1,)), ((), ())),
                                 preferred_element_type=F32) * dmat
        intra = _dot(scores.astype(BF16), v)
        state = s_ref[h]
        cross = _dot((qr * q_decay).astype(BF16), state.astype(BF16))
        kv = _dot((kr * k_decay).T.astype(BF16), v)
        s_ref[h] = g_chunk * state + kv
        o = intra + cross
        if final:
            o = o + of_ref[:, sl]
            mu = jnp.mean(o, axis=-1, keepdims=True)
            var = jnp.mean(jnp.square(o - mu), axis=-1, keepdims=True)
            o = (o - mu) * lax.rsqrt(var + EPS)
            g = g_ref[:, sl]
            o_ref[:, sl] = (o * (g * jax.nn.sigmoid(g))).astype(o_ref.dtype)
        else:
            o_ref[:, sl] = o


def _retention_tables(log_gamma, reverse):
    pos = jnp.arange(CHUNK, dtype=F32)
    lg = log_gamma[:, None, None]
    i = pos[:, None]
    j = pos[None, :]
    if reverse:
        diff = j - i
        mask = diff > 0
        q_pow = CHUNK - pos
        k_pow = pos
    else:
        diff = i - j
        mask = diff >= 0
        q_pow = pos + 1.0
        k_pow = CHUNK - 1.0 - pos
    dmat = jnp.where(mask[None], jnp.exp(lg * jnp.maximum(diff, 0.0)[None]), 0.0)
    full = (RET_HEADS, CHUNK, RET_DK)
    q_decay = jnp.broadcast_to(jnp.exp(log_gamma[:, None] * q_pow)[:, :, None], full)
    k_decay = jnp.broadcast_to(jnp.exp(log_gamma[:, None] * k_pow)[:, :, None], full)
    g_chunk = jnp.broadcast_to(jnp.exp(log_gamma * CHUNK)[:, None, None], full)
    return jnp.stack([dmat, q_decay, k_decay, g_chunk])


def _retention_pass(proj, cos2, sin2, tables, n_ctx, reverse, o_fwd=None):
    t = proj.shape[0]
    nc = t // CHUNK
    ncc = n_ctx // CHUNK
    final = o_fwd is not None

    def sec(s):
        return pl.BlockSpec((CHUNK, BRANCH_W),
                            lambda st: (_chunk_order(st, nc, ncc, reverse), s))

    row = pl.BlockSpec((CHUNK, RET_DK), lambda st: (_chunk_order(st, nc, ncc, reverse), 0))
    out_spec = pl.BlockSpec((CHUNK, BRANCH_W),
                            lambda st: (_chunk_order(st, nc, ncc, reverse), 0))
    in_specs = [sec(0), sec(1), sec(2), row, row,
                pl.BlockSpec((4, RET_HEADS, CHUNK, RET_DK), lambda st: (0, 0, 0, 0))]
    args = [proj, proj, proj, cos2, sin2, tables]
    if final:
        in_specs += [out_spec, sec(3)]
        args += [o_fwd, proj]
    return pl.pallas_call(
        functools.partial(_retention_body, final=final),
        grid=(nc,),
        in_specs=in_specs,
        out_specs=out_spec,
        out_shape=jax.ShapeDtypeStruct((t, BRANCH_W), BF16 if final else F32),
        scratch_shapes=[pltpu.VMEM((RET_HEADS, RET_DK, RET_DK), F32)],
        compiler_params=_cparams(("arbitrary",)),
        name="retention_bwd" if final else "retention_fwd",
    )(*args)


def _s5_body(*refs, reverse, final):
    if final:
        (u_ref, wbr_ref, wbi_ref, a_ref, wcr_ref, wci_ref, yf_ref, wglu_ref,
         o_ref, xr_ref, xi_ref, bur_ref, bui_ref, str_ref, sti_ref, wgb_ref) = refs
    else:
        (u_ref, wbr_ref, wbi_ref, a_ref, wcr_ref, wci_ref, d_ref,
         o_ref, xr_ref, xi_ref, bur_ref, bui_ref, str_ref, sti_ref) = refs

    @pl.when(pl.program_id(0) == 0)
    def _():
        xr_ref[...] = jnp.zeros_like(xr_ref)
        xi_ref[...] = jnp.zeros_like(xi_ref)
        if final:
            wgb_ref[...] = wglu_ref[...].astype(BF16)

    u = u_ref[...]
    ub = u.astype(BF16)
    for r in range(S5_LANE_BLOCKS):
        ur = ub[:, r * LANES:(r + 1) * LANES]
        cols = slice(r * S5_BLOCK_STATE, (r + 1) * S5_BLOCK_STATE)
        bur_ref[:, cols] = _dot(ur, wbr_ref[r])
        bui_ref[:, cols] = _dot(ur, wbi_ref[r])

    a_re = a_ref[0:1, :]
    a_im = a_ref[1:2, :]

    def step(i, carry):
        x_re, x_im = carry
        t = (CHUNK - 1 - i) if reverse else i
        n_re = a_re * x_re - a_im * x_im + bur_ref[pl.ds(t, 1), :]
        n_im = a_re * x_im + a_im * x_re + bui_ref[pl.ds(t, 1), :]
        str_ref[pl.ds(t, 1), :] = n_re
        sti_ref[pl.ds(t, 1), :] = n_im
        return n_re, n_im

    x_re, x_im = lax.fori_loop(0, CHUNK, step, (xr_ref[...], xi_ref[...]), unroll=8)
    xr_ref[...] = x_re
    xi_ref[...] = x_im

    ys = []
    for r in range(S5_LANE_BLOCKS):
        cols = slice(r * S5_BLOCK_STATE, (r + 1) * S5_BLOCK_STATE)
        ys.append(_dot(str_ref[:, cols].astype(BF16), wcr_ref[r])
                  + _dot(sti_ref[:, cols].astype(BF16), wci_ref[r]))
    y = jnp.concatenate(ys, axis=1)
    if final:
        y = jax.nn.gelu(y + yf_ref[...])
        o_ref[...] = (y * jax.nn.sigmoid(_dot(y.astype(BF16), wgb_ref[...]))).astype(o_ref.dtype)
    else:
        o_ref[...] = y + d_ref[...] * u


def _s5_params(a_re, a_im, b_re, b_im, log_dt, c_re, c_im):
    dt = jnp.exp(log_dt)[:, None]
    mag = jnp.exp(dt * a_re)
    ang = dt * a_im
    ab_re, ab_im = mag * jnp.cos(ang), mag * jnp.sin(ang)
    nr, ni = ab_re - 1.0, ab_im
    den = a_re * a_re + a_im * a_im
    f_re = (nr * a_re + ni * a_im) / den
    f_im = (ni * a_re - nr * a_im) / den
    bb_re = f_re[..., None] * b_re - f_im[..., None] * b_im
    bb_im = f_re[..., None] * b_im + f_im[..., None] * b_re
    eye = jnp.eye(S5_GROUPS_PER_BLOCK, dtype=F32)

    def in_map(bb):
        bb4 = bb.reshape(S5_LANE_BLOCKS, S5_GROUPS_PER_BLOCK, S5_STATE, S5_GROUP)
        w = jnp.einsum('rgpc,gh->rgchp', bb4, eye)
        return w.reshape(S5_LANE_BLOCKS, LANES, S5_BLOCK_STATE).astype(BF16)

    def out_map(c):
        c4 = c.reshape(S5_LANE_BLOCKS, S5_GROUPS_PER_BLOCK, S5_GROUP, S5_STATE)
        w = jnp.einsum('rgcp,gh->rgphc', c4, eye)
        return w.reshape(S5_LANE_BLOCKS, S5_BLOCK_STATE, LANES).astype(BF16)

    a = jnp.stack([ab_re.reshape(-1), ab_im.reshape(-1)])
    return in_map(bb_re), in_map(bb_im), a, out_map(c_re), out_map(-c_im)


def _s5_pass(proj, params, n_ctx, reverse, d=None, y_fwd=None, w_glu=None):
    t = proj.shape[0]
    nc = t // CHUNK
    ncc = n_ctx // CHUNK
    final = y_fwd is not None
    wbr, wbi, a, wcr, wci = params

    def order(st):
        return _chunk_order(st, nc, ncc, reverse)

    def whole(x):
        nd = x.ndim
        return pl.BlockSpec(x.shape, lambda st: (0,) * nd)

    io_spec = pl.BlockSpec((CHUNK, BRANCH_W), lambda st: (order(st), 0))
    in_specs = [pl.BlockSpec((CHUNK, BRANCH_W), lambda st: (order(st), 4)),
                whole(wbr), whole(wbi), whole(a), whole(wcr), whole(wci)]
    args = [proj, wbr, wbi, a, wcr, wci]
    scratch = [pltpu.VMEM((1, S5_STATE_W), F32), pltpu.VMEM((1, S5_STATE_W), F32),
               pltpu.VMEM((CHUNK, S5_STATE_W), F32), pltpu.VMEM((CHUNK, S5_STATE_W), F32),
               pltpu.VMEM((CHUNK, S5_STATE_W), F32), pltpu.VMEM((CHUNK, S5_STATE_W), F32)]
    if final:
        in_specs += [io_spec, whole(w_glu)]
        args += [y_fwd, w_glu]
        scratch += [pltpu.VMEM((BRANCH_W, BRANCH_W), BF16)]
    else:
        d2 = d.reshape(1, BRANCH_W)
        in_specs += [whole(d2)]
        args += [d2]
    return pl.pallas_call(
        functools.partial(_s5_body, reverse=reverse, final=final),
        grid=(nc,),
        in_specs=in_specs,
        out_specs=io_spec,
        out_shape=jax.ShapeDtypeStruct((t, BRANCH_W), BF16 if final else F32),
        scratch_shapes=scratch,
        compiler_params=_cparams(("arbitrary",)),
        name="s5_bwd" if final else "s5_fwd",
    )(*args)


def _local_body(cx_ref, cb_ref, cc_ref, cxp_ref, ccp_ref, cxn_ref, ccn_ref, cw_ref,
                gu_ref, gv_ref, lng_ref, lnb_ref, ws_ref, bs_ref,
                oc_ref, og_ref, *, n_chunks, n_ctx_chunks):
    c = pl.program_id(0)
    seq_start = jnp.logical_or(c == 0, c == n_ctx_chunks)
    seq_end = jnp.logical_or(c == n_ctx_chunks - 1, c == n_chunks - 1)
    z = cc_ref[...] * cx_ref[...]
    z_prev = ccp_ref[SUBLANES - 1:SUBLANES, :] * cxp_ref[SUBLANES - 1:SUBLANES, :]
    z_next = ccn_ref[0:1, :] * cxn_ref[0:1, :]
    z_prev = jnp.where(seq_start, 0.0, z_prev)
    z_next = jnp.where(seq_end, 0.0, z_next)
    rows = lax.broadcasted_iota(jnp.int32, (CHUNK, 1), 0)
    z_up = jnp.where(rows == 0, z_prev, pltpu.roll(z, 1, 0))
    z_dn = jnp.where(rows == CHUNK - 1, z_next, pltpu.roll(z, CHUNK - 1, 0))
    y = cw_ref[0:1, :] * z_up + cw_ref[1:2, :] * z + cw_ref[2:3, :] * z_dn
    oc_ref[...] = (cb_ref[...] * y).astype(oc_ref.dtype)

    u = jax.nn.gelu(gu_ref[...])
    v = jax.nn.gelu(gv_ref[...])
    mu = jnp.mean(v, axis=-1, keepdims=True)
    var = jnp.mean(jnp.square(v - mu), axis=-1, keepdims=True)
    v = ((v - mu) * lax.rsqrt(var + EPS) * lng_ref[...] + lnb_ref[...]).astype(BF16)
    for g in range(CMLP_GROUPS):
        sl = slice(g * CMLP_GW, (g + 1) * CMLP_GW)
        mixed = _dot(ws_ref[g], v[:, sl]) + bs_ref[g]
        og_ref[:, sl] = (u[:, sl] * mixed).astype(og_ref.dtype)


def _local_mixers(proj, conv_w, ln_g, ln_b, ws, bs, n_ctx):
    t = proj.shape[0]
    nc = t // CHUNK
    ncc = n_ctx // CHUNK
    per = CHUNK // SUBLANES
    last = t // SUBLANES - 1

    def sec(s):
        return pl.BlockSpec((CHUNK, BRANCH_W), lambda c: (c, s))

    def prev(s):
        return pl.BlockSpec((SUBLANES, BRANCH_W), lambda c: (jnp.maximum(c * per - 1, 0), s))

    def nxt(s):
        return pl.BlockSpec((SUBLANES, BRANCH_W), lambda c: (jnp.minimum((c + 1) * per, last), s))

    def whole(x):
        nd = x.ndim
        return pl.BlockSpec(x.shape, lambda c: (0,) * nd)

    ws_b = ws.astype(BF16)
    bs_full = jnp.broadcast_to(bs[:, :, None], (CMLP_GROUPS, CHUNK, CMLP_GW))
    ln_g2 = ln_g.reshape(1, BRANCH_W)
    ln_b2 = ln_b.reshape(1, BRANCH_W)
    out_spec = pl.BlockSpec((CHUNK, BRANCH_W), lambda c: (c, 0))
    return pl.pallas_call(
        functools.partial(_local_body, n_chunks=nc, n_ctx_chunks=ncc),
        grid=(nc,),
        in_specs=[sec(5), sec(6), sec(7), prev(5), prev(7), nxt(5), nxt(7), whole(conv_w),
                  sec(8), sec(9), whole(ln_g2), whole(ln_b2), whole(ws_b), whole(bs_full)],
        out_specs=[out_spec, out_spec],
        out_shape=[jax.ShapeDtypeStruct((t, BRANCH_W), BF16),
                   jax.ShapeDtypeStruct((t, BRANCH_W), BF16)],
        compiler_params=_cparams(("parallel",)),
        name="local_mixers",
    )(proj, proj, proj, proj, proj, proj, proj, conv_w,
      proj, proj, ln_g2, ln_b2, ws_b, bs_full)


def _final_norm_body(x_ref, g_ref, o_ref):
    x = x_ref[...]
    o_ref[...] = x * lax.rsqrt(jnp.mean(x * x, axis=-1, keepdims=True) + EPS) * g_ref[...]


def _final_norm(x, g, n_ctx):
    t, d = x.shape
    tm = math.gcd(n_ctx, 512)
    skip = n_ctx // tm
    n_lat = t - n_ctx
    return pl.pallas_call(
        _final_norm_body,
        grid=(n_lat // tm,),
        in_specs=[pl.BlockSpec((tm, d), lambda i: (i + skip, 0)),
                  pl.BlockSpec((1, d), lambda i: (0, 0))],
        out_specs=pl.BlockSpec((tm, d), lambda i: (i, 0)),
        out_shape=jax.ShapeDtypeStruct((n_lat, d), F32),
        compiler_params=_cparams(("parallel",)),
        name="final_norm",
    )(x, g.reshape(1, d))


def _rope_tables(n_lat, n_ctx):
    rows = n_lat // GRID_W
    row = jnp.broadcast_to(jnp.arange(rows)[:, None], (rows, GRID_W)).reshape(-1).astype(F32)
    col = jnp.broadcast_to(jnp.arange(GRID_W)[None, :], (rows, GRID_W)).reshape(-1).astype(F32)
    nf = RET_DK // 4
    freqs = ROPE_BASE ** (-jnp.arange(nf, dtype=F32) / nf)
    ang = jnp.concatenate([row[:, None] * freqs, col[:, None] * freqs], axis=-1)
    cos, sin = jnp.cos(ang), jnp.sin(ang)
    cos2 = jnp.concatenate([cos, cos], axis=-1)
    sin2 = jnp.concatenate([-sin, sin], axis=-1)
    cos2 = jnp.concatenate([jnp.ones((n_ctx, RET_DK), F32), cos2], axis=0)
    sin2 = jnp.concatenate([jnp.zeros((n_ctx, RET_DK), F32), sin2], axis=0)
    return cos2, sin2


def kernel(x, c, ctx, c_ctx, ada_w, ada_b, norm1_g, norm2_g, w_in, ret_decay_logit, s5_a_re, s5_a_im, s5_b_re, s5_b_im, s5_c_re, s5_c_im, s5_d, s5_log_dt, s5_w_glu, conv_w, cmlp_ln_g, cmlp_ln_b, cmlp_ws, cmlp_bs, w_branch, w_merge, b_merge, w_out, ffn_w1, ffn_w3, ffn_w2, final_norm_g):
    batch, n_lat, d = x.shape
    assert batch == 1 and c.shape[0] == 1 and ctx.shape[0] == 1
    n_ctx = ctx.shape[1]
    assert n_ctx % CHUNK == 0 and n_lat % CHUNK == 0 and n_lat % GRID_W == 0
    depth = ada_w.shape[0]

    cos2, sin2 = _rope_tables(n_lat, n_ctx)
    cvec = jnp.zeros((SUBLANES, d), F32).at[0].set(c[0]).at[1].set(c_ctx)
    mod_all = _adaln(cvec, ada_w, ada_b)
    xs = jnp.concatenate([ctx[0], x[0]], axis=0)

    for l in range(depth):
        mod = mod_all[l, 0:2].reshape(2, 6, d)
        sh1, sc1, g1, sh2, sc2, g2 = [mod[:, i] for i in range(6)]
        h = _modulate(xs, norm1_g[l], sc1, sh1, n_ctx)
        proj = _matmul(h, w_in, l, 1024, F32)

        lg = jax.nn.log_sigmoid(ret_decay_logit[l].astype(F32))
        o_f = _retention_pass(proj, cos2, sin2, _retention_tables(lg[0], False), n_ctx, False)
        y_ret = _retention_pass(proj, cos2, sin2, _retention_tables(lg[1], True), n_ctx, True,
                                o_fwd=o_f)

        p_f = _s5_params(s5_a_re[l], s5_a_im[l], s5_b_re[l, 0], s5_b_im[l, 0], s5_log_dt[l, 0],
                         s5_c_re[l], s5_c_im[l])
        p_b = _s5_params(s5_a_re[l], s5_a_im[l], s5_b_re[l, 1], s5_b_im[l, 1], s5_log_dt[l, 1],
                         s5_c_re[l], s5_c_im[l])
        y_f = _s5_pass(proj, p_f, n_ctx, False, d=s5_d[l])
        y_s5 = _s5_pass(proj, p_b, n_ctx, True, y_fwd=y_f, w_glu=s5_w_glu[l])

        y_conv, y_gmlp = _local_mixers(proj, conv_w[l], cmlp_ln_g[l], cmlp_ln_b[l],
                                       cmlp_ws[l], cmlp_bs[l], n_ctx)

        merged = _merge(h, [y_ret, y_s5, y_conv, y_gmlp], w_merge, b_merge, w_branch, l, 512)
        xs = _matmul_residual(merged, w_out, l, xs, g1, n_ctx, 1024)

        h2 = _modulate(xs, norm2_g[l], sc2, sh2, n_ctx)
        act = _ffn_up(h2, ffn_w1, ffn_w3, l, 512)
        xs = _matmul_residual(act, ffn_w2, l, xs, g2, n_ctx, 512)

    return _final_norm(xs, final_norm_g, n_ctx)[None]
```

```python
import functools
import math

import jax
import jax.numpy as jnp
from jax import lax
from jax.experimental import pallas as pl
from jax.experimental.pallas import tpu as pltpu

F32 = jnp.float32
BF16 = jnp.bfloat16

D_MODEL = 2048
GRID_W = 64
BRANCH_W = 512
N_BRANCH = 4
N_SECTIONS = 10
RET_HEADS = 4
RET_DK = BRANCH_W // RET_HEADS
ROPE_BASE = 10000.0
S5_GROUP = 16
S5_GROUPS = BRANCH_W // S5_GROUP
S5_STATE = 64
CMLP_GROUPS = 4
CMLP_GW = BRANCH_W // CMLP_GROUPS
EPS = 1e-6

CHUNK = 128
SEQ_BLOCK = 256
LANES = 128
SUBLANES = 8
BF16_ROWS = 16
S5_LANE_BLOCKS = BRANCH_W // LANES
S5_GROUPS_PER_BLOCK = LANES // S5_GROUP
S5_BLOCK_STATE = S5_GROUPS_PER_BLOCK * S5_STATE
S5_STATE_W = S5_GROUPS * S5_STATE
MIB = 1024 * 1024
VMEM_BUDGET_MIB = 56

SEC_Q, SEC_K, SEC_V, SEC_G, SEC_U, SEC_CX, SEC_CB, SEC_CC, SEC_GU, SEC_GV = range(N_SECTIONS)


def _cparams(semantics, vmem_mib=VMEM_BUDGET_MIB):
    return pltpu.CompilerParams(dimension_semantics=semantics,
                                vmem_limit_bytes=vmem_mib * MIB)


def _dot(a, b):
    return jnp.dot(a, b, preferred_element_type=F32)


def _row_tile(t, cap=1056):
    for tm in (1056, 1024, 768, 528, 512, 384, 256, 128):
        if tm <= cap and t % tm == 0:
            return tm
    raise ValueError(f"unsupported token count {t}")


def _is_ctx_rows(tile_idx, tm, n_ctx):
    rows = tile_idx * tm + lax.broadcasted_iota(jnp.int32, (tm, 1), 0)
    return rows < n_ctx


def _weight_spec(block, index_map):
    return pl.BlockSpec(block, index_map, pipeline_mode=pl.Buffered(1))


def _whole(x):
    nd = x.ndim
    return pl.BlockSpec(x.shape, lambda *_: (0,) * nd)


def _adaln_body(c_ref, w_ref, b_ref, o_ref):
    c = c_ref[...]
    s = c * jax.nn.sigmoid(c)
    w = w_ref[...]
    s_hi = s.astype(BF16)
    s_lo = (s - s_hi.astype(F32)).astype(BF16)
    w_hi = w.astype(BF16)
    w_lo = (w - w_hi.astype(F32)).astype(BF16)
    acc = _dot(s_hi, w_hi) + _dot(s_lo, w_hi) + _dot(s_hi, w_lo)
    o_ref[...] = acc + b_ref[...]


def _adaln(cvec, ada_w, ada_b):
    depth, d, n = ada_w.shape
    tn = 1024
    return pl.pallas_call(
        _adaln_body,
        grid=(depth, n // tn),
        in_specs=[
            pl.BlockSpec((SUBLANES, d), lambda l, j: (0, 0)),
            pl.BlockSpec((None, d, tn), lambda l, j: (l, 0, j)),
            pl.BlockSpec((None, 1, tn), lambda l, j: (l, 0, j)),
        ],
        out_specs=pl.BlockSpec((None, SUBLANES, tn), lambda l, j: (l, 0, j)),
        out_shape=jax.ShapeDtypeStruct((depth, SUBLANES, n), F32),
        compiler_params=_cparams(("parallel", "parallel")),
        name="adaln",
    )(cvec, ada_w, ada_b.reshape(depth, 1, n))


def _modulate_body(x_ref, g_ref, sc_ref, sh_ref, o_ref, *, tm, n_ctx):
    base = pl.program_id(0) * tm
    g = g_ref[...]
    gain_lat = g * (1.0 + sc_ref[0:1, :])
    gain_ctx = g * (1.0 + sc_ref[1:2, :])
    shift_lat = sh_ref[0:1, :]
    shift_ctx = sh_ref[1:2, :]

    def slab(r, carry):
        start = pl.multiple_of(r * BF16_ROWS, BF16_ROWS)
        x = x_ref[pl.ds(start, BF16_ROWS), :]
        inv = lax.rsqrt(jnp.mean(x * x, axis=-1, keepdims=True) + EPS)
        is_ctx = (base + start) < n_ctx
        gain = jnp.where(is_ctx, gain_ctx, gain_lat)
        shift = jnp.where(is_ctx, shift_ctx, shift_lat)
        o_ref[pl.ds(start, BF16_ROWS), :] = (x * inv * gain + shift).astype(o_ref.dtype)
        return carry

    lax.fori_loop(0, tm // BF16_ROWS, slab, 0, unroll=3)


def _modulate(x, g, sc, sh, n_ctx):
    t, d = x.shape
    tm = _row_tile(t, cap=528)
    return pl.pallas_call(
        functools.partial(_modulate_body, tm=tm, n_ctx=n_ctx),
        grid=(t // tm,),
        in_specs=[
            pl.BlockSpec((tm, d), lambda i: (i, 0)),
            pl.BlockSpec((1, d), lambda i: (0, 0)),
            pl.BlockSpec((2, d), lambda i: (0, 0)),
            pl.BlockSpec((2, d), lambda i: (0, 0)),
        ],
        out_specs=pl.BlockSpec((tm, d), lambda i: (i, 0)),
        out_shape=jax.ShapeDtypeStruct((t, d), BF16),
        compiler_params=_cparams(("parallel",)),
        name="modulate",
    )(x, g.reshape(1, d), sc, sh)


def _mm_body(x_ref, w_ref, o_ref, wb_ref):
    @pl.when(pl.program_id(1) == 0)
    def _():
        wb_ref[...] = w_ref[...].astype(BF16)

    o_ref[...] = _dot(x_ref[...], wb_ref[...]).astype(o_ref.dtype)


def _matmul(x, w_stack, layer, tn, out_dtype):
    t, k = x.shape
    n = w_stack.shape[-1]
    tm = _row_tile(t)
    return pl.pallas_call(
        _mm_body,
        grid=(n // tn, t // tm),
        in_specs=[
            pl.BlockSpec((tm, k), lambda j, i: (i, 0)),
            _weight_spec((None, k, tn), lambda j, i: (layer, 0, j)),
        ],
        out_specs=pl.BlockSpec((tm, tn), lambda j, i: (i, j)),
        out_shape=jax.ShapeDtypeStruct((t, n), out_dtype),
        scratch_shapes=[pltpu.VMEM((k, tn), BF16)],
        compiler_params=_cparams(("arbitrary", "arbitrary")),
        name="matmul",
    )(x, w_stack)


def _mm_residual_body(a_ref, w_ref, x_ref, g_ref, o_ref, wb_ref, *, tm, n_ctx):
    @pl.when(pl.program_id(1) == 0)
    def _():
        wb_ref[...] = w_ref[...].astype(BF16)

    y = _dot(a_ref[...], wb_ref[...])
    is_ctx = _is_ctx_rows(pl.program_id(1), tm, n_ctx)
    gate = jnp.where(is_ctx, g_ref[1:2, :], g_ref[0:1, :])
    o_ref[...] = x_ref[...] + gate * y


def _matmul_residual(a, w_stack, layer, x, gate, n_ctx, tn):
    t, k = a.shape
    n = w_stack.shape[-1]
    tm = _row_tile(t)
    return pl.pallas_call(
        functools.partial(_mm_residual_body, tm=tm, n_ctx=n_ctx),
        grid=(n // tn, t // tm),
        in_specs=[
            pl.BlockSpec((tm, k), lambda j, i: (i, 0)),
            _weight_spec((None, k, tn), lambda j, i: (layer, 0, j)),
            pl.BlockSpec((tm, tn), lambda j, i: (i, j)),
            pl.BlockSpec((2, tn), lambda j, i: (0, j)),
        ],
        out_specs=pl.BlockSpec((tm, tn), lambda j, i: (i, j)),
        out_shape=jax.ShapeDtypeStruct((t, n), F32),
        scratch_shapes=[pltpu.VMEM((k, tn), BF16)],
        compiler_params=_cparams(("arbitrary", "arbitrary")),
        name="matmul_residual",
    )(a, w_stack, x, gate)


def _ffn_up_body(h_ref, w1_ref, w3_ref, o_ref, w1b_ref, w3b_ref):
    @pl.when(pl.program_id(1) == 0)
    def _():
        w1b_ref[...] = w1_ref[...].astype(BF16)
        w3b_ref[...] = w3_ref[...].astype(BF16)

    h = h_ref[...]
    a = _dot(h, w1b_ref[...])
    b = _dot(h, w3b_ref[...])
    o_ref[...] = (a * jax.nn.sigmoid(a) * b).astype(o_ref.dtype)


def _ffn_up(h, w1_stack, w3_stack, layer, tn):
    t, k = h.shape
    n = w1_stack.shape[-1]
    tm = _row_tile(t)
    wspec = _weight_spec((None, k, tn), lambda j, i: (layer, 0, j))
    return pl.pallas_call(
        _ffn_up_body,
        grid=(n // tn, t // tm),
        in_specs=[pl.BlockSpec((tm, k), lambda j, i: (i, 0)), wspec, wspec],
        out_specs=pl.BlockSpec((tm, tn), lambda j, i: (i, j)),
        out_shape=jax.ShapeDtypeStruct((t, n), BF16),
        scratch_shapes=[pltpu.VMEM((k, tn), BF16), pltpu.VMEM((k, tn), BF16)],
        compiler_params=_cparams(("arbitrary", "arbitrary")),
        name="ffn_up",
    )(h, w1_stack, w3_stack)


def _merge_body(*refs):
    nb = N_BRANCH
    h_ref = refs[0]
    o_refs = refs[1:1 + nb]
    wm_refs = refs[1 + nb:1 + 2 * nb]
    bm_refs = refs[1 + 2 * nb:1 + 3 * nb]
    wb_refs = refs[1 + 3 * nb:1 + 4 * nb]
    out_ref, wmb_ref, wbb_ref = refs[1 + 4 * nb:]

    @pl.when(pl.program_id(1) == 0)
    def _():
        for k in range(nb):
            wmb_ref[k] = wm_refs[k][...].astype(BF16)
            wbb_ref[k] = wb_refs[k][...].astype(BF16)

    h = h_ref[...]
    acc = None
    for k in range(nb):
        gate = jax.nn.sigmoid(_dot(h, wmb_ref[k]) + bm_refs[k][...])
        term = gate * _dot(o_refs[k][...], wbb_ref[k])
        acc = term if acc is None else acc + term
    out_ref[...] = acc.astype(out_ref.dtype)


def _merge(h, outs, w_merge, b_merge, w_branch, layer, tn):
    t, d = h.shape
    bw = outs[0].shape[1]
    depth = w_merge.shape[0]
    tm = _row_tile(t, cap=528)
    nt = d // tn
    b_merge3 = b_merge.reshape(depth, 1, N_BRANCH * d)
    in_specs = [pl.BlockSpec((tm, d), lambda j, i: (i, 0))]
    in_specs += [pl.BlockSpec((tm, bw), lambda j, i: (i, 0)) for _ in range(N_BRANCH)]
    in_specs += [_weight_spec((None, d, tn), lambda j, i, k=k: (layer, 0, k * nt + j))
                 for k in range(N_BRANCH)]
    in_specs += [pl.BlockSpec((None, 1, tn), lambda j, i, k=k: (layer, 0, k * nt + j))
                 for k in range(N_BRANCH)]
    in_specs += [_weight_spec((None, None, bw, tn), lambda j, i, k=k: (layer, k, 0, j))
                 for k in range(N_BRANCH)]
    return pl.pallas_call(
        _merge_body,
        grid=(nt, t // tm),
        in_specs=in_specs,
        out_specs=pl.BlockSpec((tm, tn), lambda j, i: (i, j)),
        out_shape=jax.ShapeDtypeStruct((t, d), BF16),
        scratch_shapes=[pltpu.VMEM((N_BRANCH, d, tn), BF16),
                        pltpu.VMEM((N_BRANCH, bw, tn), BF16)],
        compiler_params=_cparams(("arbitrary", "arbitrary")),
        name="merge",
    )(h, *outs, *([w_merge] * N_BRANCH), *([b_merge3] * N_BRANCH), *([w_branch] * N_BRANCH))


def _block_order(step, n_blocks, n_ctx_blocks, reverse):
    if not reverse:
        return step
    return jnp.where(step < n_ctx_blocks, n_ctx_blocks - 1 - step,
                     n_blocks + n_ctx_blocks - 1 - step)


def _retention_body(qf_ref, kf_ref, vf_ref, cosf_ref, sinf_ref,
                    qb_ref, kb_ref, vb_ref, cosb_ref, sinb_ref, tab_ref,
                    of_ref, ob_ref, s_ref):
    @pl.when(pl.program_id(0) == 0)
    def _():
        s_ref[...] = jnp.zeros_like(s_ref)

    scale = RET_DK ** -0.5
    chunks = SEQ_BLOCK // CHUNK
    dirs = ((qf_ref, kf_ref, vf_ref, cosf_ref, sinf_ref, of_ref, range(chunks)),
            (qb_ref, kb_ref, vb_ref, cosb_ref, sinb_ref, ob_ref, range(chunks - 1, -1, -1)))
    for d, (q_ref, k_ref, v_ref, cos_ref, sin_ref, o_ref, order) in enumerate(dirs):
        for h in range(RET_HEADS):
            sl = slice(h * RET_DK, (h + 1) * RET_DK)
            dmat = tab_ref[d, 0, h]
            q_decay = tab_ref[d, 1, h]
            k_decay = tab_ref[d, 2, h]
            g_chunk = tab_ref[d, 3, h]
            state = s_ref[d, h]
            for ci in order:
                rows = slice(ci * CHUNK, (ci + 1) * CHUNK)
                cos = cos_ref[rows, :]
                sin = sin_ref[rows, :]
                q = q_ref[rows, sl]
                k = k_ref[rows, sl]
                v = v_ref[rows, sl].astype(BF16)
                qr = (q * cos + pltpu.roll(q, RET_DK // 2, 1) * sin) * scale
                kr = k * cos + pltpu.roll(k, RET_DK // 2, 1) * sin
                scores = lax.dot_general(qr.astype(BF16), kr.astype(BF16),
                                         (((1,), (1,)), ((), ())),
                                         preferred_element_type=F32) * dmat
                intra = _dot(scores.astype(BF16), v)
                cross = _dot((qr * q_decay).astype(BF16), state.astype(BF16))
                kv = _dot((kr * k_decay).T.astype(BF16), v)
                o_ref[rows, sl] = intra + cross
                state = g_chunk * state + kv
            s_ref[d, h] = state


def _retention_tables(log_gamma):
    pos = jnp.arange(CHUNK, dtype=F32)
    i = pos[:, None]
    j = pos[None, :]
    full = log_gamma.shape[:1] + (RET_HEADS, CHUNK, RET_DK)

    def one_dir(lg, diff, mask, q_pow, k_pow):
        dmat = jnp.where(mask, jnp.exp(lg[:, :, None, None] * jnp.maximum(diff, 0.0)), 0.0)
        q_decay = jnp.broadcast_to(jnp.exp(lg[:, :, None] * q_pow)[..., None], full)
        k_decay = jnp.broadcast_to(jnp.exp(lg[:, :, None] * k_pow)[..., None], full)
        g_chunk = jnp.broadcast_to(jnp.exp(lg * CHUNK)[:, :, None, None], full)
        return jnp.stack([dmat, q_decay, k_decay, g_chunk], axis=1)

    fwd = one_dir(log_gamma[:, 0], i - j, (i - j) >= 0, pos + 1.0, CHUNK - 1.0 - pos)
    bwd = one_dir(log_gamma[:, 1], j - i, (j - i) > 0, CHUNK - pos, pos)
    return jnp.stack([fwd, bwd], axis=1)


def _retention(proj, cos2, sin2, tables, layer, n_ctx):
    t = proj.shape[0]
    nb = t // SEQ_BLOCK
    nbc = n_ctx // SEQ_BLOCK

    def specs(reverse):
        def blk(st):
            return _block_order(st, nb, nbc, reverse)
        sec = [pl.BlockSpec((SEQ_BLOCK, BRANCH_W), lambda st, s=s: (blk(st), s))
               for s in (SEC_Q, SEC_K, SEC_V)]
        row = pl.BlockSpec((SEQ_BLOCK, RET_DK), lambda st: (blk(st), 0))
        out = pl.BlockSpec((SEQ_BLOCK, BRANCH_W), lambda st: (blk(st), 0))
        return sec + [row, row], out

    in_f, out_f = specs(False)
    in_b, out_b = specs(True)
    tab_spec = pl.BlockSpec((None, 2, 4, RET_HEADS, CHUNK, RET_DK),
                            lambda st: (layer, 0, 0, 0, 0, 0))
    shape = jax.ShapeDtypeStruct((t, BRANCH_W), F32)
    return pl.pallas_call(
        _retention_body,
        grid=(nb,),
        in_specs=in_f + in_b + [tab_spec],
        out_specs=[out_f, out_b],
        out_shape=[shape, shape],
        scratch_shapes=[pltpu.VMEM((2, RET_HEADS, RET_DK, RET_DK), F32)],
        compiler_params=_cparams(("arbitrary",)),
        name="retention",
    )(proj, proj, proj, cos2, sin2, proj, proj, proj, cos2, sin2, tables)


def _s5_body(uf_ref, ub_ref, wb_ref, a_ref, wc_ref, d_ref, yf_ref, yb_ref,
             x_ref, bu_ref, st_ref):
    @pl.when(pl.program_id(0) == 0)
    def _():
        x_ref[...] = jnp.zeros_like(x_ref)

    for d, u_ref in enumerate((uf_ref, ub_ref)):
        ub = u_ref[...].astype(BF16)
        for r in range(S5_LANE_BLOCKS):
            bu = _dot(ub[:, r * LANES:(r + 1) * LANES], wb_ref[d, r])
            cols = slice(r * S5_BLOCK_STATE, (r + 1) * S5_BLOCK_STATE)
            bu_ref[2 * d, :, cols] = bu[:, :S5_BLOCK_STATE]
            bu_ref[2 * d + 1, :, cols] = bu[:, S5_BLOCK_STATE:]

    af_re, af_im = a_ref[0, 0:1, :], a_ref[0, 1:2, :]
    ab_re, ab_im = a_ref[1, 0:1, :], a_ref[1, 1:2, :]

    def step(i, carry):
        f_re, f_im, b_re, b_im = carry
        tf = pl.ds(i, 1)
        tb = pl.ds(SEQ_BLOCK - 1 - i, 1)
        nf_re = af_re * f_re - af_im * f_im + bu_ref[0, tf, :]
        nf_im = af_re * f_im + af_im * f_re + bu_ref[1, tf, :]
        nb_re = ab_re * b_re - ab_im * b_im + bu_ref[2, tb, :]
        nb_im = ab_re * b_im + ab_im * b_re + bu_ref[3, tb, :]
        st_ref[0, tf, :] = nf_re
        st_ref[1, tf, :] = nf_im
        st_ref[2, tb, :] = nb_re
        st_ref[3, tb, :] = nb_im
        return nf_re, nf_im, nb_re, nb_im

    carry = lax.fori_loop(0, SEQ_BLOCK, step,
                          (x_ref[0:1, :], x_ref[1:2, :], x_ref[2:3, :], x_ref[3:4, :]),
                          unroll=8)
    for n, val in enumerate(carry):
        x_ref[n:n + 1, :] = val

    def readout(d):
        ys = []
        for r in range(S5_LANE_BLOCKS):
            cols = slice(r * S5_BLOCK_STATE, (r + 1) * S5_BLOCK_STATE)
            st = jnp.concatenate([st_ref[2 * d, :, cols], st_ref[2 * d + 1, :, cols]],
                                 axis=1).astype(BF16)
            ys.append(_dot(st, wc_ref[r]))
        return jnp.concatenate(ys, axis=1)

    yf_ref[...] = readout(0) + d_ref[...] * uf_ref[...]
    yb_ref[...] = readout(1)


def _s5_prepare(a_re, a_im, b_re, b_im, log_dt, c_re, c_im):
    depth = a_re.shape[0]
    a_re = a_re[:, None]
    a_im = a_im[:, None]
    dt = jnp.exp(log_dt)[..., None]
    mag = jnp.exp(dt * a_re)
    ang = dt * a_im
    ab_re, ab_im = mag * jnp.cos(ang), mag * jnp.sin(ang)
    nr, ni = ab_re - 1.0, ab_im
    den = a_re * a_re + a_im * a_im
    f_re = (nr * a_re + ni * a_im) / den
    f_im = (ni * a_re - nr * a_im) / den
    bb_re = f_re[..., None] * b_re - f_im[..., None] * b_im
    bb_im = f_re[..., None] * b_im + f_im[..., None] * b_re
    eye = jnp.eye(S5_GROUPS_PER_BLOCK, dtype=F32)

    def in_map(bb):
        bb6 = bb.reshape(depth, 2, S5_LANE_BLOCKS, S5_GROUPS_PER_BLOCK, S5_STATE, S5_GROUP)
        w = jnp.einsum('ldrgpc,gh->ldrgchp', bb6, eye)
        return w.reshape(depth, 2, S5_LANE_BLOCKS, LANES, S5_BLOCK_STATE)

    def out_map(c):
        c5 = c.reshape(depth, S5_LANE_BLOCKS, S5_GROUPS_PER_BLOCK, S5_GROUP, S5_STATE)
        w = jnp.einsum('lrgcp,gh->lrgphc', c5, eye)
        return w.reshape(depth, S5_LANE_BLOCKS, S5_BLOCK_STATE, LANES)

    wb = jnp.concatenate([in_map(bb_re), in_map(bb_im)], axis=-1).astype(BF16)
    a = jnp.stack([ab_re.reshape(depth, 2, S5_STATE_W), ab_im.reshape(depth, 2, S5_STATE_W)],
                  axis=2)
    wc = jnp.concatenate([out_map(c_re), out_map(-c_im)], axis=-2).astype(BF16)
    return wb, a, wc


def _s5(proj, wb, a, wc, s5_d, layer, n_ctx):
    t = proj.shape[0]
    nb = t // SEQ_BLOCK
    nbc = n_ctx // SEQ_BLOCK
    depth = s5_d.shape[0]

    def io(reverse, sec):
        return pl.BlockSpec((SEQ_BLOCK, BRANCH_W),
                            lambda st: (_block_order(st, nb, nbc, reverse), sec))

    shape = jax.ShapeDtypeStruct((t, BRANCH_W), F32)
    planes = pltpu.VMEM((4, SEQ_BLOCK, S5_STATE_W), F32)
    return pl.pallas_call(
        _s5_body,
        grid=(nb,),
        in_specs=[io(False, SEC_U), io(True, SEC_U),
                  pl.BlockSpec((None,) + wb.shape[1:], lambda st: (layer, 0, 0, 0, 0)),
                  pl.BlockSpec((None,) + a.shape[1:], lambda st: (layer, 0, 0, 0)),
                  pl.BlockSpec((None,) + wc.shape[1:], lambda st: (layer, 0, 0, 0)),
                  pl.BlockSpec((None, 1, BRANCH_W), lambda st: (layer, 0, 0))],
        out_specs=[io(False, 0), io(True, 0)],
        out_shape=[shape, shape],
        scratch_shapes=[pltpu.VMEM((4, S5_STATE_W), F32), planes, planes],
        compiler_params=_cparams(("arbitrary",)),
        name="s5",
    )(proj, proj, wb, a, wc, s5_d.reshape(depth, 1, BRANCH_W))


def _local_body(of_ref, ob_ref, g_ref, yf_ref, yb_ref, wglu_ref,
                cx_ref, cb_ref, cc_ref, cxp_ref, ccp_ref, cxn_ref, ccn_ref, cw_ref,
                gu_ref, gv_ref, lng_ref, lnb_ref, ws_ref, bs_ref,
                oret_ref, os5_ref, oc_ref, og_ref, *, n_blocks, n_ctx_blocks):
    for h in range(RET_HEADS):
        sl = slice(h * RET_DK, (h + 1) * RET_DK)
        o = of_ref[:, sl] + ob_ref[:, sl]
        mu = jnp.mean(o, axis=-1, keepdims=True)
        var = jnp.mean(jnp.square(o - mu), axis=-1, keepdims=True)
        o = (o - mu) * lax.rsqrt(var + EPS)
        g = g_ref[:, sl]
        oret_ref[:, sl] = (o * (g * jax.nn.sigmoid(g))).astype(oret_ref.dtype)

    y = jax.nn.gelu(yf_ref[...] + yb_ref[...])
    os5_ref[...] = (y * jax.nn.sigmoid(_dot(y.astype(BF16), wglu_ref[...]))).astype(os5_ref.dtype)

    c = pl.program_id(0)
    seq_start = jnp.logical_or(c == 0, c == n_ctx_blocks)
    seq_end = jnp.logical_or(c == n_ctx_blocks - 1, c == n_blocks - 1)
    z = cc_ref[...] * cx_ref[...]
    z_prev = ccp_ref[SUBLANES - 1:SUBLANES, :] * cxp_ref[SUBLANES - 1:SUBLANES, :]
    z_next = ccn_ref[0:1, :] * cxn_ref[0:1, :]
    z_prev = jnp.where(seq_start, 0.0, z_prev)
    z_next = jnp.where(seq_end, 0.0, z_next)
    rows = lax.broadcasted_iota(jnp.int32, (SEQ_BLOCK, 1), 0)
    z_up = jnp.where(rows == 0, z_prev, pltpu.roll(z, 1, 0))
    z_dn = jnp.where(rows == SEQ_BLOCK - 1, z_next, pltpu.roll(z, SEQ_BLOCK - 1, 0))
    yc = cw_ref[0:1, :] * z_up + cw_ref[1:2, :] * z + cw_ref[2:3, :] * z_dn
    oc_ref[...] = (cb_ref[...] * yc).astype(oc_ref.dtype)

    u = jax.nn.gelu(gu_ref[...])
    v = jax.nn.gelu(gv_ref[...])
    mu = jnp.mean(v, axis=-1, keepdims=True)
    var = jnp.mean(jnp.square(v - mu), axis=-1, keepdims=True)
    v = ((v - mu) * lax.rsqrt(var + EPS) * lng_ref[...] + lnb_ref[...]).astype(BF16)
    for ci in range(SEQ_BLOCK // CHUNK):
        rws = slice(ci * CHUNK, (ci + 1) * CHUNK)
        for g in range(CMLP_GROUPS):
            sl = slice(g * CMLP_GW, (g + 1) * CMLP_GW)
            mixed = _dot(ws_ref[g], v[rws, sl]) + bs_ref[g]
            og_ref[rws, sl] = (u[rws, sl] * mixed).astype(og_ref.dtype)


def _local_mixers(proj, o_f, o_b, y_f, y_b, wglu_b, conv_w, ln_g, ln_b, ws_b, bs_full,
                  layer, n_ctx):
    t = proj.shape[0]
    nb = t // SEQ_BLOCK
    nbc = n_ctx // SEQ_BLOCK
    per = SEQ_BLOCK // SUBLANES
    last = t // SUBLANES - 1
    depth = conv_w.shape[0]

    def sec(s):
        return pl.BlockSpec((SEQ_BLOCK, BRANCH_W), lambda c: (c, s))

    def prev(s):
        return pl.BlockSpec((SUBLANES, BRANCH_W), lambda c: (jnp.maximum(c * per - 1, 0), s))

    def nxt(s):
        return pl.BlockSpec((SUBLANES, BRANCH_W), lambda c: (jnp.minimum((c + 1) * per, last), s))

    def layer_spec(x):
        nd = x.ndim - 1
        return pl.BlockSpec((None,) + x.shape[1:], lambda c: (layer,) + (0,) * nd)

    ln_g3 = ln_g.reshape(depth, 1, BRANCH_W)
    ln_b3 = ln_b.reshape(depth, 1, BRANCH_W)
    blk = sec(0)
    out_shape = jax.ShapeDtypeStruct((t, BRANCH_W), BF16)
    return pl.pallas_call(
        functools.partial(_local_body, n_blocks=nb, n_ctx_blocks=nbc),
        grid=(nb,),
        in_specs=[blk, blk, sec(SEC_G), blk, blk, layer_spec(wglu_b),
                  sec(SEC_CX), sec(SEC_CB), sec(SEC_CC),
                  prev(SEC_CX), prev(SEC_CC), nxt(SEC_CX), nxt(SEC_CC), layer_spec(conv_w),
                  sec(SEC_GU), sec(SEC_GV), layer_spec(ln_g3), layer_spec(ln_b3),
                  layer_spec(ws_b), layer_spec(bs_full)],
        out_specs=[blk, blk, blk, blk],
        out_shape=[out_shape] * 4,
        compiler_params=_cparams(("parallel",)),
        name="local_mixers",
    )(o_f, o_b, proj, y_f, y_b, wglu_b,
      proj, proj, proj, proj, proj, proj, proj, conv_w,
      proj, proj, ln_g3, ln_b3, ws_b, bs_full)


def _final_norm_body(x_ref, g_ref, o_ref, *, tm):
    g = g_ref[...]

    def slab(r, carry):
        start = pl.multiple_of(r * BF16_ROWS, BF16_ROWS)
        x = x_ref[pl.ds(start, BF16_ROWS), :]
        inv = lax.rsqrt(jnp.mean(x * x, axis=-1, keepdims=True) + EPS)
        o_ref[pl.ds(start, BF16_ROWS), :] = x * inv * g
        return carry

    lax.fori_loop(0, tm // BF16_ROWS, slab, 0, unroll=2)


def _final_norm(x, g, n_ctx):
    t, d = x.shape
    tm = math.gcd(n_ctx, 512)
    skip = n_ctx // tm
    n_lat = t - n_ctx
    return pl.pallas_call(
        functools.partial(_final_norm_body, tm=tm),
        grid=(n_lat // tm,),
        in_specs=[pl.BlockSpec((tm, d), lambda i: (i + skip, 0)),
                  pl.BlockSpec((1, d), lambda i: (0, 0))],
        out_specs=pl.BlockSpec((tm, d), lambda i: (i, 0)),
        out_shape=jax.ShapeDtypeStruct((n_lat, d), F32),
        compiler_params=_cparams(("parallel",)),
        name="final_norm",
    )(x, g.reshape(1, d))


def _rope_tables(n_lat, n_ctx):
    rows = n_lat // GRID_W
    row = jnp.broadcast_to(jnp.arange(rows)[:, None], (rows, GRID_W)).reshape(-1).astype(F32)
    col = jnp.broadcast_to(jnp.arange(GRID_W)[None, :], (rows, GRID_W)).reshape(-1).astype(F32)
    nf = RET_DK // 4
    freqs = ROPE_BASE ** (-jnp.arange(nf, dtype=F32) / nf)
    ang = jnp.concatenate([row[:, None] * freqs, col[:, None] * freqs], axis=-1)
    cos, sin = jnp.cos(ang), jnp.sin(ang)
    cos2 = jnp.concatenate([cos, cos], axis=-1)
    sin2 = jnp.concatenate([-sin, sin], axis=-1)
    cos2 = jnp.concatenate([jnp.ones((n_ctx, RET_DK), F32), cos2], axis=0)
    sin2 = jnp.concatenate([jnp.zeros((n_ctx, RET_DK), F32), sin2], axis=0)
    return cos2, sin2


def kernel(x, c, ctx, c_ctx, ada_w, ada_b, norm1_g, norm2_g, w_in, ret_decay_logit, s5_a_re, s5_a_im, s5_b_re, s5_b_im, s5_c_re, s5_c_im, s5_d, s5_log_dt, s5_w_glu, conv_w, cmlp_ln_g, cmlp_ln_b, cmlp_ws, cmlp_bs, w_branch, w_merge, b_merge, w_out, ffn_w1, ffn_w3, ffn_w2, final_norm_g):
    batch, n_lat, d = x.shape
    assert batch == 1 and c.shape[0] == 1 and ctx.shape[0] == 1
    n_ctx = ctx.shape[1]
    assert n_ctx % SEQ_BLOCK == 0 and n_lat % SEQ_BLOCK == 0 and n_lat % GRID_W == 0
    depth = ada_w.shape[0]

    cos2, sin2 = _rope_tables(n_lat, n_ctx)
    ret_tables = _retention_tables(jax.nn.log_sigmoid(ret_decay_logit.astype(F32)))
    s5_wb, s5_a, s5_wc = _s5_prepare(s5_a_re, s5_a_im, s5_b_re, s5_b_im, s5_log_dt,
                                     s5_c_re, s5_c_im)
    wglu_b = s5_w_glu.astype(BF16)
    ws_b = cmlp_ws.astype(BF16)
    bs_full = jnp.broadcast_to(cmlp_bs[..., None], cmlp_bs.shape + (CMLP_GW,))

    cvec = jnp.zeros((SUBLANES, d), F32).at[0].set(c[0]).at[1].set(c_ctx)
    mod_all = _adaln(cvec, ada_w, ada_b)
    mod_all = mod_all[:, 0:2].reshape(depth, 2, 6, d)
    xs = jnp.concatenate([ctx[0], x[0]], axis=0)

    for l in range(depth):
        sh1, sc1, g1, sh2, sc2, g2 = [mod_all[l, :, i] for i in range(6)]
        h = _modulate(xs, norm1_g[l], sc1, sh1, n_ctx)
        proj = _matmul(h, w_in, l, 1024, F32)

        o_f, o_b = _retention(proj, cos2, sin2, ret_tables, l, n_ctx)
        y_f, y_b = _s5(proj, s5_wb, s5_a, s5_wc, s5_d, l, n_ctx)
        outs = _local_mixers(proj, o_f, o_b, y_f, y_b, wglu_b, conv_w, cmlp_ln_g, cmlp_ln_b,
                             ws_b, bs_full, l, n_ctx)

        merged = _merge(h, outs, w_merge, b_merge, w_branch, l, 512)
        xs = _matmul_residual(merged, w_out, l, xs, g1, n_ctx, 1024)

        h2 = _modulate(xs, norm2_g[l], sc2, sh2, n_ctx)
        act = _ffn_up(h2, ffn_w1, ffn_w3, l, 512)
        xs = _matmul_residual(act, ffn_w2, l, xs, g2, n_ctx, 512)

    return _final_norm(xs, final_norm_g, n_ctx)[None]
```

```python
import functools
import math

import jax
import jax.numpy as jnp
from jax import lax
from jax.experimental import pallas as pl
from jax.experimental.pallas import tpu as pltpu

F32 = jnp.float32
BF16 = jnp.bfloat16

D_MODEL = 2048
GRID_W = 64
BRANCH_W = 512
N_BRANCH = 4
N_SECTIONS = 10
RET_HEADS = 4
RET_DK = BRANCH_W // RET_HEADS
ROPE_BASE = 10000.0
S5_GROUP = 16
S5_GROUPS = BRANCH_W // S5_GROUP
S5_STATE = 64
CMLP_GROUPS = 4
CMLP_GW = BRANCH_W // CMLP_GROUPS
EPS = 1e-6

CHUNK = 128
SEQ_BLOCK = 256
LANES = 128
SUBLANES = 8
BF16_ROWS = 16
S5_LANE_BLOCKS = BRANCH_W // LANES
S5_GROUPS_PER_BLOCK = LANES // S5_GROUP
S5_BLOCK_STATE = S5_GROUPS_PER_BLOCK * S5_STATE
S5_STATE_W = S5_GROUPS * S5_STATE
MIB = 1024 * 1024
VMEM_BUDGET_MIB = 56

SEC_Q, SEC_K, SEC_V, SEC_G, SEC_U, SEC_CX, SEC_CB, SEC_CC, SEC_GU, SEC_GV = range(N_SECTIONS)


def _cparams(semantics, vmem_mib=VMEM_BUDGET_MIB):
    return pltpu.CompilerParams(dimension_semantics=semantics,
                                vmem_limit_bytes=vmem_mib * MIB)


def _dot(a, b):
    return jnp.dot(a, b, preferred_element_type=F32)


def _row_tile(t, cap=1056):
    for tm in (1056, 1024, 768, 704, 528, 512, 384, 352, 256, 128):
        if tm <= cap and t % tm == 0:
            return tm
    raise ValueError(f"unsupported token count {t}")


def _is_ctx_rows(tile_idx, tm, n_ctx):
    rows = tile_idx * tm + lax.broadcasted_iota(jnp.int32, (tm, 1), 0)
    return rows < n_ctx


def _weight_spec(block, index_map):
    return pl.BlockSpec(block, index_map, pipeline_mode=pl.Buffered(1))


def _whole(x):
    nd = x.ndim
    return pl.BlockSpec(x.shape, lambda *_: (0,) * nd)


def _adaln_body(c_ref, w_ref, b_ref, o_ref):
    c = c_ref[...]
    s = c * jax.nn.sigmoid(c)
    w = w_ref[...]
    s_hi = s.astype(BF16)
    s_lo = (s - s_hi.astype(F32)).astype(BF16)
    w_hi = w.astype(BF16)
    w_lo = (w - w_hi.astype(F32)).astype(BF16)
    acc = _dot(s_hi, w_hi) + _dot(s_lo, w_hi) + _dot(s_hi, w_lo)
    o_ref[...] = acc + b_ref[...]


def _adaln(cvec, ada_w, ada_b):
    depth, d, n = ada_w.shape
    tn = 1024
    return pl.pallas_call(
        _adaln_body,
        grid=(depth, n // tn),
        in_specs=[
            pl.BlockSpec((SUBLANES, d), lambda l, j: (0, 0)),
            pl.BlockSpec((None, d, tn), lambda l, j: (l, 0, j)),
            pl.BlockSpec((None, 1, tn), lambda l, j: (l, 0, j)),
        ],
        out_specs=pl.BlockSpec((None, SUBLANES, tn), lambda l, j: (l, 0, j)),
        out_shape=jax.ShapeDtypeStruct((depth, SUBLANES, n), F32),
        compiler_params=_cparams(("parallel", "parallel")),
        name="adaln",
    )(cvec, ada_w, ada_b.reshape(depth, 1, n))


def _modulate_body(x_ref, g_ref, sc_ref, sh_ref, o_ref, *, tm, n_ctx):
    base = pl.program_id(0) * tm
    g = g_ref[...]
    gain_lat = g * (1.0 + sc_ref[0:1, :])
    gain_ctx = g * (1.0 + sc_ref[1:2, :])
    shift_lat = sh_ref[0:1, :]
    shift_ctx = sh_ref[1:2, :]

    def slab(r, carry):
        start = pl.multiple_of(r * BF16_ROWS, BF16_ROWS)
        x = x_ref[pl.ds(start, BF16_ROWS), :]
        inv = lax.rsqrt(jnp.mean(x * x, axis=-1, keepdims=True) + EPS)
        is_ctx = (base + start) < n_ctx
        gain = jnp.where(is_ctx, gain_ctx, gain_lat)
        shift = jnp.where(is_ctx, shift_ctx, shift_lat)
        o_ref[pl.ds(start, BF16_ROWS), :] = (x * inv * gain + shift).astype(o_ref.dtype)
        return carry

    lax.fori_loop(0, tm // BF16_ROWS, slab, 0, unroll=3)


def _modulate(x, g, sc, sh, n_ctx):
    t, d = x.shape
    tm = _row_tile(t, cap=528)
    return pl.pallas_call(
        functools.partial(_modulate_body, tm=tm, n_ctx=n_ctx),
        grid=(t // tm,),
        in_specs=[
            pl.BlockSpec((tm, d), lambda i: (i, 0)),
            pl.BlockSpec((1, d), lambda i: (0, 0)),
            pl.BlockSpec((2, d), lambda i: (0, 0)),
            pl.BlockSpec((2, d), lambda i: (0, 0)),
        ],
        out_specs=pl.BlockSpec((tm, d), lambda i: (i, 0)),
        out_shape=jax.ShapeDtypeStruct((t, d), BF16),
        compiler_params=_cparams(("parallel",)),
        name="modulate",
    )(x, g.reshape(1, d), sc, sh)


def _mm_body(x_ref, w_ref, o_ref, wb_ref):
    @pl.when(pl.program_id(1) == 0)
    def _():
        wb_ref[...] = w_ref[...].astype(BF16)

    o_ref[...] = _dot(x_ref[...], wb_ref[...]).astype(o_ref.dtype)


def _matmul(x, w_stack, layer, tn, tm_cap, out_dtype):
    t, k = x.shape
    n = w_stack.shape[-1]
    tm = _row_tile(t, tm_cap)
    return pl.pallas_call(
        _mm_body,
        grid=(n // tn, t // tm),
        in_specs=[
            pl.BlockSpec((tm, k), lambda j, i: (i, 0)),
            _weight_spec((None, k, tn), lambda j, i: (layer, 0, j)),
        ],
        out_specs=pl.BlockSpec((tm, tn), lambda j, i: (i, j)),
        out_shape=jax.ShapeDtypeStruct((t, n), out_dtype),
        scratch_shapes=[pltpu.VMEM((k, tn), BF16)],
        compiler_params=_cparams(("arbitrary", "arbitrary")),
        name="matmul",
    )(x, w_stack)


def _mm_residual_body(a_ref, w_ref, x_ref, g_ref, o_ref, wb_ref, *, tm, n_ctx):
    @pl.when(pl.program_id(1) == 0)
    def _():
        wb_ref[...] = w_ref[...].astype(BF16)

    y = _dot(a_ref[...], wb_ref[...])
    is_ctx = _is_ctx_rows(pl.program_id(1), tm, n_ctx)
    gate = jnp.where(is_ctx, g_ref[1:2, :], g_ref[0:1, :])
    o_ref[...] = x_ref[...] + gate * y


def _matmul_residual(a, w_stack, layer, x, gate, n_ctx, tn, tm_cap):
    t, k = a.shape
    n = w_stack.shape[-1]
    tm = _row_tile(t, tm_cap)
    return pl.pallas_call(
        functools.partial(_mm_residual_body, tm=tm, n_ctx=n_ctx),
        grid=(n // tn, t // tm),
        in_specs=[
            pl.BlockSpec((tm, k), lambda j, i: (i, 0)),
            _weight_spec((None, k, tn), lambda j, i: (layer, 0, j)),
            pl.BlockSpec((tm, tn), lambda j, i: (i, j)),
            pl.BlockSpec((2, tn), lambda j, i: (0, j)),
        ],
        out_specs=pl.BlockSpec((tm, tn), lambda j, i: (i, j)),
        out_shape=jax.ShapeDtypeStruct((t, n), F32),
        scratch_shapes=[pltpu.VMEM((k, tn), BF16)],
        compiler_params=_cparams(("arbitrary", "arbitrary")),
        name="matmul_residual",
    )(a, w_stack, x, gate)


def _mm_residual_modulate_body(a_ref, w_ref, x_ref, gate_ref, g_ref, sc_ref, sh_ref,
                               xo_ref, ho_ref, wb_ref, y_ref, *, tm, n_ctx):
    @pl.when(pl.program_id(0) == 0)
    def _():
        wb_ref[...] = w_ref[...].astype(BF16)

    y_ref[...] = _dot(a_ref[...], wb_ref[...])
    base = pl.program_id(0) * tm
    g = g_ref[...]
    gain_lat = g * (1.0 + sc_ref[0:1, :])
    gain_ctx = g * (1.0 + sc_ref[1:2, :])

    def slab(r, carry):
        start = pl.multiple_of(r * BF16_ROWS, BF16_ROWS)
        rows = pl.ds(start, BF16_ROWS)
        is_ctx = (base + start) < n_ctx
        gate = jnp.where(is_ctx, gate_ref[1:2, :], gate_ref[0:1, :])
        x = x_ref[rows, :] + gate * y_ref[rows, :]
        xo_ref[rows, :] = x
        inv = lax.rsqrt(jnp.mean(x * x, axis=-1, keepdims=True) + EPS)
        gain = jnp.where(is_ctx, gain_ctx, gain_lat)
        shift = jnp.where(is_ctx, sh_ref[1:2, :], sh_ref[0:1, :])
        ho_ref[rows, :] = (x * inv * gain + shift).astype(ho_ref.dtype)
        return carry

    lax.fori_loop(0, tm // BF16_ROWS, slab, 0, unroll=2)


def _matmul_residual_modulate(a, w_stack, layer, x, gate, g, sc, sh, n_ctx, tm_cap):
    t, k = a.shape
    n = w_stack.shape[-1]
    tm = _row_tile(t, tm_cap)
    full = pl.BlockSpec((tm, n), lambda i: (i, 0))
    vec2 = pl.BlockSpec((2, n), lambda i: (0, 0))
    return pl.pallas_call(
        functools.partial(_mm_residual_modulate_body, tm=tm, n_ctx=n_ctx),
        grid=(t // tm,),
        in_specs=[
            pl.BlockSpec((tm, k), lambda i: (i, 0)),
            _weight_spec((None, k, n), lambda i: (layer, 0, 0)),
            full, vec2,
            pl.BlockSpec((1, n), lambda i: (0, 0)), vec2, vec2,
        ],
        out_specs=[full, full],
        out_shape=[jax.ShapeDtypeStruct((t, n), F32), jax.ShapeDtypeStruct((t, n), BF16)],
        scratch_shapes=[pltpu.VMEM((k, n), BF16), pltpu.VMEM((tm, n), F32)],
        compiler_params=_cparams(("arbitrary",)),
        name="matmul_residual_modulate",
    )(a, w_stack, x, gate, g.reshape(1, n), sc, sh)


def _ffn_up_body(h_ref, w1_ref, w3_ref, o_ref, w1b_ref, w3b_ref):
    @pl.when(pl.program_id(1) == 0)
    def _():
        w1b_ref[...] = w1_ref[...].astype(BF16)
        w3b_ref[...] = w3_ref[...].astype(BF16)

    h = h_ref[...]
    a = _dot(h, w1b_ref[...])
    b = _dot(h, w3b_ref[...])
    o_ref[...] = (a * jax.nn.sigmoid(a) * b).astype(o_ref.dtype)


def _ffn_up(h, w1_stack, w3_stack, layer, tn, tm_cap):
    t, k = h.shape
    n = w1_stack.shape[-1]
    tm = _row_tile(t, tm_cap)
    wspec = _weight_spec((None, k, tn), lambda j, i: (layer, 0, j))
    return pl.pallas_call(
        _ffn_up_body,
        grid=(n // tn, t // tm),
        in_specs=[pl.BlockSpec((tm, k), lambda j, i: (i, 0)), wspec, wspec],
        out_specs=pl.BlockSpec((tm, tn), lambda j, i: (i, j)),
        out_shape=jax.ShapeDtypeStruct((t, n), BF16),
        scratch_shapes=[pltpu.VMEM((k, tn), BF16), pltpu.VMEM((k, tn), BF16)],
        compiler_params=_cparams(("arbitrary", "arbitrary")),
        name="ffn_up",
    )(h, w1_stack, w3_stack)


def _merge_body(*refs):
    nb = N_BRANCH
    h_ref = refs[0]
    o_refs = refs[1:1 + nb]
    wm_refs = refs[1 + nb:1 + 2 * nb]
    bm_refs = refs[1 + 2 * nb:1 + 3 * nb]
    wb_refs = refs[1 + 3 * nb:1 + 4 * nb]
    out_ref, wmb_ref, wbb_ref = refs[1 + 4 * nb:]

    @pl.when(pl.program_id(1) == 0)
    def _():
        for k in range(nb):
            wmb_ref[k] = wm_refs[k][...].astype(BF16)
            wbb_ref[k] = wb_refs[k][...].astype(BF16)

    h = h_ref[...]
    acc = None
    for k in range(nb):
        gate = jax.nn.sigmoid(_dot(h, wmb_ref[k]) + bm_refs[k][...])
        term = gate * _dot(o_refs[k][...], wbb_ref[k])
        acc = term if acc is None else acc + term
    out_ref[...] = acc.astype(out_ref.dtype)


def _merge(h, outs, w_merge, b_merge, w_branch, layer, tn):
    t, d = h.shape
    bw = outs[0].shape[1]
    depth = w_merge.shape[0]
    tm = _row_tile(t, cap=528)
    nt = d // tn
    b_merge3 = b_merge.reshape(depth, 1, N_BRANCH * d)
    in_specs = [pl.BlockSpec((tm, d), lambda j, i: (i, 0))]
    in_specs += [pl.BlockSpec((tm, bw), lambda j, i: (i, 0)) for _ in range(N_BRANCH)]
    in_specs += [_weight_spec((None, d, tn), lambda j, i, k=k: (layer, 0, k * nt + j))
                 for k in range(N_BRANCH)]
    in_specs += [pl.BlockSpec((None, 1, tn), lambda j, i, k=k: (layer, 0, k * nt + j))
                 for k in range(N_BRANCH)]
    in_specs += [_weight_spec((None, None, bw, tn), lambda j, i, k=k: (layer, k, 0, j))
                 for k in range(N_BRANCH)]
    return pl.pallas_call(
        _merge_body,
        grid=(nt, t // tm),
        in_specs=in_specs,
        out_specs=pl.BlockSpec((tm, tn), lambda j, i: (i, j)),
        out_shape=jax.ShapeDtypeStruct((t, d), BF16),
        scratch_shapes=[pltpu.VMEM((N_BRANCH, d, tn), BF16),
                        pltpu.VMEM((N_BRANCH, bw, tn), BF16)],
        compiler_params=_cparams(("arbitrary", "arbitrary")),
        name="merge",
    )(h, *outs, *([w_merge] * N_BRANCH), *([b_merge3] * N_BRANCH), *([w_branch] * N_BRANCH))


def _block_order(step, n_blocks, n_ctx_blocks, reverse):
    if not reverse:
        return step
    return jnp.where(step < n_ctx_blocks, n_ctx_blocks - 1 - step,
                     n_blocks + n_ctx_blocks - 1 - step)


def _retention_body(qf_ref, kf_ref, vf_ref, cosf_ref, sinf_ref,
                    qb_ref, kb_ref, vb_ref, cosb_ref, sinb_ref, tab_ref,
                    of_ref, ob_ref, s_ref):
    @pl.when(pl.program_id(0) == 0)
    def _():
        s_ref[...] = jnp.zeros_like(s_ref)

    scale = RET_DK ** -0.5
    chunks = SEQ_BLOCK // CHUNK
    dirs = ((qf_ref, kf_ref, vf_ref, cosf_ref, sinf_ref, of_ref, range(chunks)),
            (qb_ref, kb_ref, vb_ref, cosb_ref, sinb_ref, ob_ref, range(chunks - 1, -1, -1)))
    for d, (q_ref, k_ref, v_ref, cos_ref, sin_ref, o_ref, order) in enumerate(dirs):
        for h in range(RET_HEADS):
            sl = slice(h * RET_DK, (h + 1) * RET_DK)
            dmat = tab_ref[d, 0, h]
            q_decay = tab_ref[d, 1, h]
            k_decay = tab_ref[d, 2, h]
            g_chunk = tab_ref[d, 3, h]
            state = s_ref[d, h]
            for ci in order:
                rows = slice(ci * CHUNK, (ci + 1) * CHUNK)
                cos = cos_ref[rows, :]
                sin = sin_ref[rows, :]
                q = q_ref[rows, sl]
                k = k_ref[rows, sl]
                v = v_ref[rows, sl].astype(BF16)
                qr = (q * cos + pltpu.roll(q, RET_DK // 2, 1) * sin) * scale
                kr = k * cos + pltpu.roll(k, RET_DK // 2, 1) * sin
                scores = lax.dot_general(qr.astype(BF16), kr.astype(BF16),
                                         (((1,), (1,)), ((), ())),
                                         preferred_element_type=F32) * dmat
                intra = _dot(scores.astype(BF16), v)
                cross = _dot((qr * q_decay).astype(BF16), state.astype(BF16))
                kv = _dot((kr * k_decay).T.astype(BF16), v)
                o_ref[rows, sl] = intra + cross
                state = g_chunk * state + kv
            s_ref[d, h] = state


def _retention_tables(log_gamma):
    pos = jnp.arange(CHUNK, dtype=F32)
    i = pos[:, None]
    j = pos[None, :]
    full = log_gamma.shape[:1] + (RET_HEADS, CHUNK, RET_DK)

    def one_dir(lg, diff, mask, q_pow, k_pow):
        dmat = jnp.where(mask, jnp.exp(lg[:, :, None, None] * jnp.maximum(diff, 0.0)), 0.0)
        q_decay = jnp.broadcast_to(jnp.exp(lg[:, :, None] * q_pow)[..., None], full)
        k_decay = jnp.broadcast_to(jnp.exp(lg[:, :, None] * k_pow)[..., None], full)
        g_chunk = jnp.broadcast_to(jnp.exp(lg * CHUNK)[:, :, None, None], full)
        return jnp.stack([dmat, q_decay, k_decay, g_chunk], axis=1)

    fwd = one_dir(log_gamma[:, 0], i - j, (i - j) >= 0, pos + 1.0, CHUNK - 1.0 - pos)
    bwd = one_dir(log_gamma[:, 1], j - i, (j - i) > 0, CHUNK - pos, pos)
    return jnp.stack([fwd, bwd], axis=1)


def _retention(proj, cos2, sin2, tables, layer, n_ctx):
    t = proj.shape[0]
    nb = t // SEQ_BLOCK
    nbc = n_ctx // SEQ_BLOCK

    def specs(reverse):
        def blk(st):
            return _block_order(st, nb, nbc, reverse)
        sec = [pl.BlockSpec((SEQ_BLOCK, BRANCH_W), lambda st, s=s: (blk(st), s))
               for s in (SEC_Q, SEC_K, SEC_V)]
        row = pl.BlockSpec((SEQ_BLOCK, RET_DK), lambda st: (blk(st), 0))
        out = pl.BlockSpec((SEQ_BLOCK, BRANCH_W), lambda st: (blk(st), 0))
        return sec + [row, row], out

    in_f, out_f = specs(False)
    in_b, out_b = specs(True)
    tab_spec = pl.BlockSpec((None, 2, 4, RET_HEADS, CHUNK, RET_DK),
                            lambda st: (layer, 0, 0, 0, 0, 0))
    shape = jax.ShapeDtypeStruct((t, BRANCH_W), F32)
    return pl.pallas_call(
        _retention_body,
        grid=(nb,),
        in_specs=in_f + in_b + [tab_spec],
        out_specs=[out_f, out_b],
        out_shape=[shape, shape],
        scratch_shapes=[pltpu.VMEM((2, RET_HEADS, RET_DK, RET_DK), F32)],
        compiler_params=_cparams(("arbitrary",)),
        name="retention",
    )(proj, proj, proj, cos2, sin2, proj, proj, proj, cos2, sin2, tables)


def _s5_body(uf_ref, ub_ref, wb_ref, a_ref, wc_ref, d_ref, yf_ref, yb_ref,
             x_ref, bu_ref, st_ref):
    @pl.when(pl.program_id(0) == 0)
    def _():
        x_ref[...] = jnp.zeros_like(x_ref)

    for d, u_ref in enumerate((uf_ref, ub_ref)):
        ub = u_ref[...].astype(BF16)
        for r in range(S5_LANE_BLOCKS):
            bu = _dot(ub[:, r * LANES:(r + 1) * LANES], wb_ref[d, r])
            cols = slice(r * S5_BLOCK_STATE, (r + 1) * S5_BLOCK_STATE)
            bu_ref[2 * d, :, cols] = bu[:, :S5_BLOCK_STATE]
            bu_ref[2 * d + 1, :, cols] = bu[:, S5_BLOCK_STATE:]

    af_re, af_im = a_ref[0, 0:1, :], a_ref[0, 1:2, :]
    ab_re, ab_im = a_ref[1, 0:1, :], a_ref[1, 1:2, :]

    def step(i, carry):
        f_re, f_im, b_re, b_im = carry
        tf = pl.ds(i, 1)
        tb = pl.ds(SEQ_BLOCK - 1 - i, 1)
        nf_re = af_re * f_re - af_im * f_im + bu_ref[0, tf, :]
        nf_im = af_re * f_im + af_im * f_re + bu_ref[1, tf, :]
        nb_re = ab_re * b_re - ab_im * b_im + bu_ref[2, tb, :]
        nb_im = ab_re * b_im + ab_im * b_re + bu_ref[3, tb, :]
        st_ref[0, tf, :] = nf_re
        st_ref[1, tf, :] = nf_im
        st_ref[2, tb, :] = nb_re
        st_ref[3, tb, :] = nb_im
        return nf_re, nf_im, nb_re, nb_im

    carry = lax.fori_loop(0, SEQ_BLOCK, step,
                          (x_ref[0:1, :], x_ref[1:2, :], x_ref[2:3, :], x_ref[3:4, :]),
                          unroll=8)
    for n, val in enumerate(carry):
        x_ref[n:n + 1, :] = val

    def readout(d):
        ys = []
        for r in range(S5_LANE_BLOCKS):
            cols = slice(r * S5_BLOCK_STATE, (r + 1) * S5_BLOCK_STATE)
            st = jnp.concatenate([st_ref[2 * d, :, cols], st_ref[2 * d + 1, :, cols]],
                                 axis=1).astype(BF16)
            ys.append(_dot(st, wc_ref[r]))
        return jnp.concatenate(ys, axis=1)

    yf_ref[...] = readout(0) + d_ref[...] * uf_ref[...]
    yb_ref[...] = readout(1)


def _s5_prepare(a_re, a_im, b_re, b_im, log_dt, c_re, c_im):
    depth = a_re.shape[0]
    a_re = a_re[:, None]
    a_im = a_im[:, None]
    dt = jnp.exp(log_dt)[..., None]
    mag = jnp.exp(dt * a_re)
    ang = dt * a_im
    ab_re, ab_im = mag * jnp.cos(ang), mag * jnp.sin(ang)
    nr, ni = ab_re - 1.0, ab_im
    den = a_re * a_re + a_im * a_im
    f_re = (nr * a_re + ni * a_im) / den
    f_im = (ni * a_re - nr * a_im) / den
    bb_re = f_re[..., None] * b_re - f_im[..., None] * b_im
    bb_im = f_re[..., None] * b_im + f_im[..., None] * b_re
    eye = jnp.eye(S5_GROUPS_PER_BLOCK, dtype=F32)

    def in_map(bb):
        bb6 = bb.reshape(depth, 2, S5_LANE_BLOCKS, S5_GROUPS_PER_BLOCK, S5_STATE, S5_GROUP)
        w = jnp.einsum('ldrgpc,gh->ldrgchp', bb6, eye)
        return w.reshape(depth, 2, S5_LANE_BLOCKS, LANES, S5_BLOCK_STATE)

    def out_map(c):
        c5 = c.reshape(depth, S5_LANE_BLOCKS, S5_GROUPS_PER_BLOCK, S5_GROUP, S5_STATE)
        w = jnp.einsum('lrgcp,gh->lrgphc', c5, eye)
        return w.reshape(depth, S5_LANE_BLOCKS, S5_BLOCK_STATE, LANES)

    wb = jnp.concatenate([in_map(bb_re), in_map(bb_im)], axis=-1).astype(BF16)
    a = jnp.stack([ab_re.reshape(depth, 2, S5_STATE_W), ab_im.reshape(depth, 2, S5_STATE_W)],
                  axis=2)
    wc = jnp.concatenate([out_map(c_re), out_map(-c_im)], axis=-2).astype(BF16)
    return wb, a, wc


def _s5(proj, wb, a, wc, s5_d, layer, n_ctx):
    t = proj.shape[0]
    nb = t // SEQ_BLOCK
    nbc = n_ctx // SEQ_BLOCK
    depth = s5_d.shape[0]

    def io(reverse, sec):
        return pl.BlockSpec((SEQ_BLOCK, BRANCH_W),
                            lambda st: (_block_order(st, nb, nbc, reverse), sec))

    shape = jax.ShapeDtypeStruct((t, BRANCH_W), F32)
    planes = pltpu.VMEM((4, SEQ_BLOCK, S5_STATE_W), F32)
    return pl.pallas_call(
        _s5_body,
        grid=(nb,),
        in_specs=[io(False, SEC_U), io(True, SEC_U),
                  pl.BlockSpec((None,) + wb.shape[1:], lambda st: (layer, 0, 0, 0, 0)),
                  pl.BlockSpec((None,) + a.shape[1:], lambda st: (layer, 0, 0, 0)),
                  pl.BlockSpec((None,) + wc.shape[1:], lambda st: (layer, 0, 0, 0)),
                  pl.BlockSpec((None, 1, BRANCH_W), lambda st: (layer, 0, 0))],
        out_specs=[io(False, 0), io(True, 0)],
        out_shape=[shape, shape],
        scratch_shapes=[pltpu.VMEM((4, S5_STATE_W), F32), planes, planes],
        compiler_params=_cparams(("arbitrary",)),
        name="s5",
    )(proj, proj, wb, a, wc, s5_d.reshape(depth, 1, BRANCH_W))


def _local_body(of_ref, ob_ref, g_ref, yf_ref, yb_ref, wglu_ref,
                cx_ref, cb_ref, cc_ref, cxp_ref, ccp_ref, cxn_ref, ccn_ref, cw_ref,
                gu_ref, gv_ref, lng_ref, lnb_ref, ws_ref, bs_ref,
                oret_ref, os5_ref, oc_ref, og_ref, *, n_blocks, n_ctx_blocks):
    for h in range(RET_HEADS):
        sl = slice(h * RET_DK, (h + 1) * RET_DK)
        o = of_ref[:, sl] + ob_ref[:, sl]
        mu = jnp.mean(o, axis=-1, keepdims=True)
        var = jnp.mean(jnp.square(o - mu), axis=-1, keepdims=True)
        o = (o - mu) * lax.rsqrt(var + EPS)
        g = g_ref[:, sl]
        oret_ref[:, sl] = (o * (g * jax.nn.sigmoid(g))).astype(oret_ref.dtype)

    y = jax.nn.gelu(yf_ref[...] + yb_ref[...])
    os5_ref[...] = (y * jax.nn.sigmoid(_dot(y.astype(BF16), wglu_ref[...]))).astype(os5_ref.dtype)

    c = pl.program_id(0)
    seq_start = jnp.logical_or(c == 0, c == n_ctx_blocks)
    seq_end = jnp.logical_or(c == n_ctx_blocks - 1, c == n_blocks - 1)
    z = cc_ref[...] * cx_ref[...]
    z_prev = ccp_ref[SUBLANES - 1:SUBLANES, :] * cxp_ref[SUBLANES - 1:SUBLANES, :]
    z_next = ccn_ref[0:1, :] * cxn_ref[0:1, :]
    z_prev = jnp.where(seq_start, 0.0, z_prev)
    z_next = jnp.where(seq_end, 0.0, z_next)
    rows = lax.broadcasted_iota(jnp.int32, (SEQ_BLOCK, 1), 0)
    z_up = jnp.where(rows == 0, z_prev, pltpu.roll(z, 1, 0))
    z_dn = jnp.where(rows == SEQ_BLOCK - 1, z_next, pltpu.roll(z, SEQ_BLOCK - 1, 0))
    yc = cw_ref[0:1, :] * z_up + cw_ref[1:2, :] * z + cw_ref[2:3, :] * z_dn
    oc_ref[...] = (cb_ref[...] * yc).astype(oc_ref.dtype)

    u = jax.nn.gelu(gu_ref[...])
    v = jax.nn.gelu(gv_ref[...])
    mu = jnp.mean(v, axis=-1, keepdims=True)
    var = jnp.mean(jnp.square(v - mu), axis=-1, keepdims=True)
    v = ((v - mu) * lax.rsqrt(var + EPS) * lng_ref[...] + lnb_ref[...]).astype(BF16)
    for ci in range(SEQ_BLOCK // CHUNK):
        rws = slice(ci * CHUNK, (ci + 1) * CHUNK)
        for g in range(CMLP_GROUPS):
            sl = slice(g * CMLP_GW, (g + 1) * CMLP_GW)
            mixed = _dot(ws_ref[g], v[rws, sl]) + bs_ref[g]
            og_ref[rws, sl] = (u[rws, sl] * mixed).astype(og_ref.dtype)


def _local_mixers(proj, o_f, o_b, y_f, y_b, wglu_b, conv_w, ln_g, ln_b, ws_b, bs_full,
                  layer, n_ctx):
    t = proj.shape[0]
    nb = t // SEQ_BLOCK
    nbc = n_ctx // SEQ_BLOCK
    per = SEQ_BLOCK // SUBLANES
    last = t // SUBLANES - 1
    depth = conv_w.shape[0]

    def sec(s):
        return pl.BlockSpec((SEQ_BLOCK, BRANCH_W), lambda c: (c, s))

    def prev(s):
        return pl.BlockSpec((SUBLANES, BRANCH_W), lambda c: (jnp.maximum(c * per - 1, 0), s))

    def nxt(s):
        return pl.BlockSpec((SUBLANES, BRANCH_W), lambda c: (jnp.minimum((c + 1) * per, last), s))

    def layer_spec(x):
        nd = x.ndim - 1
        return pl.BlockSpec((None,) + x.shape[1:], lambda c: (layer,) + (0,) * nd)

    ln_g3 = ln_g.reshape(depth, 1, BRANCH_W)
    ln_b3 = ln_b.reshape(depth, 1, BRANCH_W)
    blk = sec(0)
    out_shape = jax.ShapeDtypeStruct((t, BRANCH_W), BF16)
    return pl.pallas_call(
        functools.partial(_local_body, n_blocks=nb, n_ctx_blocks=nbc),
        grid=(nb,),
        in_specs=[blk, blk, sec(SEC_G), blk, blk, layer_spec(wglu_b),
                  sec(SEC_CX), sec(SEC_CB), sec(SEC_CC),
                  prev(SEC_CX), prev(SEC_CC), nxt(SEC_CX), nxt(SEC_CC), layer_spec(conv_w),
                  sec(SEC_GU), sec(SEC_GV), layer_spec(ln_g3), layer_spec(ln_b3),
                  layer_spec(ws_b), layer_spec(bs_full)],
        out_specs=[blk, blk, blk, blk],
        out_shape=[out_shape] * 4,
        compiler_params=_cparams(("parallel",)),
        name="local_mixers",
    )(o_f, o_b, proj, y_f, y_b, wglu_b,
      proj, proj, proj, proj, proj, proj, proj, conv_w,
      proj, proj, ln_g3, ln_b3, ws_b, bs_full)


def _final_norm_body(x_ref, g_ref, o_ref, *, tm):
    g = g_ref[...]

    def slab(r, carry):
        start = pl.multiple_of(r * BF16_ROWS, BF16_ROWS)
        x = x_ref[pl.ds(start, BF16_ROWS), :]
        inv = lax.rsqrt(jnp.mean(x * x, axis=-1, keepdims=True) + EPS)
        o_ref[pl.ds(start, BF16_ROWS), :] = x * inv * g
        return carry

    lax.fori_loop(0, tm // BF16_ROWS, slab, 0, unroll=2)


def _final_norm(x, g, n_ctx):
    t, d = x.shape
    tm = math.gcd(n_ctx, 512)
    skip = n_ctx // tm
    n_lat = t - n_ctx
    return pl.pallas_call(
        functools.partial(_final_norm_body, tm=tm),
        grid=(n_lat // tm,),
        in_specs=[pl.BlockSpec((tm, d), lambda i: (i + skip, 0)),
                  pl.BlockSpec((1, d), lambda i: (0, 0))],
        out_specs=pl.BlockSpec((tm, d), lambda i: (i, 0)),
        out_shape=jax.ShapeDtypeStruct((n_lat, d), F32),
        compiler_params=_cparams(("parallel",)),
        name="final_norm",
    )(x, g.reshape(1, d))


def _rope_tables(n_lat, n_ctx):
    rows = n_lat // GRID_W
    row = jnp.broadcast_to(jnp.arange(rows)[:, None], (rows, GRID_W)).reshape(-1).astype(F32)
    col = jnp.broadcast_to(jnp.arange(GRID_W)[None, :], (rows, GRID_W)).reshape(-1).astype(F32)
    nf = RET_DK // 4
    freqs = ROPE_BASE ** (-jnp.arange(nf, dtype=F32) / nf)
    ang = jnp.concatenate([row[:, None] * freqs, col[:, None] * freqs], axis=-1)
    cos, sin = jnp.cos(ang), jnp.sin(ang)
    cos2 = jnp.concatenate([cos, cos], axis=-1)
    sin2 = jnp.concatenate([-sin, sin], axis=-1)
    cos2 = jnp.concatenate([jnp.ones((n_ctx, RET_DK), F32), cos2], axis=0)
    sin2 = jnp.concatenate([jnp.zeros((n_ctx, RET_DK), F32), sin2], axis=0)
    return cos2, sin2


def kernel(x, c, ctx, c_ctx, ada_w, ada_b, norm1_g, norm2_g, w_in, ret_decay_logit, s5_a_re, s5_a_im, s5_b_re, s5_b_im, s5_c_re, s5_c_im, s5_d, s5_log_dt, s5_w_glu, conv_w, cmlp_ln_g, cmlp_ln_b, cmlp_ws, cmlp_bs, w_branch, w_merge, b_merge, w_out, ffn_w1, ffn_w3, ffn_w2, final_norm_g):
    batch, n_lat, d = x.shape
    assert batch == 1 and c.shape[0] == 1 and ctx.shape[0] == 1
    n_ctx = ctx.shape[1]
    assert n_ctx % SEQ_BLOCK == 0 and n_lat % SEQ_BLOCK == 0 and n_lat % GRID_W == 0
    depth = ada_w.shape[0]

    cos2, sin2 = _rope_tables(n_lat, n_ctx)
    ret_tables = _retention_tables(jax.nn.log_sigmoid(ret_decay_logit.astype(F32)))
    s5_wb, s5_a, s5_wc = _s5_prepare(s5_a_re, s5_a_im, s5_b_re, s5_b_im, s5_log_dt,
                                     s5_c_re, s5_c_im)
    wglu_b = s5_w_glu.astype(BF16)
    ws_b = cmlp_ws.astype(BF16)
    bs_full = jnp.broadcast_to(cmlp_bs[..., None], cmlp_bs.shape + (CMLP_GW,))

    cvec = jnp.zeros((SUBLANES, d), F32).at[0].set(c[0]).at[1].set(c_ctx)
    mod_all = _adaln(cvec, ada_w, ada_b)
    mod_all = mod_all[:, 0:2].reshape(depth, 2, 6, d)
    xs = jnp.concatenate([ctx[0], x[0]], axis=0)

    for l in range(depth):
        sh1, sc1, g1, sh2, sc2, g2 = [mod_all[l, :, i] for i in range(6)]
        h = _modulate(xs, norm1_g[l], sc1, sh1, n_ctx)
        proj = _matmul(h, w_in, l, 2560, 352, F32)

        o_f, o_b = _retention(proj, cos2, sin2, ret_tables, l, n_ctx)
        y_f, y_b = _s5(proj, s5_wb, s5_a, s5_wc, s5_d, l, n_ctx)
        outs = _local_mixers(proj, o_f, o_b, y_f, y_b, wglu_b, conv_w, cmlp_ln_g, cmlp_ln_b,
                             ws_b, bs_full, l, n_ctx)

        merged = _merge(h, outs, w_merge, b_merge, w_branch, l, 512)
        xs, h2 = _matmul_residual_modulate(merged, w_out, l, xs, g1, norm2_g[l], sc2, sh2,
                                           n_ctx, 352)
        act = _ffn_up(h2, ffn_w1, ffn_w3, l, 1408, 352)
        xs = _matmul_residual(act, ffn_w2, l, xs, g2, n_ctx, 1024, 352)

    return _final_norm(xs, final_norm_g, n_ctx)[None]
```

```python
import functools
import math

import jax
import jax.numpy as jnp
from jax import lax
from jax.experimental import pallas as pl
from jax.experimental.pallas import tpu as pltpu

F32 = jnp.float32
BF16 = jnp.bfloat16

D_MODEL = 2048
GRID_W = 64
BRANCH_W = 512
N_BRANCH = 4
N_SECTIONS = 10
RET_HEADS = 4
RET_DK = BRANCH_W // RET_HEADS
ROPE_BASE = 10000.0
S5_GROUP = 16
S5_GROUPS = BRANCH_W // S5_GROUP
S5_STATE = 64
CMLP_GROUPS = 4
CMLP_GW = BRANCH_W // CMLP_GROUPS
EPS = 1e-6

CHUNK = 128
SEQ_BLOCK = 256
LANES = 128
SUBLANES = 8
BF16_ROWS = 16
S5_LANE_BLOCKS = BRANCH_W // LANES
S5_GROUPS_PER_BLOCK = LANES // S5_GROUP
S5_BLOCK_STATE = S5_GROUPS_PER_BLOCK * S5_STATE
S5_STATE_W = S5_GROUPS * S5_STATE
MIB = 1024 * 1024
VMEM_BUDGET_MIB = 56

SEC_Q, SEC_K, SEC_V, SEC_G, SEC_U, SEC_CX, SEC_CB, SEC_CC, SEC_GU, SEC_GV = range(N_SECTIONS)


def _cparams(semantics, vmem_mib=VMEM_BUDGET_MIB):
    return pltpu.CompilerParams(dimension_semantics=semantics,
                                vmem_limit_bytes=vmem_mib * MIB)


def _dot(a, b):
    return jnp.dot(a, b, preferred_element_type=F32)


def _row_tile(t, cap=1056):
    for tm in (1056, 1024, 768, 704, 528, 512, 384, 352, 256, 128):
        if tm <= cap and t % tm == 0:
            return tm
    raise ValueError(f"unsupported token count {t}")


def _is_ctx_rows(tile_idx, tm, n_ctx):
    rows = tile_idx * tm + lax.broadcasted_iota(jnp.int32, (tm, 1), 0)
    return rows < n_ctx


def _weight_spec(block, index_map, buffers=1):
    return pl.BlockSpec(block, index_map, pipeline_mode=pl.Buffered(buffers))


def _whole(x):
    nd = x.ndim
    return pl.BlockSpec(x.shape, lambda *_: (0,) * nd)


def _adaln_body(c_ref, w_ref, b_ref, o_ref):
    c = c_ref[...]
    s = c * jax.nn.sigmoid(c)
    w = w_ref[...]
    s_hi = s.astype(BF16)
    s_lo = (s - s_hi.astype(F32)).astype(BF16)
    w_hi = w.astype(BF16)
    w_lo = (w - w_hi.astype(F32)).astype(BF16)
    acc = _dot(s_hi, w_hi) + _dot(s_lo, w_hi) + _dot(s_hi, w_lo)
    o_ref[...] = acc + b_ref[...]


def _adaln(cvec, ada_w, ada_b):
    depth, d, n = ada_w.shape
    tn = 1024
    return pl.pallas_call(
        _adaln_body,
        grid=(depth, n // tn),
        in_specs=[
            pl.BlockSpec((SUBLANES, d), lambda l, j: (0, 0)),
            pl.BlockSpec((None, d, tn), lambda l, j: (l, 0, j)),
            pl.BlockSpec((None, 1, tn), lambda l, j: (l, 0, j)),
        ],
        out_specs=pl.BlockSpec((None, SUBLANES, tn), lambda l, j: (l, 0, j)),
        out_shape=jax.ShapeDtypeStruct((depth, SUBLANES, n), F32),
        compiler_params=_cparams(("parallel", "parallel")),
        name="adaln",
    )(cvec, ada_w, ada_b.reshape(depth, 1, n))


def _modulate_body(x_ref, g_ref, sc_ref, sh_ref, o_ref, *, tm, n_ctx):
    base = pl.program_id(0) * tm
    g = g_ref[...]
    gain_lat = g * (1.0 + sc_ref[0:1, :])
    gain_ctx = g * (1.0 + sc_ref[1:2, :])
    shift_lat = sh_ref[0:1, :]
    shift_ctx = sh_ref[1:2, :]

    def slab(r, carry):
        start = pl.multiple_of(r * BF16_ROWS, BF16_ROWS)
        x = x_ref[pl.ds(start, BF16_ROWS), :]
        inv = lax.rsqrt(jnp.mean(x * x, axis=-1, keepdims=True) + EPS)
        is_ctx = (base + start) < n_ctx
        gain = jnp.where(is_ctx, gain_ctx, gain_lat)
        shift = jnp.where(is_ctx, shift_ctx, shift_lat)
        o_ref[pl.ds(start, BF16_ROWS), :] = (x * inv * gain + shift).astype(o_ref.dtype)
        return carry

    lax.fori_loop(0, tm // BF16_ROWS, slab, 0, unroll=3)


def _modulate(x, g, sc, sh, n_ctx):
    t, d = x.shape
    tm = _row_tile(t, cap=528)
    return pl.pallas_call(
        functools.partial(_modulate_body, tm=tm, n_ctx=n_ctx),
        grid=(t // tm,),
        in_specs=[
            pl.BlockSpec((tm, d), lambda i: (i, 0)),
            pl.BlockSpec((1, d), lambda i: (0, 0)),
            pl.BlockSpec((2, d), lambda i: (0, 0)),
            pl.BlockSpec((2, d), lambda i: (0, 0)),
        ],
        out_specs=pl.BlockSpec((tm, d), lambda i: (i, 0)),
        out_shape=jax.ShapeDtypeStruct((t, d), BF16),
        compiler_params=_cparams(("parallel",)),
        name="modulate",
    )(x, g.reshape(1, d), sc, sh)


def _mm_body(x_ref, w_ref, o_ref, wb_ref):
    @pl.when(pl.program_id(1) == 0)
    def _():
        wb_ref[...] = w_ref[...].astype(BF16)

    o_ref[...] = _dot(x_ref[...], wb_ref[...]).astype(o_ref.dtype)


def _matmul(x, w_stack, layer, tn, tm_cap, w_buffers, out_dtype):
    t, k = x.shape
    n = w_stack.shape[-1]
    tm = _row_tile(t, tm_cap)
    return pl.pallas_call(
        _mm_body,
        grid=(n // tn, t // tm),
        in_specs=[
            pl.BlockSpec((tm, k), lambda j, i: (i, 0)),
            _weight_spec((None, k, tn), lambda j, i: (layer, 0, j), w_buffers),
        ],
        out_specs=pl.BlockSpec((tm, tn), lambda j, i: (i, j)),
        out_shape=jax.ShapeDtypeStruct((t, n), out_dtype),
        scratch_shapes=[pltpu.VMEM((k, tn), BF16)],
        compiler_params=_cparams(("arbitrary", "arbitrary")),
        name="matmul",
    )(x, w_stack)


def _mm_residual_body(a_ref, w_ref, x_ref, g_ref, o_ref, wb_ref, *, tm, n_ctx):
    @pl.when(pl.program_id(1) == 0)
    def _():
        wb_ref[...] = w_ref[...].astype(BF16)

    y = _dot(a_ref[...], wb_ref[...])
    is_ctx = _is_ctx_rows(pl.program_id(1), tm, n_ctx)
    gate = jnp.where(is_ctx, g_ref[1:2, :], g_ref[0:1, :])
    o_ref[...] = x_ref[...] + gate * y


def _matmul_residual(a, w_stack, layer, x, gate, n_ctx, tn, tm_cap):
    t, k = a.shape
    n = w_stack.shape[-1]
    tm = _row_tile(t, tm_cap)
    return pl.pallas_call(
        functools.partial(_mm_residual_body, tm=tm, n_ctx=n_ctx),
        grid=(n // tn, t // tm),
        in_specs=[
            pl.BlockSpec((tm, k), lambda j, i: (i, 0)),
            _weight_spec((None, k, tn), lambda j, i: (layer, 0, j)),
            pl.BlockSpec((tm, tn), lambda j, i: (i, j)),
            pl.BlockSpec((2, tn), lambda j, i: (0, j)),
        ],
        out_specs=pl.BlockSpec((tm, tn), lambda j, i: (i, j)),
        out_shape=jax.ShapeDtypeStruct((t, n), F32),
        scratch_shapes=[pltpu.VMEM((k, tn), BF16)],
        compiler_params=_cparams(("arbitrary", "arbitrary")),
        name="matmul_residual",
    )(a, w_stack, x, gate)


def _mm_residual_modulate_body(a_ref, w_ref, x_ref, gate_ref, g_ref, sc_ref, sh_ref,
                               xo_ref, ho_ref, wb_ref, y_ref, *, tm, n_ctx):
    @pl.when(pl.program_id(0) == 0)
    def _():
        wb_ref[...] = w_ref[...].astype(BF16)

    y_ref[...] = _dot(a_ref[...], wb_ref[...])
    base = pl.program_id(0) * tm
    g = g_ref[...]
    gain_lat = g * (1.0 + sc_ref[0:1, :])
    gain_ctx = g * (1.0 + sc_ref[1:2, :])

    def slab(r, carry):
        start = pl.multiple_of(r * BF16_ROWS, BF16_ROWS)
        rows = pl.ds(start, BF16_ROWS)
        is_ctx = (base + start) < n_ctx
        gate = jnp.where(is_ctx, gate_ref[1:2, :], gate_ref[0:1, :])
        x = x_ref[rows, :] + gate * y_ref[rows, :]
        xo_ref[rows, :] = x
        inv = lax.rsqrt(jnp.mean(x * x, axis=-1, keepdims=True) + EPS)
        gain = jnp.where(is_ctx, gain_ctx, gain_lat)
        shift = jnp.where(is_ctx, sh_ref[1:2, :], sh_ref[0:1, :])
        ho_ref[rows, :] = (x * inv * gain + shift).astype(ho_ref.dtype)
        return carry

    lax.fori_loop(0, tm // BF16_ROWS, slab, 0, unroll=2)


def _matmul_residual_modulate(a, w_stack, layer, x, gate, g, sc, sh, n_ctx, tm_cap):
    t, k = a.shape
    n = w_stack.shape[-1]
    tm = _row_tile(t, tm_cap)
    full = pl.BlockSpec((tm, n), lambda i: (i, 0))
    vec2 = pl.BlockSpec((2, n), lambda i: (0, 0))
    return pl.pallas_call(
        functools.partial(_mm_residual_modulate_body, tm=tm, n_ctx=n_ctx),
        grid=(t // tm,),
        in_specs=[
            pl.BlockSpec((tm, k), lambda i: (i, 0)),
            _weight_spec((None, k, n), lambda i: (layer, 0, 0)),
            full, vec2,
            pl.BlockSpec((1, n), lambda i: (0, 0)), vec2, vec2,
        ],
        out_specs=[full, full],
        out_shape=[jax.ShapeDtypeStruct((t, n), F32), jax.ShapeDtypeStruct((t, n), BF16)],
        scratch_shapes=[pltpu.VMEM((k, n), BF16), pltpu.VMEM((tm, n), F32)],
        compiler_params=_cparams(("arbitrary",)),
        name="matmul_residual_modulate",
    )(a, w_stack, x, gate, g.reshape(1, n), sc, sh)


def _ffn_up_body(h_ref, w1_ref, w3_ref, o_ref, w1b_ref, w3b_ref):
    @pl.when(pl.program_id(1) == 0)
    def _():
        w1b_ref[...] = w1_ref[...].astype(BF16)
        w3b_ref[...] = w3_ref[...].astype(BF16)

    h = h_ref[...]
    a = _dot(h, w1b_ref[...])
    b = _dot(h, w3b_ref[...])
    o_ref[...] = (a * jax.nn.sigmoid(a) * b).astype(o_ref.dtype)


def _ffn_up(h, w1_stack, w3_stack, layer, tn, tm_cap, w_buffers):
    t, k = h.shape
    n = w1_stack.shape[-1]
    tm = _row_tile(t, tm_cap)
    wspec = _weight_spec((None, k, tn), lambda j, i: (layer, 0, j), w_buffers)
    return pl.pallas_call(
        _ffn_up_body,
        grid=(n // tn, t // tm),
        in_specs=[pl.BlockSpec((tm, k), lambda j, i: (i, 0)), wspec, wspec],
        out_specs=pl.BlockSpec((tm, tn), lambda j, i: (i, j)),
        out_shape=jax.ShapeDtypeStruct((t, n), BF16),
        scratch_shapes=[pltpu.VMEM((k, tn), BF16), pltpu.VMEM((k, tn), BF16)],
        compiler_params=_cparams(("arbitrary", "arbitrary")),
        name="ffn_up",
    )(h, w1_stack, w3_stack)


def _merge_body(*refs):
    nb = N_BRANCH
    h_ref = refs[0]
    o_refs = refs[1:1 + nb]
    wm_refs = refs[1 + nb:1 + 2 * nb]
    bm_refs = refs[1 + 2 * nb:1 + 3 * nb]
    wb_refs = refs[1 + 3 * nb:1 + 4 * nb]
    out_ref, wmb_ref, wbb_ref = refs[1 + 4 * nb:]

    @pl.when(pl.program_id(1) == 0)
    def _():
        for k in range(nb):
            wmb_ref[k] = wm_refs[k][...].astype(BF16)
            wbb_ref[k] = wb_refs[k][...].astype(BF16)

    h = h_ref[...]
    acc = None
    for k in range(nb):
        gate = jax.nn.sigmoid(_dot(h, wmb_ref[k]) + bm_refs[k][...])
        term = gate * _dot(o_refs[k][...], wbb_ref[k])
        acc = term if acc is None else acc + term
    out_ref[...] = acc.astype(out_ref.dtype)


def _merge(h, outs, w_merge, b_merge, w_branch, layer, tn):
    t, d = h.shape
    bw = outs[0].shape[1]
    depth = w_merge.shape[0]
    tm = _row_tile(t, cap=528)
    nt = d // tn
    b_merge3 = b_merge.reshape(depth, 1, N_BRANCH * d)
    in_specs = [pl.BlockSpec((tm, d), lambda j, i: (i, 0))]
    in_specs += [pl.BlockSpec((tm, bw), lambda j, i: (i, 0)) for _ in range(N_BRANCH)]
    in_specs += [_weight_spec((None, d, tn), lambda j, i, k=k: (layer, 0, k * nt + j))
                 for k in range(N_BRANCH)]
    in_specs += [pl.BlockSpec((None, 1, tn), lambda j, i, k=k: (layer, 0, k * nt + j))
                 for k in range(N_BRANCH)]
    in_specs += [_weight_spec((None, None, bw, tn), lambda j, i, k=k: (layer, k, 0, j))
                 for k in range(N_BRANCH)]
    return pl.pallas_call(
        _merge_body,
        grid=(nt, t // tm),
        in_specs=in_specs,
        out_specs=pl.BlockSpec((tm, tn), lambda j, i: (i, j)),
        out_shape=jax.ShapeDtypeStruct((t, d), BF16),
        scratch_shapes=[pltpu.VMEM((N_BRANCH, d, tn), BF16),
                        pltpu.VMEM((N_BRANCH, bw, tn), BF16)],
        compiler_params=_cparams(("arbitrary", "arbitrary")),
        name="merge",
    )(h, *outs, *([w_merge] * N_BRANCH), *([b_merge3] * N_BRANCH), *([w_branch] * N_BRANCH))


def _block_order(step, n_blocks, n_ctx_blocks, reverse):
    if not reverse:
        return step
    return jnp.where(step < n_ctx_blocks, n_ctx_blocks - 1 - step,
                     n_blocks + n_ctx_blocks - 1 - step)


def _retention_body(qf_ref, kf_ref, vf_ref, cosf_ref, sinf_ref,
                    qb_ref, kb_ref, vb_ref, cosb_ref, sinb_ref, tab_ref,
                    of_ref, ob_ref, s_ref):
    @pl.when(pl.program_id(0) == 0)
    def _():
        s_ref[...] = jnp.zeros_like(s_ref)

    scale = RET_DK ** -0.5
    chunks = SEQ_BLOCK // CHUNK
    dirs = ((qf_ref, kf_ref, vf_ref, cosf_ref, sinf_ref, of_ref, range(chunks)),
            (qb_ref, kb_ref, vb_ref, cosb_ref, sinb_ref, ob_ref, range(chunks - 1, -1, -1)))
    for d, (q_ref, k_ref, v_ref, cos_ref, sin_ref, o_ref, order) in enumerate(dirs):
        for h in range(RET_HEADS):
            sl = slice(h * RET_DK, (h + 1) * RET_DK)
            dmat = tab_ref[d, 0, h]
            q_decay = tab_ref[d, 1, h]
            k_decay = tab_ref[d, 2, h]
            g_chunk = tab_ref[d, 3, h]
            state = s_ref[d, h]
            for ci in order:
                rows = slice(ci * CHUNK, (ci + 1) * CHUNK)
                cos = cos_ref[rows, :]
                sin = sin_ref[rows, :]
                q = q_ref[rows, sl]
                k = k_ref[rows, sl]
                v = v_ref[rows, sl].astype(BF16)
                qr = (q * cos + pltpu.roll(q, RET_DK // 2, 1) * sin) * scale
                kr = k * cos + pltpu.roll(k, RET_DK // 2, 1) * sin
                scores = lax.dot_general(qr.astype(BF16), kr.astype(BF16),
                                         (((1,), (1,)), ((), ())),
                                         preferred_element_type=F32) * dmat
                intra = _dot(scores.astype(BF16), v)
                cross = _dot((qr * q_decay).astype(BF16), state.astype(BF16))
                kv = _dot((kr * k_decay).T.astype(BF16), v)
                o_ref[rows, sl] = intra + cross
                state = g_chunk * state + kv
            s_ref[d, h] = state


def _retention_tables(log_gamma):
    pos = jnp.arange(CHUNK, dtype=F32)
    i = pos[:, None]
    j = pos[None, :]
    full = log_gamma.shape[:1] + (RET_HEADS, CHUNK, RET_DK)

    def one_dir(lg, diff, mask, q_pow, k_pow):
        dmat = jnp.where(mask, jnp.exp(lg[:, :, None, None] * jnp.maximum(diff, 0.0)), 0.0)
        q_decay = jnp.broadcast_to(jnp.exp(lg[:, :, None] * q_pow)[..., None], full)
        k_decay = jnp.broadcast_to(jnp.exp(lg[:, :, None] * k_pow)[..., None], full)
        g_chunk = jnp.broadcast_to(jnp.exp(lg * CHUNK)[:, :, None, None], full)
        return jnp.stack([dmat, q_decay, k_decay, g_chunk], axis=1)

    fwd = one_dir(log_gamma[:, 0], i - j, (i - j) >= 0, pos + 1.0, CHUNK - 1.0 - pos)
    bwd = one_dir(log_gamma[:, 1], j - i, (j - i) > 0, CHUNK - pos, pos)
    return jnp.stack([fwd, bwd], axis=1)


def _retention(proj, cos2, sin2, tables, layer, n_ctx):
    t = proj.shape[0]
    nb = t // SEQ_BLOCK
    nbc = n_ctx // SEQ_BLOCK

    def specs(reverse):
        def blk(st):
            return _block_order(st, nb, nbc, reverse)
        sec = [pl.BlockSpec((SEQ_BLOCK, BRANCH_W), lambda st, s=s: (blk(st), s))
               for s in (SEC_Q, SEC_K, SEC_V)]
        row = pl.BlockSpec((SEQ_BLOCK, RET_DK), lambda st: (blk(st), 0))
        out = pl.BlockSpec((SEQ_BLOCK, BRANCH_W), lambda st: (blk(st), 0))
        return sec + [row, row], out

    in_f, out_f = specs(False)
    in_b, out_b = specs(True)
    tab_spec = pl.BlockSpec((None, 2, 4, RET_HEADS, CHUNK, RET_DK),
                            lambda st: (layer, 0, 0, 0, 0, 0))
    shape = jax.ShapeDtypeStruct((t, BRANCH_W), F32)
    return pl.pallas_call(
        _retention_body,
        grid=(nb,),
        in_specs=in_f + in_b + [tab_spec],
        out_specs=[out_f, out_b],
        out_shape=[shape, shape],
        scratch_shapes=[pltpu.VMEM((2, RET_HEADS, RET_DK, RET_DK), F32)],
        compiler_params=_cparams(("arbitrary",)),
        name="retention",
    )(proj, proj, proj, cos2, sin2, proj, proj, proj, cos2, sin2, tables)


def _s5_body(uf_ref, ub_ref, wb_ref, a_ref, wc_ref, d_ref, yf_ref, yb_ref,
             x_ref, bu_ref, st_ref):
    @pl.when(pl.program_id(0) == 0)
    def _():
        x_ref[...] = jnp.zeros_like(x_ref)

    for d, u_ref in enumerate((uf_ref, ub_ref)):
        ub = u_ref[...].astype(BF16)
        for r in range(S5_LANE_BLOCKS):
            bu = _dot(ub[:, r * LANES:(r + 1) * LANES], wb_ref[d, r])
            cols = slice(r * S5_BLOCK_STATE, (r + 1) * S5_BLOCK_STATE)
            bu_ref[2 * d, :, cols] = bu[:, :S5_BLOCK_STATE]
            bu_ref[2 * d + 1, :, cols] = bu[:, S5_BLOCK_STATE:]

    af_re, af_im = a_ref[0, 0:1, :], a_ref[0, 1:2, :]
    ab_re, ab_im = a_ref[1, 0:1, :], a_ref[1, 1:2, :]

    def step(i, carry):
        f_re, f_im, b_re, b_im = carry
        tf = pl.ds(i, 1)
        tb = pl.ds(SEQ_BLOCK - 1 - i, 1)
        nf_re = af_re * f_re - af_im * f_im + bu_ref[0, tf, :]
        nf_im = af_re * f_im + af_im * f_re + bu_ref[1, tf, :]
        nb_re = ab_re * b_re - ab_im * b_im + bu_ref[2, tb, :]
        nb_im = ab_re * b_im + ab_im * b_re + bu_ref[3, tb, :]
        st_ref[0, tf, :] = nf_re
        st_ref[1, tf, :] = nf_im
        st_ref[2, tb, :] = nb_re
        st_ref[3, tb, :] = nb_im
        return nf_re, nf_im, nb_re, nb_im

    carry = lax.fori_loop(0, SEQ_BLOCK, step,
                          (x_ref[0:1, :], x_ref[1:2, :], x_ref[2:3, :], x_ref[3:4, :]),
                          unroll=8)
    for n, val in enumerate(carry):
        x_ref[n:n + 1, :] = val

    def readout(d):
        ys = []
        for r in range(S5_LANE_BLOCKS):
            cols = slice(r * S5_BLOCK_STATE, (r + 1) * S5_BLOCK_STATE)
            st = jnp.concatenate([st_ref[2 * d, :, cols], st_ref[2 * d + 1, :, cols]],
                                 axis=1).astype(BF16)
            ys.append(_dot(st, wc_ref[r]))
        return jnp.concatenate(ys, axis=1)

    yf_ref[...] = readout(0) + d_ref[...] * uf_ref[...]
    yb_ref[...] = readout(1)


def _s5_prepare(a_re, a_im, b_re, b_im, log_dt, c_re, c_im):
    depth = a_re.shape[0]
    a_re = a_re[:, None]
    a_im = a_im[:, None]
    dt = jnp.exp(log_dt)[..., None]
    mag = jnp.exp(dt * a_re)
    ang = dt * a_im
    ab_re, ab_im = mag * jnp.cos(ang), mag * jnp.sin(ang)
    nr, ni = ab_re - 1.0, ab_im
    den = a_re * a_re + a_im * a_im
    f_re = (nr * a_re + ni * a_im) / den
    f_im = (ni * a_re - nr * a_im) / den
    bb_re = f_re[..., None] * b_re - f_im[..., None] * b_im
    bb_im = f_re[..., None] * b_im + f_im[..., None] * b_re
    eye = jnp.eye(S5_GROUPS_PER_BLOCK, dtype=F32)

    def in_map(bb):
        bb6 = bb.reshape(depth, 2, S5_LANE_BLOCKS, S5_GROUPS_PER_BLOCK, S5_STATE, S5_GROUP)
        w = jnp.einsum('ldrgpc,gh->ldrgchp', bb6, eye)
        return w.reshape(depth, 2, S5_LANE_BLOCKS, LANES, S5_BLOCK_STATE)

    def out_map(c):
        c5 = c.reshape(depth, S5_LANE_BLOCKS, S5_GROUPS_PER_BLOCK, S5_GROUP, S5_STATE)
        w = jnp.einsum('lrgcp,gh->lrgphc', c5, eye)
        return w.reshape(depth, S5_LANE_BLOCKS, S5_BLOCK_STATE, LANES)

    wb = jnp.concatenate([in_map(bb_re), in_map(bb_im)], axis=-1).astype(BF16)
    a = jnp.stack([ab_re.reshape(depth, 2, S5_STATE_W), ab_im.reshape(depth, 2, S5_STATE_W)],
                  axis=2)
    wc = jnp.concatenate([out_map(c_re), out_map(-c_im)], axis=-2).astype(BF16)
    return wb, a, wc


def _s5(proj, wb, a, wc, s5_d, layer, n_ctx):
    t = proj.shape[0]
    nb = t // SEQ_BLOCK
    nbc = n_ctx // SEQ_BLOCK
    depth = s5_d.shape[0]

    def io(reverse, sec):
        return pl.BlockSpec((SEQ_BLOCK, BRANCH_W),
                            lambda st: (_block_order(st, nb, nbc, reverse), sec))

    shape = jax.ShapeDtypeStruct((t, BRANCH_W), F32)
    planes = pltpu.VMEM((4, SEQ_BLOCK, S5_STATE_W), F32)
    return pl.pallas_call(
        _s5_body,
        grid=(nb,),
        in_specs=[io(False, SEC_U), io(True, SEC_U),
                  pl.BlockSpec((None,) + wb.shape[1:], lambda st: (layer, 0, 0, 0, 0)),
                  pl.BlockSpec((None,) + a.shape[1:], lambda st: (layer, 0, 0, 0)),
                  pl.BlockSpec((None,) + wc.shape[1:], lambda st: (layer, 0, 0, 0)),
                  pl.BlockSpec((None, 1, BRANCH_W), lambda st: (layer, 0, 0))],
        out_specs=[io(False, 0), io(True, 0)],
        out_shape=[shape, shape],
        scratch_shapes=[pltpu.VMEM((4, S5_STATE_W), F32), planes, planes],
        compiler_params=_cparams(("arbitrary",)),
        name="s5",
    )(proj, proj, wb, a, wc, s5_d.reshape(depth, 1, BRANCH_W))


def _local_body(of_ref, ob_ref, g_ref, yf_ref, yb_ref, wglu_ref,
                cx_ref, cb_ref, cc_ref, cxp_ref, ccp_ref, cxn_ref, ccn_ref, cw_ref,
                gu_ref, gv_ref, lng_ref, lnb_ref, ws_ref, bs_ref,
                oret_ref, os5_ref, oc_ref, og_ref, *, n_blocks, n_ctx_blocks):
    for h in range(RET_HEADS):
        sl = slice(h * RET_DK, (h + 1) * RET_DK)
        o = of_ref[:, sl] + ob_ref[:, sl]
        mu = jnp.mean(o, axis=-1, keepdims=True)
        var = jnp.mean(jnp.square(o - mu), axis=-1, keepdims=True)
        o = (o - mu) * lax.rsqrt(var + EPS)
        g = g_ref[:, sl]
        oret_ref[:, sl] = (o * (g * jax.nn.sigmoid(g))).astype(oret_ref.dtype)

    y = jax.nn.gelu(yf_ref[...] + yb_ref[...])
    os5_ref[...] = (y * jax.nn.sigmoid(_dot(y.astype(BF16), wglu_ref[...]))).astype(os5_ref.dtype)

    c = pl.program_id(0)
    seq_start = jnp.logical_or(c == 0, c == n_ctx_blocks)
    seq_end = jnp.logical_or(c == n_ctx_blocks - 1, c == n_blocks - 1)
    z = cc_ref[...] * cx_ref[...]
    z_prev = ccp_ref[SUBLANES - 1:SUBLANES, :] * cxp_ref[SUBLANES - 1:SUBLANES, :]
    z_next = ccn_ref[0:1, :] * cxn_ref[0:1, :]
    z_prev = jnp.where(seq_start, 0.0, z_prev)
    z_next = jnp.where(seq_end, 0.0, z_next)
    rows = lax.broadcasted_iota(jnp.int32, (SEQ_BLOCK, 1), 0)
    z_up = jnp.where(rows == 0, z_prev, pltpu.roll(z, 1, 0))
    z_dn = jnp.where(rows == SEQ_BLOCK - 1, z_next, pltpu.roll(z, SEQ_BLOCK - 1, 0))
    yc = cw_ref[0:1, :] * z_up + cw_ref[1:2, :] * z + cw_ref[2:3, :] * z_dn
    oc_ref[...] = (cb_ref[...] * yc).astype(oc_ref.dtype)

    u = jax.nn.gelu(gu_ref[...])
    v = jax.nn.gelu(gv_ref[...])
    mu = jnp.mean(v, axis=-1, keepdims=True)
    var = jnp.mean(jnp.square(v - mu), axis=-1, keepdims=True)
    v = ((v - mu) * lax.rsqrt(var + EPS) * lng_ref[...] + lnb_ref[...]).astype(BF16)
    for ci in range(SEQ_BLOCK // CHUNK):
        rws = slice(ci * CHUNK, (ci + 1) * CHUNK)
        for g in range(CMLP_GROUPS):
            sl = slice(g * CMLP_GW, (g + 1) * CMLP_GW)
            mixed = _dot(ws_ref[g], v[rws, sl]) + bs_ref[g]
            og_ref[rws, sl] = (u[rws, sl] * mixed).astype(og_ref.dtype)


def _local_mixers(proj, o_f, o_b, y_f, y_b, wglu_b, conv_w, ln_g, ln_b, ws_b, bs_full,
                  layer, n_ctx):
    t = proj.shape[0]
    nb = t // SEQ_BLOCK
    nbc = n_ctx // SEQ_BLOCK
    per = SEQ_BLOCK // SUBLANES
    last = t // SUBLANES - 1
    depth = conv_w.shape[0]

    def sec(s):
        return pl.BlockSpec((SEQ_BLOCK, BRANCH_W), lambda c: (c, s))

    def prev(s):
        return pl.BlockSpec((SUBLANES, BRANCH_W), lambda c: (jnp.maximum(c * per - 1, 0), s))

    def nxt(s):
        return pl.BlockSpec((SUBLANES, BRANCH_W), lambda c: (jnp.minimum((c + 1) * per, last), s))

    def layer_spec(x):
        nd = x.ndim - 1
        return pl.BlockSpec((None,) + x.shape[1:], lambda c: (layer,) + (0,) * nd)

    ln_g3 = ln_g.reshape(depth, 1, BRANCH_W)
    ln_b3 = ln_b.reshape(depth, 1, BRANCH_W)
    blk = sec(0)
    out_shape = jax.ShapeDtypeStruct((t, BRANCH_W), BF16)
    return pl.pallas_call(
        functools.partial(_local_body, n_blocks=nb, n_ctx_blocks=nbc),
        grid=(nb,),
        in_specs=[blk, blk, sec(SEC_G), blk, blk, layer_spec(wglu_b),
                  sec(SEC_CX), sec(SEC_CB), sec(SEC_CC),
                  prev(SEC_CX), prev(SEC_CC), nxt(SEC_CX), nxt(SEC_CC), layer_spec(conv_w),
                  sec(SEC_GU), sec(SEC_GV), layer_spec(ln_g3), layer_spec(ln_b3),
                  layer_spec(ws_b), layer_spec(bs_full)],
        out_specs=[blk, blk, blk, blk],
        out_shape=[out_shape] * 4,
        compiler_params=_cparams(("parallel",)),
        name="local_mixers",
    )(o_f, o_b, proj, y_f, y_b, wglu_b,
      proj, proj, proj, proj, proj, proj, proj, conv_w,
      proj, proj, ln_g3, ln_b3, ws_b, bs_full)


def _final_norm_body(x_ref, g_ref, o_ref, *, tm):
    g = g_ref[...]

    def slab(r, carry):
        start = pl.multiple_of(r * BF16_ROWS, BF16_ROWS)
        x = x_ref[pl.ds(start, BF16_ROWS), :]
        inv = lax.rsqrt(jnp.mean(x * x, axis=-1, keepdims=True) + EPS)
        o_ref[pl.ds(start, BF16_ROWS), :] = x * inv * g
        return carry

    lax.fori_loop(0, tm // BF16_ROWS, slab, 0, unroll=2)


def _final_norm(x, g, n_ctx):
    t, d = x.shape
    tm = math.gcd(n_ctx, 512)
    skip = n_ctx // tm
    n_lat = t - n_ctx
    return pl.pallas_call(
        functools.partial(_final_norm_body, tm=tm),
        grid=(n_lat // tm,),
        in_specs=[pl.BlockSpec((tm, d), lambda i: (i + skip, 0)),
                  pl.BlockSpec((1, d), lambda i: (0, 0))],
        out_specs=pl.BlockSpec((tm, d), lambda i: (i, 0)),
        out_shape=jax.ShapeDtypeStruct((n_lat, d), F32),
        compiler_params=_cparams(("parallel",)),
        name="final_norm",
    )(x, g.reshape(1, d))


def _rope_tables(n_lat, n_ctx):
    rows = n_lat // GRID_W
    row = jnp.broadcast_to(jnp.arange(rows)[:, None], (rows, GRID_W)).reshape(-1).astype(F32)
    col = jnp.broadcast_to(jnp.arange(GRID_W)[None, :], (rows, GRID_W)).reshape(-1).astype(F32)
    nf = RET_DK // 4
    freqs = ROPE_BASE ** (-jnp.arange(nf, dtype=F32) / nf)
    ang = jnp.concatenate([row[:, None] * freqs, col[:, None] * freqs], axis=-1)
    cos, sin = jnp.cos(ang), jnp.sin(ang)
    cos2 = jnp.concatenate([cos, cos], axis=-1)
    sin2 = jnp.concatenate([-sin, sin], axis=-1)
    cos2 = jnp.concatenate([jnp.ones((n_ctx, RET_DK), F32), cos2], axis=0)
    sin2 = jnp.concatenate([jnp.zeros((n_ctx, RET_DK), F32), sin2], axis=0)
    return cos2, sin2


def kernel(x, c, ctx, c_ctx, ada_w, ada_b, norm1_g, norm2_g, w_in, ret_decay_logit, s5_a_re, s5_a_im, s5_b_re, s5_b_im, s5_c_re, s5_c_im, s5_d, s5_log_dt, s5_w_glu, conv_w, cmlp_ln_g, cmlp_ln_b, cmlp_ws, cmlp_bs, w_branch, w_merge, b_merge, w_out, ffn_w1, ffn_w3, ffn_w2, final_norm_g):
    batch, n_lat, d = x.shape
    assert batch == 1 and c.shape[0] == 1 and ctx.shape[0] == 1
    n_ctx = ctx.shape[1]
    assert n_ctx % SEQ_BLOCK == 0 and n_lat % SEQ_BLOCK == 0 and n_lat % GRID_W == 0
    depth = ada_w.shape[0]

    cos2, sin2 = _rope_tables(n_lat, n_ctx)
    ret_tables = _retention_tables(jax.nn.log_sigmoid(ret_decay_logit.astype(F32)))
    s5_wb, s5_a, s5_wc = _s5_prepare(s5_a_re, s5_a_im, s5_b_re, s5_b_im, s5_log_dt,
                                     s5_c_re, s5_c_im)
    wglu_b = s5_w_glu.astype(BF16)
    ws_b = cmlp_ws.astype(BF16)
    bs_full = jnp.broadcast_to(cmlp_bs[..., None], cmlp_bs.shape + (CMLP_GW,))

    cvec = jnp.zeros((SUBLANES, d), F32).at[0].set(c[0]).at[1].set(c_ctx)
    mod_all = _adaln(cvec, ada_w, ada_b)
    mod_all = mod_all[:, 0:2].reshape(depth, 2, 6, d)
    xs = jnp.concatenate([ctx[0], x[0]], axis=0)

    for l in range(depth):
        sh1, sc1, g1, sh2, sc2, g2 = [mod_all[l, :, i] for i in range(6)]
        h = _modulate(xs, norm1_g[l], sc1, sh1, n_ctx)
        proj = _matmul(h, w_in, l, 1280, 528, 2, F32)

        o_f, o_b = _retention(proj, cos2, sin2, ret_tables, l, n_ctx)
        y_f, y_b = _s5(proj, s5_wb, s5_a, s5_wc, s5_d, l, n_ctx)
        outs = _local_mixers(proj, o_f, o_b, y_f, y_b, wglu_b, conv_w, cmlp_ln_g, cmlp_ln_b,
                             ws_b, bs_full, l, n_ctx)

        merged = _merge(h, outs, w_merge, b_merge, w_branch, l, 512)
        xs, h2 = _matmul_residual_modulate(merged, w_out, l, xs, g1, norm2_g[l], sc2, sh2,
                                           n_ctx, 352)
        act = _ffn_up(h2, ffn_w1, ffn_w3, l, 512, 1056, 2)
        xs = _matmul_residual(act, ffn_w2, l, xs, g2, n_ctx, 1024, 352)

    return _final_norm(xs, final_norm_g, n_ctx)[None]
```

```python
import functools
import math

import jax
import jax.numpy as jnp
from jax import lax
from jax.experimental import pallas as pl
from jax.experimental.pallas import tpu as pltpu

F32 = jnp.float32
BF16 = jnp.bfloat16

D_MODEL = 2048
GRID_W = 64
BRANCH_W = 512
N_BRANCH = 4
N_SECTIONS = 10
RET_HEADS = 4
RET_DK = BRANCH_W // RET_HEADS
ROPE_BASE = 10000.0
S5_GROUP = 16
S5_GROUPS = BRANCH_W // S5_GROUP
S5_STATE = 64
CMLP_GROUPS = 4
CMLP_GW = BRANCH_W // CMLP_GROUPS
EPS = 1e-6

CHUNK = 128
SEQ_BLOCK = 256
LANES = 128
SUBLANES = 8
BF16_ROWS = 16
HALO_ROWS = BF16_ROWS
S5_LANE_BLOCKS = BRANCH_W // LANES
S5_GROUPS_PER_BLOCK = LANES // S5_GROUP
S5_BLOCK_STATE = S5_GROUPS_PER_BLOCK * S5_STATE
S5_STATE_W = S5_GROUPS * S5_STATE
MIB = 1024 * 1024
VMEM_BUDGET_MIB = 56

SEC_Q, SEC_K, SEC_V, SEC_G, SEC_U, SEC_CX, SEC_CB, SEC_CC, SEC_GU, SEC_GV = range(N_SECTIONS)


def _cparams(semantics, vmem_mib=VMEM_BUDGET_MIB):
    return pltpu.CompilerParams(dimension_semantics=semantics,
                                vmem_limit_bytes=vmem_mib * MIB)


def _dot(a, b):
    return jnp.dot(a, b, preferred_element_type=F32)


def _row_tile(t, cap=1056):
    for tm in (1056, 1024, 768, 704, 528, 512, 384, 352, 256, 128):
        if tm <= cap and t % tm == 0:
            return tm
    raise ValueError(f"unsupported token count {t}")


def _is_ctx_rows(tile_idx, tm, n_ctx):
    rows = tile_idx * tm + lax.broadcasted_iota(jnp.int32, (tm, 1), 0)
    return rows < n_ctx


def _weight_spec(block, index_map, buffers=1):
    return pl.BlockSpec(block, index_map, pipeline_mode=pl.Buffered(buffers))


def _whole(x):
    nd = x.ndim
    return pl.BlockSpec(x.shape, lambda *_: (0,) * nd)


def _adaln_body(c_ref, w_ref, b_ref, o_ref):
    c = c_ref[...]
    s = c * jax.nn.sigmoid(c)
    w = w_ref[...]
    s_hi = s.astype(BF16)
    s_lo = (s - s_hi.astype(F32)).astype(BF16)
    w_hi = w.astype(BF16)
    w_lo = (w - w_hi.astype(F32)).astype(BF16)
    acc = _dot(s_hi, w_hi) + _dot(s_lo, w_hi) + _dot(s_hi, w_lo)
    o_ref[...] = acc + b_ref[...]


def _adaln(cvec, ada_w, ada_b):
    depth, d, n = ada_w.shape
    tn = 1024
    return pl.pallas_call(
        _adaln_body,
        grid=(depth, n // tn),
        in_specs=[
            pl.BlockSpec((SUBLANES, d), lambda l, j: (0, 0)),
            pl.BlockSpec((None, d, tn), lambda l, j: (l, 0, j)),
            pl.BlockSpec((None, 1, tn), lambda l, j: (l, 0, j)),
        ],
        out_specs=pl.BlockSpec((None, SUBLANES, tn), lambda l, j: (l, 0, j)),
        out_shape=jax.ShapeDtypeStruct((depth, SUBLANES, n), F32),
        compiler_params=_cparams(("parallel", "parallel")),
        name="adaln",
    )(cvec, ada_w, ada_b.reshape(depth, 1, n))


def _modulate_body(x_ref, g_ref, sc_ref, sh_ref, o_ref, *, tm, n_ctx):
    base = pl.program_id(0) * tm
    g = g_ref[...]
    gain_lat = g * (1.0 + sc_ref[0:1, :])
    gain_ctx = g * (1.0 + sc_ref[1:2, :])
    shift_lat = sh_ref[0:1, :]
    shift_ctx = sh_ref[1:2, :]

    def slab(r, carry):
        start = pl.multiple_of(r * BF16_ROWS, BF16_ROWS)
        x = x_ref[pl.ds(start, BF16_ROWS), :]
        inv = lax.rsqrt(jnp.mean(x * x, axis=-1, keepdims=True) + EPS)
        is_ctx = (base + start) < n_ctx
        gain = jnp.where(is_ctx, gain_ctx, gain_lat)
        shift = jnp.where(is_ctx, shift_ctx, shift_lat)
        o_ref[pl.ds(start, BF16_ROWS), :] = (x * inv * gain + shift).astype(o_ref.dtype)
        return carry

    lax.fori_loop(0, tm // BF16_ROWS, slab, 0, unroll=3)


def _modulate(x, g, sc, sh, n_ctx):
    t, d = x.shape
    tm = _row_tile(t, cap=528)
    return pl.pallas_call(
        functools.partial(_modulate_body, tm=tm, n_ctx=n_ctx),
        grid=(t // tm,),
        in_specs=[
            pl.BlockSpec((tm, d), lambda i: (i, 0)),
            pl.BlockSpec((1, d), lambda i: (0, 0)),
            pl.BlockSpec((2, d), lambda i: (0, 0)),
            pl.BlockSpec((2, d), lambda i: (0, 0)),
        ],
        out_specs=pl.BlockSpec((tm, d), lambda i: (i, 0)),
        out_shape=jax.ShapeDtypeStruct((t, d), BF16),
        compiler_params=_cparams(("parallel",)),
        name="modulate",
    )(x, g.reshape(1, d), sc, sh)


def _mm_body(x_ref, w_ref, o_ref, wb_ref):
    @pl.when(pl.program_id(1) == 0)
    def _():
        wb_ref[...] = w_ref[...].astype(BF16)

    o_ref[...] = _dot(x_ref[...], wb_ref[...]).astype(o_ref.dtype)


def _matmul(x, w_stack, layer, tn, tm_cap, w_buffers, out_dtype):
    t, k = x.shape
    n = w_stack.shape[-1]
    tm = _row_tile(t, tm_cap)
    return pl.pallas_call(
        _mm_body,
        grid=(n // tn, t // tm),
        in_specs=[
            pl.BlockSpec((tm, k), lambda j, i: (i, 0)),
            _weight_spec((None, k, tn), lambda j, i: (layer, 0, j), w_buffers),
        ],
        out_specs=pl.BlockSpec((tm, tn), lambda j, i: (i, j)),
        out_shape=jax.ShapeDtypeStruct((t, n), out_dtype),
        scratch_shapes=[pltpu.VMEM((k, tn), BF16)],
        compiler_params=_cparams(("arbitrary", "arbitrary")),
        name="matmul",
    )(x, w_stack)


def _mm_residual_body(a_ref, w_ref, x_ref, g_ref, o_ref, wb_ref, *, tm, n_ctx):
    @pl.when(pl.program_id(1) == 0)
    def _():
        wb_ref[...] = w_ref[...].astype(BF16)

    y = _dot(a_ref[...], wb_ref[...])
    is_ctx = _is_ctx_rows(pl.program_id(1), tm, n_ctx)
    gate = jnp.where(is_ctx, g_ref[1:2, :], g_ref[0:1, :])
    o_ref[...] = x_ref[...] + gate * y


def _matmul_residual(a, w_stack, layer, x, gate, n_ctx, tn, tm_cap):
    t, k = a.shape
    n = w_stack.shape[-1]
    tm = _row_tile(t, tm_cap)
    return pl.pallas_call(
        functools.partial(_mm_residual_body, tm=tm, n_ctx=n_ctx),
        grid=(n // tn, t // tm),
        in_specs=[
            pl.BlockSpec((tm, k), lambda j, i: (i, 0)),
            _weight_spec((None, k, tn), lambda j, i: (layer, 0, j)),
            pl.BlockSpec((tm, tn), lambda j, i: (i, j)),
            pl.BlockSpec((2, tn), lambda j, i: (0, j)),
        ],
        out_specs=pl.BlockSpec((tm, tn), lambda j, i: (i, j)),
        out_shape=jax.ShapeDtypeStruct((t, n), F32),
        scratch_shapes=[pltpu.VMEM((k, tn), BF16)],
        compiler_params=_cparams(("arbitrary", "arbitrary")),
        name="matmul_residual",
    )(a, w_stack, x, gate)


def _mm_residual_modulate_body(a_ref, w_ref, x_ref, gate_ref, g_ref, sc_ref, sh_ref,
                               xo_ref, ho_ref, wb_ref, y_ref, *, tm, n_ctx):
    @pl.when(pl.program_id(0) == 0)
    def _():
        wb_ref[...] = w_ref[...].astype(BF16)

    y_ref[...] = _dot(a_ref[...], wb_ref[...])
    base = pl.program_id(0) * tm
    g = g_ref[...]
    gain_lat = g * (1.0 + sc_ref[0:1, :])
    gain_ctx = g * (1.0 + sc_ref[1:2, :])

    def slab(r, carry):
        start = pl.multiple_of(r * BF16_ROWS, BF16_ROWS)
        rows = pl.ds(start, BF16_ROWS)
        is_ctx = (base + start) < n_ctx
        gate = jnp.where(is_ctx, gate_ref[1:2, :], gate_ref[0:1, :])
        x = x_ref[rows, :] + gate * y_ref[rows, :]
        xo_ref[rows, :] = x
        inv = lax.rsqrt(jnp.mean(x * x, axis=-1, keepdims=True) + EPS)
        gain = jnp.where(is_ctx, gain_ctx, gain_lat)
        shift = jnp.where(is_ctx, sh_ref[1:2, :], sh_ref[0:1, :])
        ho_ref[rows, :] = (x * inv * gain + shift).astype(ho_ref.dtype)
        return carry

    lax.fori_loop(0, tm // BF16_ROWS, slab, 0, unroll=2)


def _matmul_residual_modulate(a, w_stack, layer, x, gate, g, sc, sh, n_ctx, tm_cap):
    t, k = a.shape
    n = w_stack.shape[-1]
    tm = _row_tile(t, tm_cap)
    full = pl.BlockSpec((tm, n), lambda i: (i, 0))
    vec2 = pl.BlockSpec((2, n), lambda i: (0, 0))
    return pl.pallas_call(
        functools.partial(_mm_residual_modulate_body, tm=tm, n_ctx=n_ctx),
        grid=(t // tm,),
        in_specs=[
            pl.BlockSpec((tm, k), lambda i: (i, 0)),
            _weight_spec((None, k, n), lambda i: (layer, 0, 0)),
            full, vec2,
            pl.BlockSpec((1, n), lambda i: (0, 0)), vec2, vec2,
        ],
        out_specs=[full, full],
        out_shape=[jax.ShapeDtypeStruct((t, n), F32), jax.ShapeDtypeStruct((t, n), BF16)],
        scratch_shapes=[pltpu.VMEM((k, n), BF16), pltpu.VMEM((tm, n), F32)],
        compiler_params=_cparams(("arbitrary",)),
        name="matmul_residual_modulate",
    )(a, w_stack, x, gate, g.reshape(1, n), sc, sh)


def _ffn_up_body(h_ref, w1_ref, w3_ref, o_ref, w1b_ref, w3b_ref):
    @pl.when(pl.program_id(1) == 0)
    def _():
        w1b_ref[...] = w1_ref[...].astype(BF16)
        w3b_ref[...] = w3_ref[...].astype(BF16)

    h = h_ref[...]
    a = _dot(h, w1b_ref[...])
    b = _dot(h, w3b_ref[...])
    o_ref[...] = (a * jax.nn.sigmoid(a) * b).astype(o_ref.dtype)


def _ffn_up(h, w1_stack, w3_stack, layer, tn, tm_cap, w_buffers):
    t, k = h.shape
    n = w1_stack.shape[-1]
    tm = _row_tile(t, tm_cap)
    wspec = _weight_spec((None, k, tn), lambda j, i: (layer, 0, j), w_buffers)
    return pl.pallas_call(
        _ffn_up_body,
        grid=(n // tn, t // tm),
        in_specs=[pl.BlockSpec((tm, k), lambda j, i: (i, 0)), wspec, wspec],
        out_specs=pl.BlockSpec((tm, tn), lambda j, i: (i, j)),
        out_shape=jax.ShapeDtypeStruct((t, n), BF16),
        scratch_shapes=[pltpu.VMEM((k, tn), BF16), pltpu.VMEM((k, tn), BF16)],
        compiler_params=_cparams(("arbitrary", "arbitrary")),
        name="ffn_up",
    )(h, w1_stack, w3_stack)


def _merge_body(*refs):
    nb = N_BRANCH
    h_ref = refs[0]
    o_refs = refs[1:1 + nb]
    wm_refs = refs[1 + nb:1 + 2 * nb]
    bm_refs = refs[1 + 2 * nb:1 + 3 * nb]
    wb_refs = refs[1 + 3 * nb:1 + 4 * nb]
    out_ref, wmb_ref, wbb_ref = refs[1 + 4 * nb:]

    @pl.when(pl.program_id(1) == 0)
    def _():
        for k in range(nb):
            wmb_ref[k] = wm_refs[k][...].astype(BF16)
            wbb_ref[k] = wb_refs[k][...].astype(BF16)

    h = h_ref[...]
    acc = None
    for k in range(nb):
        gate = jax.nn.sigmoid(_dot(h, wmb_ref[k]) + bm_refs[k][...])
        term = gate * _dot(o_refs[k][...], wbb_ref[k])
        acc = term if acc is None else acc + term
    out_ref[...] = acc.astype(out_ref.dtype)


def _merge(h, outs, w_merge, b_merge, w_branch, layer, tn, tm_cap, w_buffers):
    t, d = h.shape
    bw = outs[0].shape[1]
    depth = w_merge.shape[0]
    tm = _row_tile(t, tm_cap)
    nt = d // tn
    b_merge3 = b_merge.reshape(depth, 1, N_BRANCH * d)
    in_specs = [pl.BlockSpec((tm, d), lambda j, i: (i, 0))]
    in_specs += [pl.BlockSpec((tm, bw), lambda j, i: (i, 0)) for _ in range(N_BRANCH)]
    in_specs += [_weight_spec((None, d, tn), lambda j, i, k=k: (layer, 0, k * nt + j), w_buffers)
                 for k in range(N_BRANCH)]
    in_specs += [pl.BlockSpec((None, 1, tn), lambda j, i, k=k: (layer, 0, k * nt + j))
                 for k in range(N_BRANCH)]
    in_specs += [_weight_spec((None, None, bw, tn), lambda j, i, k=k: (layer, k, 0, j), w_buffers)
                 for k in range(N_BRANCH)]
    return pl.pallas_call(
        _merge_body,
        grid=(nt, t // tm),
        in_specs=in_specs,
        out_specs=pl.BlockSpec((tm, tn), lambda j, i: (i, j)),
        out_shape=jax.ShapeDtypeStruct((t, d), BF16),
        scratch_shapes=[pltpu.VMEM((N_BRANCH, d, tn), BF16),
                        pltpu.VMEM((N_BRANCH, bw, tn), BF16)],
        compiler_params=_cparams(("arbitrary", "arbitrary")),
        name="merge",
    )(h, *outs, *([w_merge] * N_BRANCH), *([b_merge3] * N_BRANCH), *([w_branch] * N_BRANCH))


def _block_order(step, n_blocks, n_ctx_blocks, reverse):
    if not reverse:
        return step
    return jnp.where(step < n_ctx_blocks, n_ctx_blocks - 1 - step,
                     n_blocks + n_ctx_blocks - 1 - step)


def _retention_body(qf_ref, kf_ref, vf_ref, cosf_ref, sinf_ref,
                    qb_ref, kb_ref, vb_ref, cosb_ref, sinb_ref, tab_ref,
                    of_ref, ob_ref, s_ref):
    @pl.when(pl.program_id(0) == 0)
    def _():
        s_ref[...] = jnp.zeros_like(s_ref)

    scale = RET_DK ** -0.5
    chunks = SEQ_BLOCK // CHUNK
    dirs = ((qf_ref, kf_ref, vf_ref, cosf_ref, sinf_ref, of_ref, range(chunks)),
            (qb_ref, kb_ref, vb_ref, cosb_ref, sinb_ref, ob_ref, range(chunks - 1, -1, -1)))
    for d, (q_ref, k_ref, v_ref, cos_ref, sin_ref, o_ref, order) in enumerate(dirs):
        for h in range(RET_HEADS):
            sl = slice(h * RET_DK, (h + 1) * RET_DK)
            dmat = tab_ref[d, 0, h]
            q_decay = tab_ref[d, 1, h]
            k_decay = tab_ref[d, 2, h]
            g_chunk = tab_ref[d, 3, h]
            state = s_ref[d, h]
            for ci in order:
                rows = slice(ci * CHUNK, (ci + 1) * CHUNK)
                cos = cos_ref[rows, :]
                sin = sin_ref[rows, :]
                q = q_ref[rows, sl].astype(F32)
                k = k_ref[rows, sl].astype(F32)
                v = v_ref[rows, sl].astype(BF16)
                qr = (q * cos + pltpu.roll(q, RET_DK // 2, 1) * sin) * scale
                kr = k * cos + pltpu.roll(k, RET_DK // 2, 1) * sin
                scores = lax.dot_general(qr.astype(BF16), kr.astype(BF16),
                                         (((1,), (1,)), ((), ())),
                                         preferred_element_type=F32) * dmat
                intra = _dot(scores.astype(BF16), v)
                cross = _dot((qr * q_decay).astype(BF16), state.astype(BF16))
                kv = _dot((kr * k_decay).T.astype(BF16), v)
                o_ref[rows, sl] = intra + cross
                state = g_chunk * state + kv
            s_ref[d, h] = state


def _retention_tables(log_gamma):
    pos = jnp.arange(CHUNK, dtype=F32)
    i = pos[:, None]
    j = pos[None, :]
    full = log_gamma.shape[:1] + (RET_HEADS, CHUNK, RET_DK)

    def one_dir(lg, diff, mask, q_pow, k_pow):
        dmat = jnp.where(mask, jnp.exp(lg[:, :, None, None] * jnp.maximum(diff, 0.0)), 0.0)
        q_decay = jnp.broadcast_to(jnp.exp(lg[:, :, None] * q_pow)[..., None], full)
        k_decay = jnp.broadcast_to(jnp.exp(lg[:, :, None] * k_pow)[..., None], full)
        g_chunk = jnp.broadcast_to(jnp.exp(lg * CHUNK)[:, :, None, None], full)
        return jnp.stack([dmat, q_decay, k_decay, g_chunk], axis=1)

    fwd = one_dir(log_gamma[:, 0], i - j, (i - j) >= 0, pos + 1.0, CHUNK - 1.0 - pos)
    bwd = one_dir(log_gamma[:, 1], j - i, (j - i) > 0, CHUNK - pos, pos)
    return jnp.stack([fwd, bwd], axis=1)


def _retention(proj, cos2, sin2, tables, layer, n_ctx):
    t = proj.shape[0]
    nb = t // SEQ_BLOCK
    nbc = n_ctx // SEQ_BLOCK

    def specs(reverse):
        def blk(st):
            return _block_order(st, nb, nbc, reverse)
        sec = [pl.BlockSpec((SEQ_BLOCK, BRANCH_W), lambda st, s=s: (blk(st), s))
               for s in (SEC_Q, SEC_K, SEC_V)]
        row = pl.BlockSpec((SEQ_BLOCK, RET_DK), lambda st: (blk(st), 0))
        out = pl.BlockSpec((SEQ_BLOCK, BRANCH_W), lambda st: (blk(st), 0))
        return sec + [row, row], out

    in_f, out_f = specs(False)
    in_b, out_b = specs(True)
    tab_spec = pl.BlockSpec((None, 2, 4, RET_HEADS, CHUNK, RET_DK),
                            lambda st: (layer, 0, 0, 0, 0, 0))
    shape = jax.ShapeDtypeStruct((t, BRANCH_W), F32)
    return pl.pallas_call(
        _retention_body,
        grid=(nb,),
        in_specs=in_f + in_b + [tab_spec],
        out_specs=[out_f, out_b],
        out_shape=[shape, shape],
        scratch_shapes=[pltpu.VMEM((2, RET_HEADS, RET_DK, RET_DK), F32)],
        compiler_params=_cparams(("arbitrary",)),
        name="retention",
    )(proj, proj, proj, cos2, sin2, proj, proj, proj, cos2, sin2, tables)


def _s5_body(uf_ref, ub_ref, wb_ref, a_ref, wc_ref, d_ref, yf_ref, yb_ref,
             x_ref, bu_ref, st_ref):
    @pl.when(pl.program_id(0) == 0)
    def _():
        x_ref[...] = jnp.zeros_like(x_ref)

    for d, u_ref in enumerate((uf_ref, ub_ref)):
        ub = u_ref[...].astype(BF16)
        for r in range(S5_LANE_BLOCKS):
            bu = _dot(ub[:, r * LANES:(r + 1) * LANES], wb_ref[d, r])
            cols = slice(r * S5_BLOCK_STATE, (r + 1) * S5_BLOCK_STATE)
            bu_ref[2 * d, :, cols] = bu[:, :S5_BLOCK_STATE]
            bu_ref[2 * d + 1, :, cols] = bu[:, S5_BLOCK_STATE:]

    af_re, af_im = a_ref[0, 0:1, :], a_ref[0, 1:2, :]
    ab_re, ab_im = a_ref[1, 0:1, :], a_ref[1, 1:2, :]

    def step(i, carry):
        f_re, f_im, b_re, b_im = carry
        tf = pl.ds(i, 1)
        tb = pl.ds(SEQ_BLOCK - 1 - i, 1)
        nf_re = af_re * f_re - af_im * f_im + bu_ref[0, tf, :]
        nf_im = af_re * f_im + af_im * f_re + bu_ref[1, tf, :]
        nb_re = ab_re * b_re - ab_im * b_im + bu_ref[2, tb, :]
        nb_im = ab_re * b_im + ab_im * b_re + bu_ref[3, tb, :]
        st_ref[0, tf, :] = nf_re
        st_ref[1, tf, :] = nf_im
        st_ref[2, tb, :] = nb_re
        st_ref[3, tb, :] = nb_im
        return nf_re, nf_im, nb_re, nb_im

    carry = lax.fori_loop(0, SEQ_BLOCK, step,
                          (x_ref[0:1, :], x_ref[1:2, :], x_ref[2:3, :], x_ref[3:4, :]),
                          unroll=8)
    for n, val in enumerate(carry):
        x_ref[n:n + 1, :] = val

    def readout(d):
        ys = []
        for r in range(S5_LANE_BLOCKS):
            cols = slice(r * S5_BLOCK_STATE, (r + 1) * S5_BLOCK_STATE)
            st = jnp.concatenate([st_ref[2 * d, :, cols], st_ref[2 * d + 1, :, cols]],
                                 axis=1).astype(BF16)
            ys.append(_dot(st, wc_ref[r]))
        return jnp.concatenate(ys, axis=1)

    yf_ref[...] = readout(0) + d_ref[...] * uf_ref[...].astype(F32)
    yb_ref[...] = readout(1)


def _s5_prepare(a_re, a_im, b_re, b_im, log_dt, c_re, c_im):
    depth = a_re.shape[0]
    a_re = a_re[:, None]
    a_im = a_im[:, None]
    dt = jnp.exp(log_dt)[..., None]
    mag = jnp.exp(dt * a_re)
    ang = dt * a_im
    ab_re, ab_im = mag * jnp.cos(ang), mag * jnp.sin(ang)
    nr, ni = ab_re - 1.0, ab_im
    den = a_re * a_re + a_im * a_im
    f_re = (nr * a_re + ni * a_im) / den
    f_im = (ni * a_re - nr * a_im) / den
    bb_re = f_re[..., None] * b_re - f_im[..., None] * b_im
    bb_im = f_re[..., None] * b_im + f_im[..., None] * b_re
    eye = jnp.eye(S5_GROUPS_PER_BLOCK, dtype=F32)

    def in_map(bb):
        bb6 = bb.reshape(depth, 2, S5_LANE_BLOCKS, S5_GROUPS_PER_BLOCK, S5_STATE, S5_GROUP)
        w = jnp.einsum('ldrgpc,gh->ldrgchp', bb6, eye)
        return w.reshape(depth, 2, S5_LANE_BLOCKS, LANES, S5_BLOCK_STATE)

    def out_map(c):
        c5 = c.reshape(depth, S5_LANE_BLOCKS, S5_GROUPS_PER_BLOCK, S5_GROUP, S5_STATE)
        w = jnp.einsum('lrgcp,gh->lrgphc', c5, eye)
        return w.reshape(depth, S5_LANE_BLOCKS, S5_BLOCK_STATE, LANES)

    wb = jnp.concatenate([in_map(bb_re), in_map(bb_im)], axis=-1).astype(BF16)
    a = jnp.stack([ab_re.reshape(depth, 2, S5_STATE_W), ab_im.reshape(depth, 2, S5_STATE_W)],
                  axis=2)
    wc = jnp.concatenate([out_map(c_re), out_map(-c_im)], axis=-2).astype(BF16)
    return wb, a, wc


def _s5(proj, wb, a, wc, s5_d, layer, n_ctx):
    t = proj.shape[0]
    nb = t // SEQ_BLOCK
    nbc = n_ctx // SEQ_BLOCK
    depth = s5_d.shape[0]

    def io(reverse, sec):
        return pl.BlockSpec((SEQ_BLOCK, BRANCH_W),
                            lambda st: (_block_order(st, nb, nbc, reverse), sec))

    shape = jax.ShapeDtypeStruct((t, BRANCH_W), F32)
    planes = pltpu.VMEM((4, SEQ_BLOCK, S5_STATE_W), F32)
    return pl.pallas_call(
        _s5_body,
        grid=(nb,),
        in_specs=[io(False, SEC_U), io(True, SEC_U),
                  pl.BlockSpec((None,) + wb.shape[1:], lambda st: (layer, 0, 0, 0, 0)),
                  pl.BlockSpec((None,) + a.shape[1:], lambda st: (layer, 0, 0, 0)),
                  pl.BlockSpec((None,) + wc.shape[1:], lambda st: (layer, 0, 0, 0)),
                  pl.BlockSpec((None, 1, BRANCH_W), lambda st: (layer, 0, 0))],
        out_specs=[io(False, 0), io(True, 0)],
        out_shape=[shape, shape],
        scratch_shapes=[pltpu.VMEM((4, S5_STATE_W), F32), planes, planes],
        compiler_params=_cparams(("arbitrary",)),
        name="s5",
    )(proj, proj, wb, a, wc, s5_d.reshape(depth, 1, BRANCH_W))


def _local_body(of_ref, ob_ref, g_ref, yf_ref, yb_ref, wglu_ref,
                cx_ref, cb_ref, cc_ref, cxp_ref, ccp_ref, cxn_ref, ccn_ref, cw_ref,
                gu_ref, gv_ref, lng_ref, lnb_ref, ws_ref, bs_ref,
                oret_ref, os5_ref, oc_ref, og_ref, *, n_blocks, n_ctx_blocks):
    for h in range(RET_HEADS):
        sl = slice(h * RET_DK, (h + 1) * RET_DK)
        o = of_ref[:, sl] + ob_ref[:, sl]
        mu = jnp.mean(o, axis=-1, keepdims=True)
        var = jnp.mean(jnp.square(o - mu), axis=-1, keepdims=True)
        o = (o - mu) * lax.rsqrt(var + EPS)
        g = g_ref[:, sl].astype(F32)
        oret_ref[:, sl] = (o * (g * jax.nn.sigmoid(g))).astype(oret_ref.dtype)

    y = jax.nn.gelu(yf_ref[...] + yb_ref[...])
    os5_ref[...] = (y * jax.nn.sigmoid(_dot(y.astype(BF16), wglu_ref[...]))).astype(os5_ref.dtype)

    c = pl.program_id(0)
    seq_start = jnp.logical_or(c == 0, c == n_ctx_blocks)
    seq_end = jnp.logical_or(c == n_ctx_blocks - 1, c == n_blocks - 1)
    z = cc_ref[...].astype(F32) * cx_ref[...].astype(F32)
    z_prev = (ccp_ref[...].astype(F32) * cxp_ref[...].astype(F32))[HALO_ROWS - 1:HALO_ROWS, :]
    z_next = (ccn_ref[...].astype(F32) * cxn_ref[...].astype(F32))[0:1, :]
    z_prev = jnp.where(seq_start, 0.0, z_prev)
    z_next = jnp.where(seq_end, 0.0, z_next)
    rows = lax.broadcasted_iota(jnp.int32, (SEQ_BLOCK, 1), 0)
    z_up = jnp.where(rows == 0, z_prev, pltpu.roll(z, 1, 0))
    z_dn = jnp.where(rows == SEQ_BLOCK - 1, z_next, pltpu.roll(z, SEQ_BLOCK - 1, 0))
    yc = cw_ref[0:1, :] * z_up + cw_ref[1:2, :] * z + cw_ref[2:3, :] * z_dn
    oc_ref[...] = (cb_ref[...].astype(F32) * yc).astype(oc_ref.dtype)

    u = jax.nn.gelu(gu_ref[...].astype(F32))
    v = jax.nn.gelu(gv_ref[...].astype(F32))
    mu = jnp.mean(v, axis=-1, keepdims=True)
    var = jnp.mean(jnp.square(v - mu), axis=-1, keepdims=True)
    v = ((v - mu) * lax.rsqrt(var + EPS) * lng_ref[...] + lnb_ref[...]).astype(BF16)
    for ci in range(SEQ_BLOCK // CHUNK):
        rws = slice(ci * CHUNK, (ci + 1) * CHUNK)
        for g in range(CMLP_GROUPS):
            sl = slice(g * CMLP_GW, (g + 1) * CMLP_GW)
            mixed = _dot(ws_ref[g], v[rws, sl]) + bs_ref[g]
            og_ref[rws, sl] = (u[rws, sl] * mixed).astype(og_ref.dtype)


def _local_mixers(proj, o_f, o_b, y_f, y_b, wglu_b, conv_w, ln_g, ln_b, ws_b, bs_full,
                  layer, n_ctx):
    t = proj.shape[0]
    nb = t // SEQ_BLOCK
    nbc = n_ctx // SEQ_BLOCK
    per = SEQ_BLOCK // HALO_ROWS
    last = t // HALO_ROWS - 1
    depth = conv_w.shape[0]

    def sec(s):
        return pl.BlockSpec((SEQ_BLOCK, BRANCH_W), lambda c: (c, s))

    def prev(s):
        return pl.BlockSpec((HALO_ROWS, BRANCH_W), lambda c: (jnp.maximum(c * per - 1, 0), s))

    def nxt(s):
        return pl.BlockSpec((HALO_ROWS, BRANCH_W), lambda c: (jnp.minimum((c + 1) * per, last), s))

    def layer_spec(x):
        nd = x.ndim - 1
        return pl.BlockSpec((None,) + x.shape[1:], lambda c: (layer,) + (0,) * nd)

    ln_g3 = ln_g.reshape(depth, 1, BRANCH_W)
    ln_b3 = ln_b.reshape(depth, 1, BRANCH_W)
    blk = sec(0)
    out_shape = jax.ShapeDtypeStruct((t, BRANCH_W), BF16)
    return pl.pallas_call(
        functools.partial(_local_body, n_blocks=nb, n_ctx_blocks=nbc),
        grid=(nb,),
        in_specs=[blk, blk, sec(SEC_G), blk, blk, layer_spec(wglu_b),
                  sec(SEC_CX), sec(SEC_CB), sec(SEC_CC),
                  prev(SEC_CX), prev(SEC_CC), nxt(SEC_CX), nxt(SEC_CC), layer_spec(conv_w),
                  sec(SEC_GU), sec(SEC_GV), layer_spec(ln_g3), layer_spec(ln_b3),
                  layer_spec(ws_b), layer_spec(bs_full)],
        out_specs=[blk, blk, blk, blk],
        out_shape=[out_shape] * 4,
        compiler_params=_cparams(("parallel",)),
        name="local_mixers",
    )(o_f, o_b, proj, y_f, y_b, wglu_b,
      proj, proj, proj, proj, proj, proj, proj, conv_w,
      proj, proj, ln_g3, ln_b3, ws_b, bs_full)


def _final_norm_body(x_ref, g_ref, o_ref, *, tm):
    g = g_ref[...]

    def slab(r, carry):
        start = pl.multiple_of(r * BF16_ROWS, BF16_ROWS)
        x = x_ref[pl.ds(start, BF16_ROWS), :]
        inv = lax.rsqrt(jnp.mean(x * x, axis=-1, keepdims=True) + EPS)
        o_ref[pl.ds(start, BF16_ROWS), :] = x * inv * g
        return carry

    lax.fori_loop(0, tm // BF16_ROWS, slab, 0, unroll=2)


def _final_norm(x, g, n_ctx):
    t, d = x.shape
    tm = math.gcd(n_ctx, 512)
    skip = n_ctx // tm
    n_lat = t - n_ctx
    return pl.pallas_call(
        functools.partial(_final_norm_body, tm=tm),
        grid=(n_lat // tm,),
        in_specs=[pl.BlockSpec((tm, d), lambda i: (i + skip, 0)),
                  pl.BlockSpec((1, d), lambda i: (0, 0))],
        out_specs=pl.BlockSpec((tm, d), lambda i: (i, 0)),
        out_shape=jax.ShapeDtypeStruct((n_lat, d), F32),
        compiler_params=_cparams(("parallel",)),
        name="final_norm",
    )(x, g.reshape(1, d))


def _rope_tables(n_lat, n_ctx):
    rows = n_lat // GRID_W
    row = jnp.broadcast_to(jnp.arange(rows)[:, None], (rows, GRID_W)).reshape(-1).astype(F32)
    col = jnp.broadcast_to(jnp.arange(GRID_W)[None, :], (rows, GRID_W)).reshape(-1).astype(F32)
    nf = RET_DK // 4
    freqs = ROPE_BASE ** (-jnp.arange(nf, dtype=F32) / nf)
    ang = jnp.concatenate([row[:, None] * freqs, col[:, None] * freqs], axis=-1)
    cos, sin = jnp.cos(ang), jnp.sin(ang)
    cos2 = jnp.concatenate([cos, cos], axis=-1)
    sin2 = jnp.concatenate([-sin, sin], axis=-1)
    cos2 = jnp.concatenate([jnp.ones((n_ctx, RET_DK), F32), cos2], axis=0)
    sin2 = jnp.concatenate([jnp.zeros((n_ctx, RET_DK), F32), sin2], axis=0)
    return cos2, sin2


def kernel(x, c, ctx, c_ctx, ada_w, ada_b, norm1_g, norm2_g, w_in, ret_decay_logit, s5_a_re, s5_a_im, s5_b_re, s5_b_im, s5_c_re, s5_c_im, s5_d, s5_log_dt, s5_w_glu, conv_w, cmlp_ln_g, cmlp_ln_b, cmlp_ws, cmlp_bs, w_branch, w_merge, b_merge, w_out, ffn_w1, ffn_w3, ffn_w2, final_norm_g):
    batch, n_lat, d = x.shape
    assert batch == 1 and c.shape[0] == 1 and ctx.shape[0] == 1
    n_ctx = ctx.shape[1]
    assert n_ctx % SEQ_BLOCK == 0 and n_lat % SEQ_BLOCK == 0 and n_lat % GRID_W == 0
    depth = ada_w.shape[0]

    cos2, sin2 = _rope_tables(n_lat, n_ctx)
    ret_tables = _retention_tables(jax.nn.log_sigmoid(ret_decay_logit.astype(F32)))
    s5_wb, s5_a, s5_wc = _s5_prepare(s5_a_re, s5_a_im, s5_b_re, s5_b_im, s5_log_dt,
                                     s5_c_re, s5_c_im)
    wglu_b = s5_w_glu.astype(BF16)
    ws_b = cmlp_ws.astype(BF16)
    bs_full = jnp.broadcast_to(cmlp_bs[..., None], cmlp_bs.shape + (CMLP_GW,))

    cvec = jnp.zeros((SUBLANES, d), F32).at[0].set(c[0]).at[1].set(c_ctx)
    mod_all = _adaln(cvec, ada_w, ada_b)
    mod_all = mod_all[:, 0:2].reshape(depth, 2, 6, d)
    xs = jnp.concatenate([ctx[0], x[0]], axis=0)

    for l in range(depth):
        sh1, sc1, g1, sh2, sc2, g2 = [mod_all[l, :, i] for i in range(6)]
        h = _modulate(xs, norm1_g[l], sc1, sh1, n_ctx)
        proj = _matmul(h, w_in, l, 1280, 528, 2, BF16)

        o_f, o_b = _retention(proj, cos2, sin2, ret_tables, l, n_ctx)
        y_f, y_b = _s5(proj, s5_wb, s5_a, s5_wc, s5_d, l, n_ctx)
        outs = _local_mixers(proj, o_f, o_b, y_f, y_b, wglu_b, conv_w, cmlp_ln_g, cmlp_ln_b,
                             ws_b, bs_full, l, n_ctx)

        merged = _merge(h, outs, w_merge, b_merge, w_branch, l, 256, 1056, 2)
        xs, h2 = _matmul_residual_modulate(merged, w_out, l, xs, g1, norm2_g[l], sc2, sh2,
                                           n_ctx, 352)
        act = _ffn_up(h2, ffn_w1, ffn_w3, l, 512, 1056, 2)
        xs = _matmul_residual(act, ffn_w2, l, xs, g2, n_ctx, 1024, 352)

    return _final_norm(xs, final_norm_g, n_ctx)[None]
```

```python
import functools
import math

import jax
import jax.numpy as jnp
from jax import lax
from jax.experimental import pallas as pl
from jax.experimental.pallas import tpu as pltpu

F32 = jnp.float32
BF16 = jnp.bfloat16

D_MODEL = 2048
GRID_W = 64
BRANCH_W = 512
N_BRANCH = 4
N_SECTIONS = 10
RET_HEADS = 4
RET_DK = BRANCH_W // RET_HEADS
ROPE_BASE = 10000.0
S5_GROUP = 16
S5_GROUPS = BRANCH_W // S5_GROUP
S5_STATE = 64
CMLP_GROUPS = 4
CMLP_GW = BRANCH_W // CMLP_GROUPS
EPS = 1e-6

CHUNK = 128
SEQ_BLOCK = 256
LANES = 128
SUBLANES = 8
BF16_ROWS = 16
HALO_ROWS = BF16_ROWS
S5_LANE_BLOCKS = BRANCH_W // LANES
S5_GROUPS_PER_BLOCK = LANES // S5_GROUP
S5_BLOCK_STATE = S5_GROUPS_PER_BLOCK * S5_STATE
S5_STATE_W = S5_GROUPS * S5_STATE
MIB = 1024 * 1024
VMEM_BUDGET_MIB = 56

SEC_Q, SEC_K, SEC_V, SEC_G, SEC_U, SEC_CX, SEC_CB, SEC_CC, SEC_GU, SEC_GV = range(N_SECTIONS)


def _cparams(semantics, vmem_mib=VMEM_BUDGET_MIB):
    return pltpu.CompilerParams(dimension_semantics=semantics,
                                vmem_limit_bytes=vmem_mib * MIB)


def _dot(a, b):
    return jnp.dot(a, b, preferred_element_type=F32)


def _row_tile(t, cap=1056):
    for tm in (1056, 1024, 768, 704, 528, 512, 384, 352, 256, 128):
        if tm <= cap and t % tm == 0:
            return tm
    raise ValueError(f"unsupported token count {t}")


def _is_ctx_rows(tile_idx, tm, n_ctx):
    rows = tile_idx * tm + lax.broadcasted_iota(jnp.int32, (tm, 1), 0)
    return rows < n_ctx


def _weight_spec(block, index_map, buffers=1):
    return pl.BlockSpec(block, index_map, pipeline_mode=pl.Buffered(buffers))


def _whole(x):
    nd = x.ndim
    return pl.BlockSpec(x.shape, lambda *_: (0,) * nd)


def _adaln_body(c_ref, w_ref, b_ref, o_ref, *, d, tn):
    reps = tn // LANES

    def kblock(kb, accs):
        rows = pl.ds(pl.multiple_of(kb * SUBLANES, SUBLANES), SUBLANES)
        w = w_ref[rows, :]
        out = []
        for r, acc in enumerate(accs):
            c = c_ref[r, rows, :]
            s = jnp.concatenate([c * jax.nn.sigmoid(c)] * reps, axis=1)
            out.append(acc + s * w)
        return tuple(out)

    zero = jnp.zeros((SUBLANES, tn), F32)
    accs = lax.fori_loop(0, d // SUBLANES, kblock, (zero, zero), unroll=8)
    rows = [jnp.sum(acc, axis=0, keepdims=True) + b_ref[...] for acc in accs]
    o_ref[...] = jnp.concatenate(rows + [jnp.zeros((SUBLANES - len(rows), tn), F32)], axis=0)


def _adaln(c, c_ctx, ada_w, ada_b):
    depth, d, n = ada_w.shape
    tn = 1024
    c_rep = jnp.broadcast_to(jnp.stack([c, c_ctx])[:, :, None], (2, d, LANES))
    return pl.pallas_call(
        functools.partial(_adaln_body, d=d, tn=tn),
        grid=(depth, n // tn),
        in_specs=[
            pl.BlockSpec((2, d, LANES), lambda l, j: (0, 0, 0)),
            pl.BlockSpec((None, d, tn), lambda l, j: (l, 0, j)),
            pl.BlockSpec((None, 1, tn), lambda l, j: (l, 0, j)),
        ],
        out_specs=pl.BlockSpec((None, SUBLANES, tn), lambda l, j: (l, 0, j)),
        out_shape=jax.ShapeDtypeStruct((depth, SUBLANES, n), F32),
        compiler_params=_cparams(("parallel", "parallel")),
        name="adaln",
    )(c_rep, ada_w, ada_b.reshape(depth, 1, n))


def _modulate_body(x_ref, g_ref, sc_ref, sh_ref, o_ref, *, tm, n_ctx):
    base = pl.program_id(0) * tm
    g = g_ref[...]
    gain_lat = g * (1.0 + sc_ref[0:1, :])
    gain_ctx = g * (1.0 + sc_ref[1:2, :])
    shift_lat = sh_ref[0:1, :]
    shift_ctx = sh_ref[1:2, :]

    def slab(r, carry):
        start = pl.multiple_of(r * BF16_ROWS, BF16_ROWS)
        x = x_ref[pl.ds(start, BF16_ROWS), :]
        inv = lax.rsqrt(jnp.mean(x * x, axis=-1, keepdims=True) + EPS)
        is_ctx = (base + start) < n_ctx
        gain = jnp.where(is_ctx, gain_ctx, gain_lat)
        shift = jnp.where(is_ctx, shift_ctx, shift_lat)
        o_ref[pl.ds(start, BF16_ROWS), :] = (x * inv * gain + shift).astype(o_ref.dtype)
        return carry

    lax.fori_loop(0, tm // BF16_ROWS, slab, 0, unroll=3)


def _modulate(x, g, sc, sh, n_ctx):
    t, d = x.shape
    tm = _row_tile(t, cap=528)
    return pl.pallas_call(
        functools.partial(_modulate_body, tm=tm, n_ctx=n_ctx),
        grid=(t // tm,),
        in_specs=[
            pl.BlockSpec((tm, d), lambda i: (i, 0)),
            pl.BlockSpec((1, d), lambda i: (0, 0)),
            pl.BlockSpec((2, d), lambda i: (0, 0)),
            pl.BlockSpec((2, d), lambda i: (0, 0)),
        ],
        out_specs=pl.BlockSpec((tm, d), lambda i: (i, 0)),
        out_shape=jax.ShapeDtypeStruct((t, d), BF16),
        compiler_params=_cparams(("parallel",)),
        name="modulate",
    )(x, g.reshape(1, d), sc, sh)


def _mm_body(x_ref, w_ref, o_ref, wb_ref):
    @pl.when(pl.program_id(1) == 0)
    def _():
        wb_ref[...] = w_ref[...].astype(BF16)

    o_ref[...] = _dot(x_ref[...], wb_ref[...]).astype(o_ref.dtype)


def _matmul(x, w_stack, layer, tn, tm_cap, w_buffers, out_dtype):
    t, k = x.shape
    n = w_stack.shape[-1]
    tm = _row_tile(t, tm_cap)
    return pl.pallas_call(
        _mm_body,
        grid=(n // tn, t // tm),
        in_specs=[
            pl.BlockSpec((tm, k), lambda j, i: (i, 0)),
            _weight_spec((None, k, tn), lambda j, i: (layer, 0, j), w_buffers),
        ],
        out_specs=pl.BlockSpec((tm, tn), lambda j, i: (i, j)),
        out_shape=jax.ShapeDtypeStruct((t, n), out_dtype),
        scratch_shapes=[pltpu.VMEM((k, tn), BF16)],
        compiler_params=_cparams(("arbitrary", "arbitrary")),
        name="matmul",
    )(x, w_stack)


def _mm_residual_body(a_ref, w_ref, x_ref, g_ref, o_ref, wb_ref, *, tm, n_ctx):
    @pl.when(pl.program_id(1) == 0)
    def _():
        wb_ref[...] = w_ref[...].astype(BF16)

    y = _dot(a_ref[...], wb_ref[...])
    is_ctx = _is_ctx_rows(pl.program_id(1), tm, n_ctx)
    gate = jnp.where(is_ctx, g_ref[1:2, :], g_ref[0:1, :])
    o_ref[...] = x_ref[...] + gate * y


def _matmul_residual(a, w_stack, layer, x, gate, n_ctx, tn, tm_cap):
    t, k = a.shape
    n = w_stack.shape[-1]
    tm = _row_tile(t, tm_cap)
    return pl.pallas_call(
        functools.partial(_mm_residual_body, tm=tm, n_ctx=n_ctx),
        grid=(n // tn, t // tm),
        in_specs=[
            pl.BlockSpec((tm, k), lambda j, i: (i, 0)),
            _weight_spec((None, k, tn), lambda j, i: (layer, 0, j)),
            pl.BlockSpec((tm, tn), lambda j, i: (i, j)),
            pl.BlockSpec((2, tn), lambda j, i: (0, j)),
        ],
        out_specs=pl.BlockSpec((tm, tn), lambda j, i: (i, j)),
        out_shape=jax.ShapeDtypeStruct((t, n), F32),
        scratch_shapes=[pltpu.VMEM((k, tn), BF16)],
        compiler_params=_cparams(("arbitrary", "arbitrary")),
        name="matmul_residual",
    )(a, w_stack, x, gate)


def _mm_residual_modulate_body(a_ref, w_ref, x_ref, gate_ref, g_ref, sc_ref, sh_ref,
                               xo_ref, ho_ref, wb_ref, y0_ref, y1_ref, *, tm, n_ctx):
    i = pl.program_id(0)

    @pl.when(i == 0)
    def _():
        wb_ref[...] = w_ref[...].astype(BF16)
        y1_ref[...] = jnp.zeros_like(y1_ref)

    base = jnp.maximum(i - 1, 0) * tm
    g = g_ref[...]
    gain_lat = g * (1.0 + sc_ref[0:1, :])
    gain_ctx = g * (1.0 + sc_ref[1:2, :])

    def run(y_write, y_read):
        y_write[...] = _dot(a_ref[...], wb_ref[...])
        for r in range(tm // BF16_ROWS):
            rows = slice(r * BF16_ROWS, (r + 1) * BF16_ROWS)
            is_ctx = (base + r * BF16_ROWS) < n_ctx
            gate = jnp.where(is_ctx, gate_ref[1:2, :], gate_ref[0:1, :])
            x = x_ref[rows, :] + gate * y_read[rows, :]
            xo_ref[rows, :] = x
            inv = lax.rsqrt(jnp.mean(x * x, axis=-1, keepdims=True) + EPS)
            gain = jnp.where(is_ctx, gain_ctx, gain_lat)
            shift = jnp.where(is_ctx, sh_ref[1:2, :], sh_ref[0:1, :])
            ho_ref[rows, :] = (x * inv * gain + shift).astype(ho_ref.dtype)

    @pl.when(lax.rem(i, 2) == 0)
    def _():
        run(y0_ref, y1_ref)

    @pl.when(lax.rem(i, 2) == 1)
    def _():
        run(y1_ref, y0_ref)


def _matmul_residual_modulate(a, w_stack, layer, x, gate, g, sc, sh, n_ctx, tm_cap):
    t, k = a.shape
    n = w_stack.shape[-1]
    tm = _row_tile(t, tm_cap)
    nt = t // tm
    lagged = pl.BlockSpec((tm, n), lambda i: (jnp.maximum(i - 1, 0), 0))
    vec2 = pl.BlockSpec((2, n), lambda i: (0, 0))
    product = pltpu.VMEM((tm, n), F32)
    return pl.pallas_call(
        functools.partial(_mm_residual_modulate_body, tm=tm, n_ctx=n_ctx),
        grid=(nt + 1,),
        in_specs=[
            pl.BlockSpec((tm, k), lambda i: (jnp.minimum(i, nt - 1), 0)),
            _weight_spec((None, k, n), lambda i: (layer, 0, 0)),
            lagged, vec2,
            pl.BlockSpec((1, n), lambda i: (0, 0)), vec2, vec2,
        ],
        out_specs=[lagged, lagged],
        out_shape=[jax.ShapeDtypeStruct((t, n), F32), jax.ShapeDtypeStruct((t, n), BF16)],
        scratch_shapes=[pltpu.VMEM((k, n), BF16), product, product],
        compiler_params=_cparams(("arbitrary",)),
        name="matmul_residual_modulate",
    )(a, w_stack, x, gate, g.reshape(1, n), sc, sh)


def _ffn_up_body(h_ref, w1_ref, w3_ref, o_ref, w1b_ref, w3b_ref):
    @pl.when(pl.program_id(1) == 0)
    def _():
        w1b_ref[...] = w1_ref[...].astype(BF16)
        w3b_ref[...] = w3_ref[...].astype(BF16)

    h = h_ref[...]
    a = _dot(h, w1b_ref[...])
    b = _dot(h, w3b_ref[...])
    o_ref[...] = (a * jax.nn.sigmoid(a) * b).astype(o_ref.dtype)


def _ffn_up(h, w1_stack, w3_stack, layer, tn, tm_cap, w_buffers):
    t, k = h.shape
    n = w1_stack.shape[-1]
    tm = _row_tile(t, tm_cap)
    wspec = _weight_spec((None, k, tn), lambda j, i: (layer, 0, j), w_buffers)
    return pl.pallas_call(
        _ffn_up_body,
        grid=(n // tn, t // tm),
        in_specs=[pl.BlockSpec((tm, k), lambda j, i: (i, 0)), wspec, wspec],
        out_specs=pl.BlockSpec((tm, tn), lambda j, i: (i, j)),
        out_shape=jax.ShapeDtypeStruct((t, n), BF16),
        scratch_shapes=[pltpu.VMEM((k, tn), BF16), pltpu.VMEM((k, tn), BF16)],
        compiler_params=_cparams(("arbitrary", "arbitrary")),
        name="ffn_up",
    )(h, w1_stack, w3_stack)


def _merge_body(*refs):
    nb = N_BRANCH
    h_ref = refs[0]
    o_refs = refs[1:1 + nb]
    wm_refs = refs[1 + nb:1 + 2 * nb]
    bm_refs = refs[1 + 2 * nb:1 + 3 * nb]
    wb_refs = refs[1 + 3 * nb:1 + 4 * nb]
    out_ref, wmb_ref, wbb_ref = refs[1 + 4 * nb:]

    @pl.when(pl.program_id(1) == 0)
    def _():
        for k in range(nb):
            wmb_ref[k] = wm_refs[k][...].astype(BF16)
            wbb_ref[k] = wb_refs[k][...].astype(BF16)

    h = h_ref[...]
    acc = None
    for k in range(nb):
        gate = jax.nn.sigmoid(_dot(h, wmb_ref[k]) + bm_refs[k][...])
        term = gate * _dot(o_refs[k][...], wbb_ref[k])
        acc = term if acc is None else acc + term
    out_ref[...] = acc.astype(out_ref.dtype)


def _merge(h, outs, w_merge, b_merge, w_branch, layer, tn, tm_cap, w_buffers):
    t, d = h.shape
    bw = outs[0].shape[1]
    depth = w_merge.shape[0]
    tm = _row_tile(t, tm_cap)
    nt = d // tn
    b_merge3 = b_merge.reshape(depth, 1, N_BRANCH * d)
    in_specs = [pl.BlockSpec((tm, d), lambda j, i: (i, 0))]
    in_specs += [pl.BlockSpec((tm, bw), lambda j, i: (i, 0)) for _ in range(N_BRANCH)]
    in_specs += [_weight_spec((None, d, tn), lambda j, i, k=k: (layer, 0, k * nt + j), w_buffers)
                 for k in range(N_BRANCH)]
    in_specs += [pl.BlockSpec((None, 1, tn), lambda j, i, k=k: (layer, 0, k * nt + j))
                 for k in range(N_BRANCH)]
    in_specs += [_weight_spec((None, None, bw, tn), lambda j, i, k=k: (layer, k, 0, j), w_buffers)
                 for k in range(N_BRANCH)]
    return pl.pallas_call(
        _merge_body,
        grid=(nt, t // tm),
        in_specs=in_specs,
        out_specs=pl.BlockSpec((tm, tn), lambda j, i: (i, j)),
        out_shape=jax.ShapeDtypeStruct((t, d), BF16),
        scratch_shapes=[pltpu.VMEM((N_BRANCH, d, tn), BF16),
                        pltpu.VMEM((N_BRANCH, bw, tn), BF16)],
        compiler_params=_cparams(("arbitrary", "arbitrary")),
        name="merge",
    )(h, *outs, *([w_merge] * N_BRANCH), *([b_merge3] * N_BRANCH), *([w_branch] * N_BRANCH))


def _block_order(step, n_blocks, n_ctx_blocks, reverse):
    if not reverse:
        return step
    return jnp.where(step < n_ctx_blocks, n_ctx_blocks - 1 - step,
                     n_blocks + n_ctx_blocks - 1 - step)


def _retention_body(qf_ref, kf_ref, vf_ref, cosf_ref, sinf_ref,
                    qb_ref, kb_ref, vb_ref, cosb_ref, sinb_ref, tab_ref,
                    of_ref, ob_ref, s_ref):
    @pl.when(pl.program_id(0) == 0)
    def _():
        s_ref[...] = jnp.zeros_like(s_ref)

    scale = RET_DK ** -0.5
    chunks = SEQ_BLOCK // CHUNK
    dirs = ((qf_ref, kf_ref, vf_ref, cosf_ref, sinf_ref, of_ref, range(chunks)),
            (qb_ref, kb_ref, vb_ref, cosb_ref, sinb_ref, ob_ref, range(chunks - 1, -1, -1)))
    for d, (q_ref, k_ref, v_ref, cos_ref, sin_ref, o_ref, order) in enumerate(dirs):
        for h in range(RET_HEADS):
            sl = slice(h * RET_DK, (h + 1) * RET_DK)
            dmat = tab_ref[d, 0, h]
            q_decay = tab_ref[d, 1, h]
            k_decay = tab_ref[d, 2, h]
            g_chunk = tab_ref[d, 3, h]
            state = s_ref[d, h]
            for ci in order:
                rows = slice(ci * CHUNK, (ci + 1) * CHUNK)
                cos = cos_ref[rows, :]
                sin = sin_ref[rows, :]
                q = q_ref[rows, sl].astype(F32)
                k = k_ref[rows, sl].astype(F32)
                v = v_ref[rows, sl].astype(BF16)
                qr = (q * cos + pltpu.roll(q, RET_DK // 2, 1) * sin) * scale
                kr = k * cos + pltpu.roll(k, RET_DK // 2, 1) * sin
                scores = lax.dot_general(qr.astype(BF16), kr.astype(BF16),
                                         (((1,), (1,)), ((), ())),
                                         preferred_element_type=F32) * dmat
                intra = _dot(scores.astype(BF16), v)
                cross = _dot((qr * q_decay).astype(BF16), state.astype(BF16))
                kv = _dot((kr * k_decay).T.astype(BF16), v)
                o_ref[rows, sl] = intra + cross
                state = g_chunk * state + kv
            s_ref[d, h] = state


def _retention_tables(log_gamma):
    pos = jnp.arange(CHUNK, dtype=F32)
    i = pos[:, None]
    j = pos[None, :]
    full = log_gamma.shape[:1] + (RET_HEADS, CHUNK, RET_DK)

    def one_dir(lg, diff, mask, q_pow, k_pow):
        dmat = jnp.where(mask, jnp.exp(lg[:, :, None, None] * jnp.maximum(diff, 0.0)), 0.0)
        q_decay = jnp.broadcast_to(jnp.exp(lg[:, :, None] * q_pow)[..., None], full)
        k_decay = jnp.broadcast_to(jnp.exp(lg[:, :, None] * k_pow)[..., None], full)
        g_chunk = jnp.broadcast_to(jnp.exp(lg * CHUNK)[:, :, None, None], full)
        return jnp.stack([dmat, q_decay, k_decay, g_chunk], axis=1)

    fwd = one_dir(log_gamma[:, 0], i - j, (i - j) >= 0, pos + 1.0, CHUNK - 1.0 - pos)
    bwd = one_dir(log_gamma[:, 1], j - i, (j - i) > 0, CHUNK - pos, pos)
    return jnp.stack([fwd, bwd], axis=1)


def _retention(proj, cos2, sin2, tables, layer, n_ctx):
    t = proj.shape[0]
    nb = t // SEQ_BLOCK
    nbc = n_ctx // SEQ_BLOCK

    def specs(reverse):
        def blk(st):
            return _block_order(st, nb, nbc, reverse)
        sec = [pl.BlockSpec((SEQ_BLOCK, BRANCH_W), lambda st, s=s: (blk(st), s))
               for s in (SEC_Q, SEC_K, SEC_V)]
        row = pl.BlockSpec((SEQ_BLOCK, RET_DK), lambda st: (blk(st), 0))
        out = pl.BlockSpec((SEQ_BLOCK, BRANCH_W), lambda st: (blk(st), 0))
        return sec + [row, row], out

    in_f, out_f = specs(False)
    in_b, out_b = specs(True)
    tab_spec = pl.BlockSpec((None, 2, 4, RET_HEADS, CHUNK, RET_DK),
                            lambda st: (layer, 0, 0, 0, 0, 0))
    shape = jax.ShapeDtypeStruct((t, BRANCH_W), F32)
    return pl.pallas_call(
        _retention_body,
        grid=(nb,),
        in_specs=in_f + in_b + [tab_spec],
        out_specs=[out_f, out_b],
        out_shape=[shape, shape],
        scratch_shapes=[pltpu.VMEM((2, RET_HEADS, RET_DK, RET_DK), F32)],
        compiler_params=_cparams(("arbitrary",)),
        name="retention",
    )(proj, proj, proj, cos2, sin2, proj, proj, proj, cos2, sin2, tables)


def _s5_body(uf_ref, ub_ref, wb_ref, a_ref, wc_ref, d_ref, yf_ref, yb_ref,
             x_ref, bu_ref, st_ref):
    @pl.when(pl.program_id(0) == 0)
    def _():
        x_ref[...] = jnp.zeros_like(x_ref)

    for d, u_ref in enumerate((uf_ref, ub_ref)):
        ub = u_ref[...].astype(BF16)
        for r in range(S5_LANE_BLOCKS):
            bu = _dot(ub[:, r * LANES:(r + 1) * LANES], wb_ref[d, r])
            cols = slice(r * S5_BLOCK_STATE, (r + 1) * S5_BLOCK_STATE)
            bu_ref[2 * d, :, cols] = bu[:, :S5_BLOCK_STATE]
            bu_ref[2 * d + 1, :, cols] = bu[:, S5_BLOCK_STATE:]

    af_re, af_im = a_ref[0, 0:1, :], a_ref[0, 1:2, :]
    ab_re, ab_im = a_ref[1, 0:1, :], a_ref[1, 1:2, :]

    def step(i, carry):
        f_re, f_im, b_re, b_im = carry
        tf = pl.ds(i, 1)
        tb = pl.ds(SEQ_BLOCK - 1 - i, 1)
        nf_re = af_re * f_re - af_im * f_im + bu_ref[0, tf, :]
        nf_im = af_re * f_im + af_im * f_re + bu_ref[1, tf, :]
        nb_re = ab_re * b_re - ab_im * b_im + bu_ref[2, tb, :]
        nb_im = ab_re * b_im + ab_im * b_re + bu_ref[3, tb, :]
        st_ref[0, tf, :] = nf_re
        st_ref[1, tf, :] = nf_im
        st_ref[2, tb, :] = nb_re
        st_ref[3, tb, :] = nb_im
        return nf_re, nf_im, nb_re, nb_im

    carry = lax.fori_loop(0, SEQ_BLOCK, step,
                          (x_ref[0:1, :], x_ref[1:2, :], x_ref[2:3, :], x_ref[3:4, :]),
                          unroll=8)
    for n, val in enumerate(carry):
        x_ref[n:n + 1, :] = val

    def readout(d):
        ys = []
        for r in range(S5_LANE_BLOCKS):
            cols = slice(r * S5_BLOCK_STATE, (r + 1) * S5_BLOCK_STATE)
            st = jnp.concatenate([st_ref[2 * d, :, cols], st_ref[2 * d + 1, :, cols]],
                                 axis=1).astype(BF16)
            ys.append(_dot(st, wc_ref[r]))
        return jnp.concatenate(ys, axis=1)

    yf_ref[...] = readout(0) + d_ref[...] * uf_ref[...].astype(F32)
    yb_ref[...] = readout(1)


def _s5_prepare(a_re, a_im, b_re, b_im, log_dt, c_re, c_im):
    depth = a_re.shape[0]
    a_re = a_re[:, None]
    a_im = a_im[:, None]
    dt = jnp.exp(log_dt)[..., None]
    mag = jnp.exp(dt * a_re)
    ang = dt * a_im
    ab_re, ab_im = mag * jnp.cos(ang), mag * jnp.sin(ang)
    nr, ni = ab_re - 1.0, ab_im
    den = a_re * a_re + a_im * a_im
    f_re = (nr * a_re + ni * a_im) / den
    f_im = (ni * a_re - nr * a_im) / den
    bb_re = f_re[..., None] * b_re - f_im[..., None] * b_im
    bb_im = f_re[..., None] * b_im + f_im[..., None] * b_re
    eye = jnp.eye(S5_GROUPS_PER_BLOCK, dtype=F32)

    def in_map(bb):
        bb6 = bb.reshape(depth, 2, S5_LANE_BLOCKS, S5_GROUPS_PER_BLOCK, S5_STATE, S5_GROUP)
        bb6 = jnp.swapaxes(bb6, -1, -2)
        w = bb6[..., None, :] * eye[:, None, :, None]
        return w.reshape(depth, 2, S5_LANE_BLOCKS, LANES, S5_BLOCK_STATE)

    def out_map(c):
        c5 = c.reshape(depth, S5_LANE_BLOCKS, S5_GROUPS_PER_BLOCK, S5_GROUP, S5_STATE)
        c5 = jnp.swapaxes(c5, -1, -2)
        w = c5[..., None, :] * eye[:, None, :, None]
        return w.reshape(depth, S5_LANE_BLOCKS, S5_BLOCK_STATE, LANES)

    wb = jnp.concatenate([in_map(bb_re), in_map(bb_im)], axis=-1).astype(BF16)
    a = jnp.stack([ab_re.reshape(depth, 2, S5_STATE_W), ab_im.reshape(depth, 2, S5_STATE_W)],
                  axis=2)
    wc = jnp.concatenate([out_map(c_re), out_map(-c_im)], axis=-2).astype(BF16)
    return wb, a, wc


def _s5(proj, wb, a, wc, s5_d, layer, n_ctx):
    t = proj.shape[0]
    nb = t // SEQ_BLOCK
    nbc = n_ctx // SEQ_BLOCK
    depth = s5_d.shape[0]

    def io(reverse, sec):
        return pl.BlockSpec((SEQ_BLOCK, BRANCH_W),
                            lambda st: (_block_order(st, nb, nbc, reverse), sec))

    shape = jax.ShapeDtypeStruct((t, BRANCH_W), F32)
    planes = pltpu.VMEM((4, SEQ_BLOCK, S5_STATE_W), F32)
    return pl.pallas_call(
        _s5_body,
        grid=(nb,),
        in_specs=[io(False, SEC_U), io(True, SEC_U),
                  pl.BlockSpec((None,) + wb.shape[1:], lambda st: (layer, 0, 0, 0, 0)),
                  pl.BlockSpec((None,) + a.shape[1:], lambda st: (layer, 0, 0, 0)),
                  pl.BlockSpec((None,) + wc.shape[1:], lambda st: (layer, 0, 0, 0)),
                  pl.BlockSpec((None, 1, BRANCH_W), lambda st: (layer, 0, 0))],
        out_specs=[io(False, 0), io(True, 0)],
        out_shape=[shape, shape],
        scratch_shapes=[pltpu.VMEM((4, S5_STATE_W), F32), planes, planes],
        compiler_params=_cparams(("arbitrary",)),
        name="s5",
    )(proj, proj, wb, a, wc, s5_d.reshape(depth, 1, BRANCH_W))


def _local_body(of_ref, ob_ref, g_ref, yf_ref, yb_ref, wglu_ref,
                cx_ref, cb_ref, cc_ref, cxp_ref, ccp_ref, cxn_ref, ccn_ref, cw_ref,
                gu_ref, gv_ref, lng_ref, lnb_ref, ws_ref, bs_ref,
                oret_ref, os5_ref, oc_ref, og_ref, *, n_blocks, n_ctx_blocks):
    for h in range(RET_HEADS):
        sl = slice(h * RET_DK, (h + 1) * RET_DK)
        o = of_ref[:, sl] + ob_ref[:, sl]
        mu = jnp.mean(o, axis=-1, keepdims=True)
        var = jnp.mean(jnp.square(o - mu), axis=-1, keepdims=True)
        o = (o - mu) * lax.rsqrt(var + EPS)
        g = g_ref[:, sl].astype(F32)
        oret_ref[:, sl] = (o * (g * jax.nn.sigmoid(g))).astype(oret_ref.dtype)

    y = jax.nn.gelu(yf_ref[...] + yb_ref[...])
    os5_ref[...] = (y * jax.nn.sigmoid(_dot(y.astype(BF16), wglu_ref[...]))).astype(os5_ref.dtype)

    c = pl.program_id(0)
    seq_start = jnp.logical_or(c == 0, c == n_ctx_blocks)
    seq_end = jnp.logical_or(c == n_ctx_blocks - 1, c == n_blocks - 1)
    z = cc_ref[...].astype(F32) * cx_ref[...].astype(F32)
    z_prev = (ccp_ref[...].astype(F32) * cxp_ref[...].astype(F32))[HALO_ROWS - 1:HALO_ROWS, :]
    z_next = (ccn_ref[...].astype(F32) * cxn_ref[...].astype(F32))[0:1, :]
    z_prev = jnp.where(seq_start, 0.0, z_prev)
    z_next = jnp.where(seq_end, 0.0, z_next)
    rows = lax.broadcasted_iota(jnp.int32, (SEQ_BLOCK, 1), 0)
    z_up = jnp.where(rows == 0, z_prev, pltpu.roll(z, 1, 0))
    z_dn = jnp.where(rows == SEQ_BLOCK - 1, z_next, pltpu.roll(z, SEQ_BLOCK - 1, 0))
    yc = cw_ref[0:1, :] * z_up + cw_ref[1:2, :] * z + cw_ref[2:3, :] * z_dn
    oc_ref[...] = (cb_ref[...].astype(F32) * yc).astype(oc_ref.dtype)

    u = jax.nn.gelu(gu_ref[...].astype(F32))
    v = jax.nn.gelu(gv_ref[...].astype(F32))
    mu = jnp.mean(v, axis=-1, keepdims=True)
    var = jnp.mean(jnp.square(v - mu), axis=-1, keepdims=True)
    v = ((v - mu) * lax.rsqrt(var + EPS) * lng_ref[...] + lnb_ref[...]).astype(BF16)
    for ci in range(SEQ_BLOCK // CHUNK):
        rws = slice(ci * CHUNK, (ci + 1) * CHUNK)
        for g in range(CMLP_GROUPS):
            sl = slice(g * CMLP_GW, (g + 1) * CMLP_GW)
            mixed = _dot(ws_ref[g], v[rws, sl]) + bs_ref[g]
            og_ref[rws, sl] = (u[rws, sl] * mixed).astype(og_ref.dtype)


def _local_mixers(proj, o_f, o_b, y_f, y_b, wglu_b, conv_w, ln_g, ln_b, ws_b, bs_full,
                  layer, n_ctx):
    t = proj.shape[0]
    nb = t // SEQ_BLOCK
    nbc = n_ctx // SEQ_BLOCK
    per = SEQ_BLOCK // HALO_ROWS
    last = t // HALO_ROWS - 1
    depth = conv_w.shape[0]

    def sec(s):
        return pl.BlockSpec((SEQ_BLOCK, BRANCH_W), lambda c: (c, s))

    def prev(s):
        return pl.BlockSpec((HALO_ROWS, BRANCH_W), lambda c: (jnp.maximum(c * per - 1, 0), s))

    def nxt(s):
        return pl.BlockSpec((HALO_ROWS, BRANCH_W), lambda c: (jnp.minimum((c + 1) * per, last), s))

    def layer_spec(x):
        nd = x.ndim - 1
        return pl.BlockSpec((None,) + x.shape[1:], lambda c: (layer,) + (0,) * nd)

    ln_g3 = ln_g.reshape(depth, 1, BRANCH_W)
    ln_b3 = ln_b.reshape(depth, 1, BRANCH_W)
    blk = sec(0)
    out_shape = jax.ShapeDtypeStruct((t, BRANCH_W), BF16)
    return pl.pallas_call(
        functools.partial(_local_body, n_blocks=nb, n_ctx_blocks=nbc),
        grid=(nb,),
        in_specs=[blk, blk, sec(SEC_G), blk, blk, layer_spec(wglu_b),
                  sec(SEC_CX), sec(SEC_CB), sec(SEC_CC),
                  prev(SEC_CX), prev(SEC_CC), nxt(SEC_CX), nxt(SEC_CC), layer_spec(conv_w),
                  sec(SEC_GU), sec(SEC_GV), layer_spec(ln_g3), layer_spec(ln_b3),
                  layer_spec(ws_b), layer_spec(bs_full)],
        out_specs=[blk, blk, blk, blk],
        out_shape=[out_shape] * 4,
        compiler_params=_cparams(("parallel",)),
        name="local_mixers",
    )(o_f, o_b, proj, y_f, y_b, wglu_b,
      proj, proj, proj, proj, proj, proj, proj, conv_w,
      proj, proj, ln_g3, ln_b3, ws_b, bs_full)


def _final_norm_body(x_ref, g_ref, o_ref, *, tm):
    g = g_ref[...]

    def slab(r, carry):
        start = pl.multiple_of(r * BF16_ROWS, BF16_ROWS)
        x = x_ref[pl.ds(start, BF16_ROWS), :]
        inv = lax.rsqrt(jnp.mean(x * x, axis=-1, keepdims=True) + EPS)
        o_ref[pl.ds(start, BF16_ROWS), :] = x * inv * g
        return carry

    lax.fori_loop(0, tm // BF16_ROWS, slab, 0, unroll=2)


def _final_norm(x, g, n_ctx):
    t, d = x.shape
    tm = math.gcd(n_ctx, 512)
    skip = n_ctx // tm
    n_lat = t - n_ctx
    return pl.pallas_call(
        functools.partial(_final_norm_body, tm=tm),
        grid=(n_lat // tm,),
        in_specs=[pl.BlockSpec((tm, d), lambda i: (i + skip, 0)),
                  pl.BlockSpec((1, d), lambda i: (0, 0))],
        out_specs=pl.BlockSpec((tm, d), lambda i: (i, 0)),
        out_shape=jax.ShapeDtypeStruct((n_lat, d), F32),
        compiler_params=_cparams(("parallel",)),
        name="final_norm",
    )(x, g.reshape(1, d))


def _rope_tables(n_lat, n_ctx):
    rows = n_lat // GRID_W
    row = jnp.broadcast_to(jnp.arange(rows)[:, None], (rows, GRID_W)).reshape(-1).astype(F32)
    col = jnp.broadcast_to(jnp.arange(GRID_W)[None, :], (rows, GRID_W)).reshape(-1).astype(F32)
    nf = RET_DK // 4
    freqs = ROPE_BASE ** (-jnp.arange(nf, dtype=F32) / nf)
    ang = jnp.concatenate([row[:, None] * freqs, col[:, None] * freqs], axis=-1)
    cos, sin = jnp.cos(ang), jnp.sin(ang)
    cos2 = jnp.concatenate([cos, cos], axis=-1)
    sin2 = jnp.concatenate([-sin, sin], axis=-1)
    cos2 = jnp.concatenate([jnp.ones((n_ctx, RET_DK), F32), cos2], axis=0)
    sin2 = jnp.concatenate([jnp.zeros((n_ctx, RET_DK), F32), sin2], axis=0)
    return cos2, sin2


def kernel(x, c, ctx, c_ctx, ada_w, ada_b, norm1_g, norm2_g, w_in, ret_decay_logit, s5_a_re, s5_a_im, s5_b_re, s5_b_im, s5_c_re, s5_c_im, s5_d, s5_log_dt, s5_w_glu, conv_w, cmlp_ln_g, cmlp_ln_b, cmlp_ws, cmlp_bs, w_branch, w_merge, b_merge, w_out, ffn_w1, ffn_w3, ffn_w2, final_norm_g):
    batch, n_lat, d = x.shape
    assert batch == 1 and c.shape[0] == 1 and ctx.shape[0] == 1
    n_ctx = ctx.shape[1]
    assert n_ctx % SEQ_BLOCK == 0 and n_lat % SEQ_BLOCK == 0 and n_lat % GRID_W == 0
    depth = ada_w.shape[0]

    cos2, sin2 = _rope_tables(n_lat, n_ctx)
    ret_tables = _retention_tables(jax.nn.log_sigmoid(ret_decay_logit.astype(F32)))
    s5_wb, s5_a, s5_wc = _s5_prepare(s5_a_re, s5_a_im, s5_b_re, s5_b_im, s5_log_dt,
                                     s5_c_re, s5_c_im)
    wglu_b = s5_w_glu.astype(BF16)
    ws_b = cmlp_ws.astype(BF16)
    bs_full = jnp.broadcast_to(cmlp_bs[..., None], cmlp_bs.shape + (CMLP_GW,))

    mod_all = _adaln(c[0], c_ctx, ada_w, ada_b)
    mod_all = mod_all[:, 0:2].reshape(depth, 2, 6, d)
    xs = jnp.concatenate([ctx[0], x[0]], axis=0)

    for l in range(depth):
        sh1, sc1, g1, sh2, sc2, g2 = [mod_all[l, :, i] for i in range(6)]
        h = _modulate(xs, norm1_g[l], sc1, sh1, n_ctx)
        proj = _matmul(h, w_in, l, 1280, 528, 2, BF16)

        o_f, o_b = _retention(proj, cos2, sin2, ret_tables, l, n_ctx)
        y_f, y_b = _s5(proj, s5_wb, s5_a, s5_wc, s5_d, l, n_ctx)
        outs = _local_mixers(proj, o_f, o_b, y_f, y_b, wglu_b, conv_w, cmlp_ln_g, cmlp_ln_b,
                             ws_b, bs_full, l, n_ctx)

        merged = _merge(h, outs, w_merge, b_merge, w_branch, l, 256, 1056, 2)
        xs, h2 = _matmul_residual_modulate(merged, w_out, l, xs, g1, norm2_g[l], sc2, sh2,
                                           n_ctx, 352)
        act = _ffn_up(h2, ffn_w1, ffn_w3, l, 512, 1056, 2)
        xs = _matmul_residual(act, ffn_w2, l, xs, g2, n_ctx, 1024, 352)

    return _final_norm(xs, final_norm_g, n_ctx)[None]
```

```python
import functools
import math

import jax
import jax.numpy as jnp
from jax import lax
from jax.experimental import pallas as pl
from jax.experimental.pallas import tpu as pltpu

F32 = jnp.float32
BF16 = jnp.bfloat16

D_MODEL = 2048
GRID_W = 64
BRANCH_W = 512
N_BRANCH = 4
N_SECTIONS = 10
RET_HEADS = 4
RET_DK = BRANCH_W // RET_HEADS
ROPE_BASE = 10000.0
S5_GROUP = 16
S5_GROUPS = BRANCH_W // S5_GROUP
S5_STATE = 64
CMLP_GROUPS = 4
CMLP_GW = BRANCH_W // CMLP_GROUPS
EPS = 1e-6

CHUNK = 128
SEQ_BLOCK = 256
LANES = 128
SUBLANES = 8
BF16_ROWS = 16
HALO_ROWS = BF16_ROWS
S5_LANE_BLOCKS = BRANCH_W // LANES
S5_GROUPS_PER_BLOCK = LANES // S5_GROUP
S5_BLOCK_STATE = S5_GROUPS_PER_BLOCK * S5_STATE
S5_STATE_W = S5_GROUPS * S5_STATE
MIB = 1024 * 1024
VMEM_BUDGET_MIB = 56

SEC_Q, SEC_K, SEC_V, SEC_G, SEC_U, SEC_CX, SEC_CB, SEC_CC, SEC_GU, SEC_GV = range(N_SECTIONS)


def _cparams(semantics, vmem_mib=VMEM_BUDGET_MIB):
    return pltpu.CompilerParams(dimension_semantics=semantics,
                                vmem_limit_bytes=vmem_mib * MIB)


def _dot(a, b):
    return jnp.dot(a, b, preferred_element_type=F32)


def _row_tile(t, cap=1056):
    for tm in (1056, 1024, 768, 704, 528, 512, 384, 352, 256, 128):
        if tm <= cap and t % tm == 0:
            return tm
    raise ValueError(f"unsupported token count {t}")


def _is_ctx_rows(tile_idx, tm, n_ctx):
    rows = tile_idx * tm + lax.broadcasted_iota(jnp.int32, (tm, 1), 0)
    return rows < n_ctx


def _weight_spec(block, index_map, buffers=1):
    return pl.BlockSpec(block, index_map, pipeline_mode=pl.Buffered(buffers))


def _whole(x):
    nd = x.ndim
    return pl.BlockSpec(x.shape, lambda *_: (0,) * nd)


def _adaln_body(c_ref, w_ref, b_ref, o_ref, *, d, tn):
    reps = tn // LANES

    def kblock(kb, accs):
        rows = pl.ds(pl.multiple_of(kb * SUBLANES, SUBLANES), SUBLANES)
        w = w_ref[rows, :]
        out = []
        for r, acc in enumerate(accs):
            c = c_ref[r, rows, :]
            s = jnp.concatenate([c * jax.nn.sigmoid(c)] * reps, axis=1)
            out.append(acc + s * w)
        return tuple(out)

    zero = jnp.zeros((SUBLANES, tn), F32)
    accs = lax.fori_loop(0, d // SUBLANES, kblock, (zero, zero), unroll=8)
    rows = [jnp.sum(acc, axis=0, keepdims=True) + b_ref[...] for acc in accs]
    o_ref[...] = jnp.concatenate(rows + [jnp.zeros((SUBLANES - len(rows), tn), F32)], axis=0)


def _adaln(c, c_ctx, ada_w, ada_b):
    depth, d, n = ada_w.shape
    tn = 1024
    c_rep = jnp.broadcast_to(jnp.stack([c, c_ctx])[:, :, None], (2, d, LANES))
    return pl.pallas_call(
        functools.partial(_adaln_body, d=d, tn=tn),
        grid=(depth, n // tn),
        in_specs=[
            pl.BlockSpec((2, d, LANES), lambda l, j: (0, 0, 0)),
            pl.BlockSpec((None, d, tn), lambda l, j: (l, 0, j)),
            pl.BlockSpec((None, 1, tn), lambda l, j: (l, 0, j)),
        ],
        out_specs=pl.BlockSpec((None, SUBLANES, tn), lambda l, j: (l, 0, j)),
        out_shape=jax.ShapeDtypeStruct((depth, SUBLANES, n), F32),
        compiler_params=_cparams(("parallel", "parallel")),
        name="adaln",
    )(c_rep, ada_w, ada_b.reshape(depth, 1, n))


def _modulate_body(x_ref, g_ref, sc_ref, sh_ref, o_ref, *, tm, n_ctx):
    base = pl.program_id(0) * tm
    g = g_ref[...]
    gain_lat = g * (1.0 + sc_ref[0:1, :])
    gain_ctx = g * (1.0 + sc_ref[1:2, :])
    shift_lat = sh_ref[0:1, :]
    shift_ctx = sh_ref[1:2, :]

    def slab(r, carry):
        start = pl.multiple_of(r * BF16_ROWS, BF16_ROWS)
        x = x_ref[pl.ds(start, BF16_ROWS), :]
        inv = lax.rsqrt(jnp.mean(x * x, axis=-1, keepdims=True) + EPS)
        is_ctx = (base + start) < n_ctx
        gain = jnp.where(is_ctx, gain_ctx, gain_lat)
        shift = jnp.where(is_ctx, shift_ctx, shift_lat)
        o_ref[pl.ds(start, BF16_ROWS), :] = (x * inv * gain + shift).astype(o_ref.dtype)
        return carry

    lax.fori_loop(0, tm // BF16_ROWS, slab, 0, unroll=3)


def _modulate(x, g, sc, sh, n_ctx):
    t, d = x.shape
    tm = _row_tile(t, cap=528)
    return pl.pallas_call(
        functools.partial(_modulate_body, tm=tm, n_ctx=n_ctx),
        grid=(t // tm,),
        in_specs=[
            pl.BlockSpec((tm, d), lambda i: (i, 0)),
            pl.BlockSpec((1, d), lambda i: (0, 0)),
            pl.BlockSpec((2, d), lambda i: (0, 0)),
            pl.BlockSpec((2, d), lambda i: (0, 0)),
        ],
        out_specs=pl.BlockSpec((tm, d), lambda i: (i, 0)),
        out_shape=jax.ShapeDtypeStruct((t, d), BF16),
        compiler_params=_cparams(("parallel",)),
        name="modulate",
    )(x, g.reshape(1, d), sc, sh)


def _modulate_concat_body(ctx_ref, lat_ref, g_ref, sc_ref, sh_ref, xo_ref, ho_ref, *, tm, n_ctx):
    i = pl.program_id(0)
    g = g_ref[...]

    def tile(src_ref, row):
        gain = g * (1.0 + sc_ref[row:row + 1, :])
        shift = sh_ref[row:row + 1, :]

        def slab(r, carry):
            rows = pl.ds(pl.multiple_of(r * BF16_ROWS, BF16_ROWS), BF16_ROWS)
            x = src_ref[rows, :]
            xo_ref[rows, :] = x
            inv = lax.rsqrt(jnp.mean(x * x, axis=-1, keepdims=True) + EPS)
            ho_ref[rows, :] = (x * inv * gain + shift).astype(ho_ref.dtype)
            return carry

        lax.fori_loop(0, tm // BF16_ROWS, slab, 0, unroll=4)

    @pl.when(i * tm < n_ctx)
    def _():
        tile(ctx_ref, 1)

    @pl.when(i * tm >= n_ctx)
    def _():
        tile(lat_ref, 0)


def _modulate_concat(ctx, lat, g, sc, sh):
    n_ctx, d = ctx.shape
    n_lat = lat.shape[0]
    t = n_ctx + n_lat
    tm = math.gcd(n_ctx, 512)
    nbc = n_ctx // tm
    out = pl.BlockSpec((tm, d), lambda i: (i, 0))
    return pl.pallas_call(
        functools.partial(_modulate_concat_body, tm=tm, n_ctx=n_ctx),
        grid=(t // tm,),
        in_specs=[
            pl.BlockSpec((tm, d), lambda i: (jnp.minimum(i, nbc - 1), 0)),
            pl.BlockSpec((tm, d), lambda i: (jnp.maximum(i - nbc, 0), 0)),
            pl.BlockSpec((1, d), lambda i: (0, 0)),
            pl.BlockSpec((2, d), lambda i: (0, 0)),
            pl.BlockSpec((2, d), lambda i: (0, 0)),
        ],
        out_specs=[out, out],
        out_shape=[jax.ShapeDtypeStruct((t, d), F32), jax.ShapeDtypeStruct((t, d), BF16)],
        compiler_params=_cparams(("parallel",)),
        name="modulate_concat",
    )(ctx, lat, g.reshape(1, d), sc, sh)


def _mm_body(x_ref, w_ref, o_ref, wb_ref):
    @pl.when(pl.program_id(1) == 0)
    def _():
        wb_ref[...] = w_ref[...].astype(BF16)

    o_ref[...] = _dot(x_ref[...], wb_ref[...]).astype(o_ref.dtype)


def _matmul(x, w_stack, layer, tn, tm_cap, w_buffers, out_dtype):
    t, k = x.shape
    n = w_stack.shape[-1]
    tm = _row_tile(t, tm_cap)
    return pl.pallas_call(
        _mm_body,
        grid=(n // tn, t // tm),
        in_specs=[
            pl.BlockSpec((tm, k), lambda j, i: (i, 0)),
            _weight_spec((None, k, tn), lambda j, i: (layer, 0, j), w_buffers),
        ],
        out_specs=pl.BlockSpec((tm, tn), lambda j, i: (i, j)),
        out_shape=jax.ShapeDtypeStruct((t, n), out_dtype),
        scratch_shapes=[pltpu.VMEM((k, tn), BF16)],
        compiler_params=_cparams(("arbitrary", "arbitrary")),
        name="matmul",
    )(x, w_stack)


def _mm_residual_body(a_ref, w_ref, x_ref, g_ref, o_ref, wb_ref, *, tm, n_ctx):
    @pl.when(pl.program_id(1) == 0)
    def _():
        wb_ref[...] = w_ref[...].astype(BF16)

    y = _dot(a_ref[...], wb_ref[...])
    is_ctx = _is_ctx_rows(pl.program_id(1), tm, n_ctx)
    gate = jnp.where(is_ctx, g_ref[1:2, :], g_ref[0:1, :])
    o_ref[...] = x_ref[...] + gate * y


def _matmul_residual(a, w_stack, layer, x, gate, n_ctx, tn, tm_cap):
    t, k = a.shape
    n = w_stack.shape[-1]
    tm = _row_tile(t, tm_cap)
    return pl.pallas_call(
        functools.partial(_mm_residual_body, tm=tm, n_ctx=n_ctx),
        grid=(n // tn, t // tm),
        in_specs=[
            pl.BlockSpec((tm, k), lambda j, i: (i, 0)),
            _weight_spec((None, k, tn), lambda j, i: (layer, 0, j)),
            pl.BlockSpec((tm, tn), lambda j, i: (i, j)),
            pl.BlockSpec((2, tn), lambda j, i: (0, j)),
        ],
        out_specs=pl.BlockSpec((tm, tn), lambda j, i: (i, j)),
        out_shape=jax.ShapeDtypeStruct((t, n), F32),
        scratch_shapes=[pltpu.VMEM((k, tn), BF16)],
        compiler_params=_cparams(("arbitrary", "arbitrary")),
        name="matmul_residual",
    )(a, w_stack, x, gate)


def _mm_residual_modulate_body(a_ref, w_ref, x_ref, gate_ref, g_ref, sc_ref, sh_ref,
                               xo_ref, ho_ref, wb_ref, y0_ref, y1_ref, *, tm, n_ctx):
    i = pl.program_id(0)

    @pl.when(i == 0)
    def _():
        wb_ref[...] = w_ref[...].astype(BF16)
        y1_ref[...] = jnp.zeros_like(y1_ref)

    base = jnp.maximum(i - 1, 0) * tm
    g = g_ref[...]
    gain_lat = g * (1.0 + sc_ref[0:1, :])
    gain_ctx = g * (1.0 + sc_ref[1:2, :])

    def run(y_write, y_read):
        y_write[...] = _dot(a_ref[...], wb_ref[...])
        for r in range(tm // BF16_ROWS):
            rows = slice(r * BF16_ROWS, (r + 1) * BF16_ROWS)
            is_ctx = (base + r * BF16_ROWS) < n_ctx
            gate = jnp.where(is_ctx, gate_ref[1:2, :], gate_ref[0:1, :])
            x = x_ref[rows, :] + gate * y_read[rows, :]
            xo_ref[rows, :] = x
            inv = lax.rsqrt(jnp.mean(x * x, axis=-1, keepdims=True) + EPS)
            gain = jnp.where(is_ctx, gain_ctx, gain_lat)
            shift = jnp.where(is_ctx, sh_ref[1:2, :], sh_ref[0:1, :])
            ho_ref[rows, :] = (x * inv * gain + shift).astype(ho_ref.dtype)

    @pl.when(lax.rem(i, 2) == 0)
    def _():
        run(y0_ref, y1_ref)

    @pl.when(lax.rem(i, 2) == 1)
    def _():
        run(y1_ref, y0_ref)


def _matmul_residual_modulate(a, w_stack, layer, x, gate, g, sc, sh, n_ctx, tm_cap):
    t, k = a.shape
    n = w_stack.shape[-1]
    tm = _row_tile(t, tm_cap)
    nt = t // tm
    lagged = pl.BlockSpec((tm, n), lambda i: (jnp.maximum(i - 1, 0), 0))
    vec2 = pl.BlockSpec((2, n), lambda i: (0, 0))
    product = pltpu.VMEM((tm, n), F32)
    return pl.pallas_call(
        functools.partial(_mm_residual_modulate_body, tm=tm, n_ctx=n_ctx),
        grid=(nt + 1,),
        in_specs=[
            pl.BlockSpec((tm, k), lambda i: (jnp.minimum(i, nt - 1), 0)),
            _weight_spec((None, k, n), lambda i: (layer, 0, 0)),
            lagged, vec2,
            pl.BlockSpec((1, n), lambda i: (0, 0)), vec2, vec2,
        ],
        out_specs=[lagged, lagged],
        out_shape=[jax.ShapeDtypeStruct((t, n), F32), jax.ShapeDtypeStruct((t, n), BF16)],
        scratch_shapes=[pltpu.VMEM((k, n), BF16), product, product],
        compiler_params=_cparams(("arbitrary",)),
        name="matmul_residual_modulate",
    )(a, w_stack, x, gate, g.reshape(1, n), sc, sh)


def _ffn_up_body(h_ref, w1_ref, w3_ref, o_ref, w1b_ref, w3b_ref):
    @pl.when(pl.program_id(1) == 0)
    def _():
        w1b_ref[...] = w1_ref[...].astype(BF16)
        w3b_ref[...] = w3_ref[...].astype(BF16)

    h = h_ref[...]
    a = _dot(h, w1b_ref[...])
    b = _dot(h, w3b_ref[...])
    o_ref[...] = (a * jax.nn.sigmoid(a) * b).astype(o_ref.dtype)


def _ffn_up(h, w1_stack, w3_stack, layer, tn, tm_cap, w_buffers):
    t, k = h.shape
    n = w1_stack.shape[-1]
    tm = _row_tile(t, tm_cap)
    wspec = _weight_spec((None, k, tn), lambda j, i: (layer, 0, j), w_buffers)
    return pl.pallas_call(
        _ffn_up_body,
        grid=(n // tn, t // tm),
        in_specs=[pl.BlockSpec((tm, k), lambda j, i: (i, 0)), wspec, wspec],
        out_specs=pl.BlockSpec((tm, tn), lambda j, i: (i, j)),
        out_shape=jax.ShapeDtypeStruct((t, n), BF16),
        scratch_shapes=[pltpu.VMEM((k, tn), BF16), pltpu.VMEM((k, tn), BF16)],
        compiler_params=_cparams(("arbitrary", "arbitrary")),
        name="ffn_up",
    )(h, w1_stack, w3_stack)


def _merge_body(*refs):
    nb = N_BRANCH
    h_ref = refs[0]
    o_refs = refs[1:1 + nb]
    wm_refs = refs[1 + nb:1 + 2 * nb]
    bm_refs = refs[1 + 2 * nb:1 + 3 * nb]
    wb_refs = refs[1 + 3 * nb:1 + 4 * nb]
    out_ref, wmb_ref, wbb_ref = refs[1 + 4 * nb:]

    @pl.when(pl.program_id(1) == 0)
    def _():
        for k in range(nb):
            wmb_ref[k] = wm_refs[k][...].astype(BF16)
            wbb_ref[k] = wb_refs[k][...].astype(BF16)

    h = h_ref[...]
    acc = None
    for k in range(nb):
        gate = jax.nn.sigmoid(_dot(h, wmb_ref[k]) + bm_refs[k][...])
        term = gate * _dot(o_refs[k][...], wbb_ref[k])
        acc = term if acc is None else acc + term
    out_ref[...] = acc.astype(out_ref.dtype)


def _merge(h, outs, w_merge, b_merge, w_branch, layer, tn, tm_cap, w_buffers):
    t, d = h.shape
    bw = outs[0].shape[1]
    depth = w_merge.shape[0]
    tm = _row_tile(t, tm_cap)
    nt = d // tn
    b_merge3 = b_merge.reshape(depth, 1, N_BRANCH * d)
    in_specs = [pl.BlockSpec((tm, d), lambda j, i: (i, 0))]
    in_specs += [pl.BlockSpec((tm, bw), lambda j, i: (i, 0)) for _ in range(N_BRANCH)]
    in_specs += [_weight_spec((None, d, tn), lambda j, i, k=k: (layer, 0, k * nt + j), w_buffers)
                 for k in range(N_BRANCH)]
    in_specs += [pl.BlockSpec((None, 1, tn), lambda j, i, k=k: (layer, 0, k * nt + j))
                 for k in range(N_BRANCH)]
    in_specs += [_weight_spec((None, None, bw, tn), lambda j, i, k=k: (layer, k, 0, j), w_buffers)
                 for k in range(N_BRANCH)]
    return pl.pallas_call(
        _merge_body,
        grid=(nt, t // tm),
        in_specs=in_specs,
        out_specs=pl.BlockSpec((tm, tn), lambda j, i: (i, j)),
        out_shape=jax.ShapeDtypeStruct((t, d), BF16),
        scratch_shapes=[pltpu.VMEM((N_BRANCH, d, tn), BF16),
                        pltpu.VMEM((N_BRANCH, bw, tn), BF16)],
        compiler_params=_cparams(("arbitrary", "arbitrary")),
        name="merge",
    )(h, *outs, *([w_merge] * N_BRANCH), *([b_merge3] * N_BRANCH), *([w_branch] * N_BRANCH))


def _block_order(step, n_blocks, n_ctx_blocks, reverse):
    if not reverse:
        return step
    return jnp.where(step < n_ctx_blocks, n_ctx_blocks - 1 - step,
                     n_blocks + n_ctx_blocks - 1 - step)


def _retention_body(qf_ref, kf_ref, vf_ref, cosf_ref, sinf_ref,
                    qb_ref, kb_ref, vb_ref, cosb_ref, sinb_ref, tab_ref,
                    of_ref, ob_ref, s_ref):
    @pl.when(pl.program_id(0) == 0)
    def _():
        s_ref[...] = jnp.zeros_like(s_ref)

    chunks = SEQ_BLOCK // CHUNK
    dirs = ((qf_ref, kf_ref, vf_ref, cosf_ref, sinf_ref, of_ref, range(chunks)),
            (qb_ref, kb_ref, vb_ref, cosb_ref, sinb_ref, ob_ref, range(chunks - 1, -1, -1)))
    for d, (q_ref, k_ref, v_ref, cos_ref, sin_ref, o_ref, order) in enumerate(dirs):
        for h in range(RET_HEADS):
            sl = slice(h * RET_DK, (h + 1) * RET_DK)
            dmat = tab_ref[d, 0, h]
            q_decay = tab_ref[d, 1, h]
            k_decay = tab_ref[d, 2, h]
            g_chunk = tab_ref[d, 3, h]
            state = s_ref[d, h]
            for ci in order:
                rows = slice(ci * CHUNK, (ci + 1) * CHUNK)
                cos = cos_ref[rows, :]
                sin = sin_ref[rows, :]
                q = q_ref[rows, sl].astype(F32)
                k = k_ref[rows, sl].astype(F32)
                v = v_ref[rows, sl].astype(BF16)
                qr = q * cos + pltpu.roll(q, RET_DK // 2, 1) * sin
                kr = k * cos + pltpu.roll(k, RET_DK // 2, 1) * sin
                scores = lax.dot_general(qr.astype(BF16), kr.astype(BF16),
                                         (((1,), (1,)), ((), ())),
                                         preferred_element_type=F32) * dmat
                intra = _dot(scores.astype(BF16), v)
                cross = _dot((qr * q_decay).astype(BF16), state.astype(BF16))
                kv = _dot((kr * k_decay).T.astype(BF16), v)
                o_ref[rows, sl] = intra + cross
                state = g_chunk * state + kv
            s_ref[d, h] = state


def _retention_tables(log_gamma):
    pos = jnp.arange(CHUNK, dtype=F32)
    i = pos[:, None]
    j = pos[None, :]
    full = log_gamma.shape[:1] + (RET_HEADS, CHUNK, RET_DK)
    scale = RET_DK ** -0.5

    def one_dir(lg, diff, mask, q_pow, k_pow):
        dmat = jnp.where(mask, jnp.exp(lg[:, :, None, None] * jnp.maximum(diff, 0.0)), 0.0) * scale
        q_decay = jnp.broadcast_to(jnp.exp(lg[:, :, None] * q_pow)[..., None], full) * scale
        k_decay = jnp.broadcast_to(jnp.exp(lg[:, :, None] * k_pow)[..., None], full)
        g_chunk = jnp.broadcast_to(jnp.exp(lg * CHUNK)[:, :, None, None], full)
        return jnp.stack([dmat, q_decay, k_decay, g_chunk], axis=1)

    fwd = one_dir(log_gamma[:, 0], i - j, (i - j) >= 0, pos + 1.0, CHUNK - 1.0 - pos)
    bwd = one_dir(log_gamma[:, 1], j - i, (j - i) > 0, CHUNK - pos, pos)
    return jnp.stack([fwd, bwd], axis=1)


def _retention(proj, cos2, sin2, tables, layer, n_ctx):
    t = proj.shape[0]
    nb = t // SEQ_BLOCK
    nbc = n_ctx // SEQ_BLOCK

    def specs(reverse):
        def blk(st):
            return _block_order(st, nb, nbc, reverse)
        sec = [pl.BlockSpec((SEQ_BLOCK, BRANCH_W), lambda st, s=s: (blk(st), s))
               for s in (SEC_Q, SEC_K, SEC_V)]
        row = pl.BlockSpec((SEQ_BLOCK, RET_DK), lambda st: (blk(st), 0))
        out = pl.BlockSpec((SEQ_BLOCK, BRANCH_W), lambda st: (blk(st), 0))
        return sec + [row, row], out

    in_f, out_f = specs(False)
    in_b, out_b = specs(True)
    tab_spec = pl.BlockSpec((None, 2, 4, RET_HEADS, CHUNK, RET_DK),
                            lambda st: (layer, 0, 0, 0, 0, 0))
    shape = jax.ShapeDtypeStruct((t, BRANCH_W), F32)
    return pl.pallas_call(
        _retention_body,
        grid=(nb,),
        in_specs=in_f + in_b + [tab_spec],
        out_specs=[out_f, out_b],
        out_shape=[shape, shape],
        scratch_shapes=[pltpu.VMEM((2, RET_HEADS, RET_DK, RET_DK), F32)],
        compiler_params=_cparams(("arbitrary",)),
        name="retention",
    )(proj, proj, proj, cos2, sin2, proj, proj, proj, cos2, sin2, tables)


def _s5_body(uf_ref, ub_ref, wb_ref, a_ref, wc_ref, d_ref, yf_ref, yb_ref,
             x_ref, bu_ref, st_ref):
    @pl.when(pl.program_id(0) == 0)
    def _():
        x_ref[...] = jnp.zeros_like(x_ref)

    for d, u_ref in enumerate((uf_ref, ub_ref)):
        ub = u_ref[...].astype(BF16)
        for r in range(S5_LANE_BLOCKS):
            bu = _dot(ub[:, r * LANES:(r + 1) * LANES], wb_ref[d, r])
            cols = slice(r * S5_BLOCK_STATE, (r + 1) * S5_BLOCK_STATE)
            bu_ref[2 * d, :, cols] = bu[:, :S5_BLOCK_STATE]
            bu_ref[2 * d + 1, :, cols] = bu[:, S5_BLOCK_STATE:]

    af_re, af_im = a_ref[0, 0:1, :], a_ref[0, 1:2, :]
    ab_re, ab_im = a_ref[1, 0:1, :], a_ref[1, 1:2, :]

    def step(i, carry):
        f_re, f_im, b_re, b_im = carry
        tf = pl.ds(i, 1)
        tb = pl.ds(SEQ_BLOCK - 1 - i, 1)
        nf_re = af_re * f_re - af_im * f_im + bu_ref[0, tf, :]
        nf_im = af_re * f_im + af_im * f_re + bu_ref[1, tf, :]
        nb_re = ab_re * b_re - ab_im * b_im + bu_ref[2, tb, :]
        nb_im = ab_re * b_im + ab_im * b_re + bu_ref[3, tb, :]
        st_ref[0, tf, :] = nf_re
        st_ref[1, tf, :] = nf_im
        st_ref[2, tb, :] = nb_re
        st_ref[3, tb, :] = nb_im
        return nf_re, nf_im, nb_re, nb_im

    carry = lax.fori_loop(0, SEQ_BLOCK, step,
                          (x_ref[0:1, :], x_ref[1:2, :], x_ref[2:3, :], x_ref[3:4, :]),
                          unroll=8)
    for n, val in enumerate(carry):
        x_ref[n:n + 1, :] = val

    def readout(d):
        ys = []
        for r in range(S5_LANE_BLOCKS):
            cols = slice(r * S5_BLOCK_STATE, (r + 1) * S5_BLOCK_STATE)
            st = jnp.concatenate([st_ref[2 * d, :, cols], st_ref[2 * d + 1, :, cols]],
                                 axis=1).astype(BF16)
            ys.append(_dot(st, wc_ref[r]))
        return jnp.concatenate(ys, axis=1)

    yf_ref[...] = readout(0) + d_ref[...] * uf_ref[...].astype(F32)
    yb_ref[...] = readout(1)


def _s5_prepare(a_re, a_im, b_re, b_im, log_dt, c_re, c_im):
    depth = a_re.shape[0]
    a_re = a_re[:, None]
    a_im = a_im[:, None]
    dt = jnp.exp(log_dt)[..., None]
    mag = jnp.exp(dt * a_re)
    ang = dt * a_im
    ab_re, ab_im = mag * jnp.cos(ang), mag * jnp.sin(ang)
    nr, ni = ab_re - 1.0, ab_im
    den = a_re * a_re + a_im * a_im
    f_re = (nr * a_re + ni * a_im) / den
    f_im = (ni * a_re - nr * a_im) / den
    bb_re = f_re[..., None] * b_re - f_im[..., None] * b_im
    bb_im = f_re[..., None] * b_im + f_im[..., None] * b_re
    gpb = S5_GROUPS_PER_BLOCK
    row_group = jnp.arange(LANES) // S5_GROUP
    col_group = jnp.arange(S5_BLOCK_STATE) // S5_STATE

    def in_map(bb):
        rows = jnp.swapaxes(bb.reshape(depth, 2, S5_LANE_BLOCKS, gpb, S5_STATE, S5_GROUP), -1, -2)
        rows = rows.reshape(depth, 2, S5_LANE_BLOCKS, LANES, S5_STATE)
        mask = (row_group[:, None] == col_group[None, :]).astype(F32)
        return jnp.tile(rows, (1, 1, 1, 1, gpb)) * mask

    def out_map(c):
        rows = jnp.swapaxes(c.reshape(depth, S5_LANE_BLOCKS, gpb, S5_GROUP, S5_STATE), -1, -2)
        rows = rows.reshape(depth, S5_LANE_BLOCKS, S5_BLOCK_STATE, S5_GROUP)
        mask = (col_group[:, None] == row_group[None, :]).astype(F32)
        return jnp.tile(rows, (1, 1, 1, gpb)) * mask

    wb = jnp.concatenate([in_map(bb_re), in_map(bb_im)], axis=-1).astype(BF16)
    a = jnp.stack([ab_re.reshape(depth, 2, S5_STATE_W), ab_im.reshape(depth, 2, S5_STATE_W)],
                  axis=2)
    wc = jnp.concatenate([out_map(c_re), out_map(-c_im)], axis=-2).astype(BF16)
    return wb, a, wc


def _s5(proj, wb, a, wc, s5_d, layer, n_ctx):
    t = proj.shape[0]
    nb = t // SEQ_BLOCK
    nbc = n_ctx // SEQ_BLOCK
    depth = s5_d.shape[0]

    def io(reverse, sec):
        return pl.BlockSpec((SEQ_BLOCK, BRANCH_W),
                            lambda st: (_block_order(st, nb, nbc, reverse), sec))

    shape = jax.ShapeDtypeStruct((t, BRANCH_W), F32)
    planes = pltpu.VMEM((4, SEQ_BLOCK, S5_STATE_W), F32)
    return pl.pallas_call(
        _s5_body,
        grid=(nb,),
        in_specs=[io(False, SEC_U), io(True, SEC_U),
                  pl.BlockSpec((None,) + wb.shape[1:], lambda st: (layer, 0, 0, 0, 0)),
                  pl.BlockSpec((None,) + a.shape[1:], lambda st: (layer, 0, 0, 0)),
                  pl.BlockSpec((None,) + wc.shape[1:], lambda st: (layer, 0, 0, 0)),
                  pl.BlockSpec((None, 1, BRANCH_W), lambda st: (layer, 0, 0))],
        out_specs=[io(False, 0), io(True, 0)],
        out_shape=[shape, shape],
        scratch_shapes=[pltpu.VMEM((4, S5_STATE_W), F32), planes, planes],
        compiler_params=_cparams(("arbitrary",)),
        name="s5",
    )(proj, proj, wb, a, wc, s5_d.reshape(depth, 1, BRANCH_W))


def _local_body(of_ref, ob_ref, g_ref, yf_ref, yb_ref, wglu_ref,
                cx_ref, cb_ref, cc_ref, cxp_ref, ccp_ref, cxn_ref, ccn_ref, cw_ref,
                gu_ref, gv_ref, lng_ref, lnb_ref, ws_ref, bs_ref,
                oret_ref, os5_ref, oc_ref, og_ref, *, n_blocks, n_ctx_blocks):
    for h in range(RET_HEADS):
        sl = slice(h * RET_DK, (h + 1) * RET_DK)
        o = of_ref[:, sl] + ob_ref[:, sl]
        mu = jnp.mean(o, axis=-1, keepdims=True)
        var = jnp.mean(jnp.square(o - mu), axis=-1, keepdims=True)
        o = (o - mu) * lax.rsqrt(var + EPS)
        g = g_ref[:, sl].astype(F32)
        oret_ref[:, sl] = (o * (g * jax.nn.sigmoid(g))).astype(oret_ref.dtype)

    y = jax.nn.gelu(yf_ref[...] + yb_ref[...])
    os5_ref[...] = (y * jax.nn.sigmoid(_dot(y.astype(BF16), wglu_ref[...]))).astype(os5_ref.dtype)

    c = pl.program_id(0)
    seq_start = jnp.logical_or(c == 0, c == n_ctx_blocks)
    seq_end = jnp.logical_or(c == n_ctx_blocks - 1, c == n_blocks - 1)
    z = cc_ref[...].astype(F32) * cx_ref[...].astype(F32)
    z_prev = (ccp_ref[...].astype(F32) * cxp_ref[...].astype(F32))[HALO_ROWS - 1:HALO_ROWS, :]
    z_next = (ccn_ref[...].astype(F32) * cxn_ref[...].astype(F32))[0:1, :]
    z_prev = jnp.where(seq_start, 0.0, z_prev)
    z_next = jnp.where(seq_end, 0.0, z_next)
    rows = lax.broadcasted_iota(jnp.int32, (SEQ_BLOCK, 1), 0)
    z_up = jnp.where(rows == 0, z_prev, pltpu.roll(z, 1, 0))
    z_dn = jnp.where(rows == SEQ_BLOCK - 1, z_next, pltpu.roll(z, SEQ_BLOCK - 1, 0))
    yc = cw_ref[0:1, :] * z_up + cw_ref[1:2, :] * z + cw_ref[2:3, :] * z_dn
    oc_ref[...] = (cb_ref[...].astype(F32) * yc).astype(oc_ref.dtype)

    u = jax.nn.gelu(gu_ref[...].astype(F32))
    v = jax.nn.gelu(gv_ref[...].astype(F32))
    mu = jnp.mean(v, axis=-1, keepdims=True)
    var = jnp.mean(jnp.square(v - mu), axis=-1, keepdims=True)
    v = ((v - mu) * lax.rsqrt(var + EPS) * lng_ref[...] + lnb_ref[...]).astype(BF16)
    for ci in range(SEQ_BLOCK // CHUNK):
        rws = slice(ci * CHUNK, (ci + 1) * CHUNK)
        for g in range(CMLP_GROUPS):
            sl = slice(g * CMLP_GW, (g + 1) * CMLP_GW)
            mixed = _dot(ws_ref[g], v[rws, sl]) + bs_ref[g]
            og_ref[rws, sl] = (u[rws, sl] * mixed).astype(og_ref.dtype)


def _local_mixers(proj, o_f, o_b, y_f, y_b, wglu_b, conv_w, ln_g, ln_b, ws_b, bs_full,
                  layer, n_ctx):
    t = proj.shape[0]
    nb = t // SEQ_BLOCK
    nbc = n_ctx // SEQ_BLOCK
    per = SEQ_BLOCK // HALO_ROWS
    last = t // HALO_ROWS - 1
    depth = conv_w.shape[0]

    def sec(s):
        return pl.BlockSpec((SEQ_BLOCK, BRANCH_W), lambda c: (c, s))

    def prev(s):
        return pl.BlockSpec((HALO_ROWS, BRANCH_W), lambda c: (jnp.maximum(c * per - 1, 0), s))

    def nxt(s):
        return pl.BlockSpec((HALO_ROWS, BRANCH_W), lambda c: (jnp.minimum((c + 1) * per, last), s))

    def layer_spec(x):
        nd = x.ndim - 1
        return pl.BlockSpec((None,) + x.shape[1:], lambda c: (layer,) + (0,) * nd)

    ln_g3 = ln_g.reshape(depth, 1, BRANCH_W)
    ln_b3 = ln_b.reshape(depth, 1, BRANCH_W)
    blk = sec(0)
    out_shape = jax.ShapeDtypeStruct((t, BRANCH_W), BF16)
    return pl.pallas_call(
        functools.partial(_local_body, n_blocks=nb, n_ctx_blocks=nbc),
        grid=(nb,),
        in_specs=[blk, blk, sec(SEC_G), blk, blk, layer_spec(wglu_b),
                  sec(SEC_CX), sec(SEC_CB), sec(SEC_CC),
                  prev(SEC_CX), prev(SEC_CC), nxt(SEC_CX), nxt(SEC_CC), layer_spec(conv_w),
                  sec(SEC_GU), sec(SEC_GV), layer_spec(ln_g3), layer_spec(ln_b3),
                  layer_spec(ws_b), layer_spec(bs_full)],
        out_specs=[blk, blk, blk, blk],
        out_shape=[out_shape] * 4,
        compiler_params=_cparams(("parallel",)),
        name="local_mixers",
    )(o_f, o_b, proj, y_f, y_b, wglu_b,
      proj, proj, proj, proj, proj, proj, proj, conv_w,
      proj, proj, ln_g3, ln_b3, ws_b, bs_full)


def _final_norm_body(x_ref, g_ref, o_ref, *, tm):
    g = g_ref[...]

    def slab(r, carry):
        start = pl.multiple_of(r * BF16_ROWS, BF16_ROWS)
        x = x_ref[pl.ds(start, BF16_ROWS), :]
        inv = lax.rsqrt(jnp.mean(x * x, axis=-1, keepdims=True) + EPS)
        o_ref[pl.ds(start, BF16_ROWS), :] = x * inv * g
        return carry

    lax.fori_loop(0, tm // BF16_ROWS, slab, 0, unroll=2)


def _final_norm(x, g, n_ctx):
    t, d = x.shape
    tm = math.gcd(n_ctx, 512)
    skip = n_ctx // tm
    n_lat = t - n_ctx
    return pl.pallas_call(
        functools.partial(_final_norm_body, tm=tm),
        grid=(n_lat // tm,),
        in_specs=[pl.BlockSpec((tm, d), lambda i: (i + skip, 0)),
                  pl.BlockSpec((1, d), lambda i: (0, 0))],
        out_specs=pl.BlockSpec((tm, d), lambda i: (i, 0)),
        out_shape=jax.ShapeDtypeStruct((n_lat, d), F32),
        compiler_params=_cparams(("parallel",)),
        name="final_norm",
    )(x, g.reshape(1, d))


def _rope_tables(n_lat, n_ctx):
    rows = n_lat // GRID_W
    row = jnp.broadcast_to(jnp.arange(rows)[:, None], (rows, GRID_W)).reshape(-1).astype(F32)
    col = jnp.broadcast_to(jnp.arange(GRID_W)[None, :], (rows, GRID_W)).reshape(-1).astype(F32)
    nf = RET_DK // 4
    freqs = ROPE_BASE ** (-jnp.arange(nf, dtype=F32) / nf)
    ang = jnp.concatenate([row[:, None] * freqs, col[:, None] * freqs], axis=-1)
    cos, sin = jnp.cos(ang), jnp.sin(ang)
    cos2 = jnp.concatenate([cos, cos], axis=-1)
    sin2 = jnp.concatenate([-sin, sin], axis=-1)
    cos2 = jnp.concatenate([jnp.ones((n_ctx, RET_DK), F32), cos2], axis=0)
    sin2 = jnp.concatenate([jnp.zeros((n_ctx, RET_DK), F32), sin2], axis=0)
    return cos2, sin2


def kernel(x, c, ctx, c_ctx, ada_w, ada_b, norm1_g, norm2_g, w_in, ret_decay_logit, s5_a_re, s5_a_im, s5_b_re, s5_b_im, s5_c_re, s5_c_im, s5_d, s5_log_dt, s5_w_glu, conv_w, cmlp_ln_g, cmlp_ln_b, cmlp_ws, cmlp_bs, w_branch, w_merge, b_merge, w_out, ffn_w1, ffn_w3, ffn_w2, final_norm_g):
    batch, n_lat, d = x.shape
    assert batch == 1 and c.shape[0] == 1 and ctx.shape[0] == 1
    n_ctx = ctx.shape[1]
    assert n_ctx % SEQ_BLOCK == 0 and n_lat % SEQ_BLOCK == 0 and n_lat % GRID_W == 0
    depth = ada_w.shape[0]

    cos2, sin2 = _rope_tables(n_lat, n_ctx)
    ret_tables = _retention_tables(jax.nn.log_sigmoid(ret_decay_logit.astype(F32)))
    s5_wb, s5_a, s5_wc = _s5_prepare(s5_a_re, s5_a_im, s5_b_re, s5_b_im, s5_log_dt,
                                     s5_c_re, s5_c_im)
    wglu_b = s5_w_glu.astype(BF16)
    ws_b = cmlp_ws.astype(BF16)
    bs_full = jnp.broadcast_to(cmlp_bs[..., None], cmlp_bs.shape + (CMLP_GW,))

    mod_all = _adaln(c[0], c_ctx, ada_w, ada_b)
    mod_all = mod_all[:, 0:2].reshape(depth, 2, 6, d)
    xs = None

    for l in range(depth):
        sh1, sc1, g1, sh2, sc2, g2 = [mod_all[l, :, i] for i in range(6)]
        if l == 0:
            xs, h = _modulate_concat(ctx[0], x[0], norm1_g[l], sc1, sh1)
        else:
            h = _modulate(xs, norm1_g[l], sc1, sh1, n_ctx)
        proj = _matmul(h, w_in, l, 1280, 528, 2, BF16)

        o_f, o_b = _retention(proj, cos2, sin2, ret_tables, l, n_ctx)
        y_f, y_b = _s5(proj, s5_wb, s5_a, s5_wc, s5_d, l, n_ctx)
        outs = _local_mixers(proj, o_f, o_b, y_f, y_b, wglu_b, conv_w, cmlp_ln_g, cmlp_ln_b,
                             ws_b, bs_full, l, n_ctx)

        merged = _merge(h, outs, w_merge, b_merge, w_branch, l, 256, 1056, 2)
        xs, h2 = _matmul_residual_modulate(merged, w_out, l, xs, g1, norm2_g[l], sc2, sh2,
                                           n_ctx, 352)
        act = _ffn_up(h2, ffn_w1, ffn_w3, l, 512, 1056, 2)
        xs = _matmul_residual(act, ffn_w2, l, xs, g2, n_ctx, 1024, 352)

    return _final_norm(xs, final_norm_g, n_ctx)[None]
```

```python
import functools
import math

import jax
import jax.numpy as jnp
from jax import lax
from jax.experimental import pallas as pl
from jax.experimental.pallas import tpu as pltpu

F32 = jnp.float32
BF16 = jnp.bfloat16

D_MODEL = 2048
GRID_W = 64
BRANCH_W = 512
N_BRANCH = 4
N_SECTIONS = 10
RET_HEADS = 4
RET_DK = BRANCH_W // RET_HEADS
ROPE_BASE = 10000.0
S5_GROUP = 16
S5_GROUPS = BRANCH_W // S5_GROUP
S5_STATE = 64
CMLP_GROUPS = 4
CMLP_GW = BRANCH_W // CMLP_GROUPS
EPS = 1e-6

CHUNK = 128
SEQ_BLOCK = 256
LANES = 128
SUBLANES = 8
BF16_ROWS = 16
HALO_ROWS = BF16_ROWS
S5_LANE_BLOCKS = BRANCH_W // LANES
S5_GROUPS_PER_BLOCK = LANES // S5_GROUP
S5_BLOCK_STATE = S5_GROUPS_PER_BLOCK * S5_STATE
S5_STATE_W = S5_GROUPS * S5_STATE
MIB = 1024 * 1024
VMEM_BUDGET_MIB = 56

SEC_Q, SEC_K, SEC_V, SEC_G, SEC_U, SEC_CX, SEC_CB, SEC_CC, SEC_GU, SEC_GV = range(N_SECTIONS)


def _cparams(semantics, vmem_mib=VMEM_BUDGET_MIB):
    return pltpu.CompilerParams(dimension_semantics=semantics,
                                vmem_limit_bytes=vmem_mib * MIB)


def _dot(a, b):
    return jnp.dot(a, b, preferred_element_type=F32)


def _row_tile(t, cap=1056):
    for tm in (1056, 1024, 768, 704, 528, 512, 384, 352, 256, 128):
        if tm <= cap and t % tm == 0:
            return tm
    raise ValueError(f"unsupported token count {t}")


def _is_ctx_rows(tile_idx, tm, n_ctx):
    rows = tile_idx * tm + lax.broadcasted_iota(jnp.int32, (tm, 1), 0)
    return rows < n_ctx


def _weight_spec(block, index_map, buffers=1):
    return pl.BlockSpec(block, index_map, pipeline_mode=pl.Buffered(buffers))


def _whole(x):
    nd = x.ndim
    return pl.BlockSpec(x.shape, lambda *_: (0,) * nd)


def _adaln_body(c_ref, w_ref, b_ref, o_ref, *, d, tn):
    reps = tn // LANES

    def kblock(kb, accs):
        rows = pl.ds(pl.multiple_of(kb * SUBLANES, SUBLANES), SUBLANES)
        w = w_ref[rows, :]
        out = []
        for r, acc in enumerate(accs):
            c = c_ref[r, rows, :]
            s = jnp.concatenate([c * jax.nn.sigmoid(c)] * reps, axis=1)
            out.append(acc + s * w)
        return tuple(out)

    zero = jnp.zeros((SUBLANES, tn), F32)
    accs = lax.fori_loop(0, d // SUBLANES, kblock, (zero, zero), unroll=8)
    rows = [jnp.sum(acc, axis=0, keepdims=True) + b_ref[...] for acc in accs]
    o_ref[...] = jnp.concatenate(rows + [jnp.zeros((SUBLANES - len(rows), tn), F32)], axis=0)


def _adaln(c, c_ctx, ada_w, ada_b):
    depth, d, n = ada_w.shape
    tn = 1024
    c_rep = jnp.broadcast_to(jnp.stack([c, c_ctx])[:, :, None], (2, d, LANES))
    return pl.pallas_call(
        functools.partial(_adaln_body, d=d, tn=tn),
        grid=(depth, n // tn),
        in_specs=[
            pl.BlockSpec((2, d, LANES), lambda l, j: (0, 0, 0)),
            pl.BlockSpec((None, d, tn), lambda l, j: (l, 0, j)),
            pl.BlockSpec((None, 1, tn), lambda l, j: (l, 0, j)),
        ],
        out_specs=pl.BlockSpec((None, SUBLANES, tn), lambda l, j: (l, 0, j)),
        out_shape=jax.ShapeDtypeStruct((depth, SUBLANES, n), F32),
        compiler_params=_cparams(("parallel", "parallel")),
        name="adaln",
    )(c_rep, ada_w, ada_b.reshape(depth, 1, n))


def _modulate_body(x_ref, g_ref, sc_ref, sh_ref, o_ref, *, tm, n_ctx):
    base = pl.program_id(0) * tm
    g = g_ref[...]
    gain_lat = g * (1.0 + sc_ref[0:1, :])
    gain_ctx = g * (1.0 + sc_ref[1:2, :])
    shift_lat = sh_ref[0:1, :]
    shift_ctx = sh_ref[1:2, :]

    def slab(r, carry):
        start = pl.multiple_of(r * BF16_ROWS, BF16_ROWS)
        x = x_ref[pl.ds(start, BF16_ROWS), :]
        inv = lax.rsqrt(jnp.mean(x * x, axis=-1, keepdims=True) + EPS)
        is_ctx = (base + start) < n_ctx
        gain = jnp.where(is_ctx, gain_ctx, gain_lat)
        shift = jnp.where(is_ctx, shift_ctx, shift_lat)
        o_ref[pl.ds(start, BF16_ROWS), :] = (x * inv * gain + shift).astype(o_ref.dtype)
        return carry

    lax.fori_loop(0, tm // BF16_ROWS, slab, 0, unroll=3)


def _modulate(x, g, sc, sh, n_ctx):
    t, d = x.shape
    tm = _row_tile(t, cap=528)
    return pl.pallas_call(
        functools.partial(_modulate_body, tm=tm, n_ctx=n_ctx),
        grid=(t // tm,),
        in_specs=[
            pl.BlockSpec((tm, d), lambda i: (i, 0)),
            pl.BlockSpec((1, d), lambda i: (0, 0)),
            pl.BlockSpec((2, d), lambda i: (0, 0)),
            pl.BlockSpec((2, d), lambda i: (0, 0)),
        ],
        out_specs=pl.BlockSpec((tm, d), lambda i: (i, 0)),
        out_shape=jax.ShapeDtypeStruct((t, d), BF16),
        compiler_params=_cparams(("parallel",)),
        name="modulate",
    )(x, g.reshape(1, d), sc, sh)


def _modulate_concat_body(ctx_ref, lat_ref, g_ref, sc_ref, sh_ref, xo_ref, ho_ref, *, tm, n_ctx):
    i = pl.program_id(0)
    g = g_ref[...]

    def tile(src_ref, row):
        gain = g * (1.0 + sc_ref[row:row + 1, :])
        shift = sh_ref[row:row + 1, :]

        def slab(r, carry):
            rows = pl.ds(pl.multiple_of(r * BF16_ROWS, BF16_ROWS), BF16_ROWS)
            x = src_ref[rows, :]
            xo_ref[rows, :] = x
            inv = lax.rsqrt(jnp.mean(x * x, axis=-1, keepdims=True) + EPS)
            ho_ref[rows, :] = (x * inv * gain + shift).astype(ho_ref.dtype)
            return carry

        lax.fori_loop(0, tm // BF16_ROWS, slab, 0, unroll=4)

    @pl.when(i * tm < n_ctx)
    def _():
        tile(ctx_ref, 1)

    @pl.when(i * tm >= n_ctx)
    def _():
        tile(lat_ref, 0)


def _modulate_concat(ctx, lat, g, sc, sh):
    n_ctx, d = ctx.shape
    n_lat = lat.shape[0]
    t = n_ctx + n_lat
    tm = math.gcd(n_ctx, 512)
    nbc = n_ctx // tm
    out = pl.BlockSpec((tm, d), lambda i: (i, 0))
    return pl.pallas_call(
        functools.partial(_modulate_concat_body, tm=tm, n_ctx=n_ctx),
        grid=(t // tm,),
        in_specs=[
            pl.BlockSpec((tm, d), lambda i: (jnp.minimum(i, nbc - 1), 0)),
            pl.BlockSpec((tm, d), lambda i: (jnp.maximum(i - nbc, 0), 0)),
            pl.BlockSpec((1, d), lambda i: (0, 0)),
            pl.BlockSpec((2, d), lambda i: (0, 0)),
            pl.BlockSpec((2, d), lambda i: (0, 0)),
        ],
        out_specs=[out, out],
        out_shape=[jax.ShapeDtypeStruct((t, d), F32), jax.ShapeDtypeStruct((t, d), BF16)],
        compiler_params=_cparams(("parallel",)),
        name="modulate_concat",
    )(ctx, lat, g.reshape(1, d), sc, sh)


def _mm_body(x_ref, w_ref, o_ref, wb_ref):
    @pl.when(pl.program_id(1) == 0)
    def _():
        wb_ref[...] = w_ref[...].astype(BF16)

    o_ref[...] = _dot(x_ref[...], wb_ref[...]).astype(o_ref.dtype)


def _matmul(x, w_stack, layer, tn, tm_cap, w_buffers, out_dtype):
    t, k = x.shape
    n = w_stack.shape[-1]
    tm = _row_tile(t, tm_cap)
    return pl.pallas_call(
        _mm_body,
        grid=(n // tn, t // tm),
        in_specs=[
            pl.BlockSpec((tm, k), lambda j, i: (i, 0)),
            _weight_spec((None, k, tn), lambda j, i: (layer, 0, j), w_buffers),
        ],
        out_specs=pl.BlockSpec((tm, tn), lambda j, i: (i, j)),
        out_shape=jax.ShapeDtypeStruct((t, n), out_dtype),
        scratch_shapes=[pltpu.VMEM((k, tn), BF16)],
        compiler_params=_cparams(("arbitrary", "arbitrary")),
        name="matmul",
    )(x, w_stack)


def _mm_residual_body(a_ref, w_ref, x_ref, g_ref, o_ref, wb_ref, *, tm, n_ctx):
    @pl.when(pl.program_id(1) == 0)
    def _():
        wb_ref[...] = w_ref[...].astype(BF16)

    y = _dot(a_ref[...], wb_ref[...])
    is_ctx = _is_ctx_rows(pl.program_id(1), tm, n_ctx)
    gate = jnp.where(is_ctx, g_ref[1:2, :], g_ref[0:1, :])
    o_ref[...] = x_ref[...] + gate * y


def _matmul_residual(a, w_stack, layer, x, gate, n_ctx, tn, tm_cap, w_buffers):
    t, k = a.shape
    n = w_stack.shape[-1]
    tm = _row_tile(t, tm_cap)
    return pl.pallas_call(
        functools.partial(_mm_residual_body, tm=tm, n_ctx=n_ctx),
        grid=(n // tn, t // tm),
        in_specs=[
            pl.BlockSpec((tm, k), lambda j, i: (i, 0)),
            _weight_spec((None, k, tn), lambda j, i: (layer, 0, j), w_buffers),
            pl.BlockSpec((tm, tn), lambda j, i: (i, j)),
            pl.BlockSpec((2, tn), lambda j, i: (0, j)),
        ],
        out_specs=pl.BlockSpec((tm, tn), lambda j, i: (i, j)),
        out_shape=jax.ShapeDtypeStruct((t, n), F32),
        scratch_shapes=[pltpu.VMEM((k, tn), BF16)],
        compiler_params=_cparams(("arbitrary", "arbitrary")),
        name="matmul_residual",
    )(a, w_stack, x, gate)


def _mm_residual_modulate_body(a_ref, w_ref, x_ref, gate_ref, g_ref, sc_ref, sh_ref,
                               xo_ref, ho_ref, wb_ref, y0_ref, y1_ref, *, tm, n_ctx):
    i = pl.program_id(0)

    @pl.when(i == 0)
    def _():
        wb_ref[...] = w_ref[...].astype(BF16)
        y1_ref[...] = jnp.zeros_like(y1_ref)

    base = jnp.maximum(i - 1, 0) * tm
    g = g_ref[...]
    gain_lat = g * (1.0 + sc_ref[0:1, :])
    gain_ctx = g * (1.0 + sc_ref[1:2, :])

    def run(y_write, y_read):
        y_write[...] = _dot(a_ref[...], wb_ref[...])
        for r in range(tm // BF16_ROWS):
            rows = slice(r * BF16_ROWS, (r + 1) * BF16_ROWS)
            is_ctx = (base + r * BF16_ROWS) < n_ctx
            gate = jnp.where(is_ctx, gate_ref[1:2, :], gate_ref[0:1, :])
            x = x_ref[rows, :] + gate * y_read[rows, :]
            xo_ref[rows, :] = x
            inv = lax.rsqrt(jnp.mean(x * x, axis=-1, keepdims=True) + EPS)
            gain = jnp.where(is_ctx, gain_ctx, gain_lat)
            shift = jnp.where(is_ctx, sh_ref[1:2, :], sh_ref[0:1, :])
            ho_ref[rows, :] = (x * inv * gain + shift).astype(ho_ref.dtype)

    @pl.when(lax.rem(i, 2) == 0)
    def _():
        run(y0_ref, y1_ref)

    @pl.when(lax.rem(i, 2) == 1)
    def _():
        run(y1_ref, y0_ref)


def _matmul_residual_modulate(a, w_stack, layer, x, gate, g, sc, sh, n_ctx, tm_cap):
    t, k = a.shape
    n = w_stack.shape[-1]
    tm = _row_tile(t, tm_cap)
    nt = t // tm
    lagged = pl.BlockSpec((tm, n), lambda i: (jnp.maximum(i - 1, 0), 0))
    vec2 = pl.BlockSpec((2, n), lambda i: (0, 0))
    product = pltpu.VMEM((tm, n), F32)
    return pl.pallas_call(
        functools.partial(_mm_residual_modulate_body, tm=tm, n_ctx=n_ctx),
        grid=(nt + 1,),
        in_specs=[
            pl.BlockSpec((tm, k), lambda i: (jnp.minimum(i, nt - 1), 0)),
            _weight_spec((None, k, n), lambda i: (layer, 0, 0)),
            lagged, vec2,
            pl.BlockSpec((1, n), lambda i: (0, 0)), vec2, vec2,
        ],
        out_specs=[lagged, lagged],
        out_shape=[jax.ShapeDtypeStruct((t, n), F32), jax.ShapeDtypeStruct((t, n), BF16)],
        scratch_shapes=[pltpu.VMEM((k, n), BF16), product, product],
        compiler_params=_cparams(("arbitrary",)),
        name="matmul_residual_modulate",
    )(a, w_stack, x, gate, g.reshape(1, n), sc, sh)


def _ffn_up_body(h_ref, w1_ref, w3_ref, o_ref, w1b_ref, w3b_ref):
    @pl.when(pl.program_id(1) == 0)
    def _():
        w1b_ref[...] = w1_ref[...].astype(BF16)
        w3b_ref[...] = w3_ref[...].astype(BF16)

    h = h_ref[...]
    a = _dot(h, w1b_ref[...])
    b = _dot(h, w3b_ref[...])
    o_ref[...] = (a * jax.nn.sigmoid(a) * b).astype(o_ref.dtype)


def _ffn_up(h, w1_stack, w3_stack, layer, tn, tm_cap, w_buffers):
    t, k = h.shape
    n = w1_stack.shape[-1]
    tm = _row_tile(t, tm_cap)
    wspec = _weight_spec((None, k, tn), lambda j, i: (layer, 0, j), w_buffers)
    return pl.pallas_call(
        _ffn_up_body,
        grid=(n // tn, t // tm),
        in_specs=[pl.BlockSpec((tm, k), lambda j, i: (i, 0)), wspec, wspec],
        out_specs=pl.BlockSpec((tm, tn), lambda j, i: (i, j)),
        out_shape=jax.ShapeDtypeStruct((t, n), BF16),
        scratch_shapes=[pltpu.VMEM((k, tn), BF16), pltpu.VMEM((k, tn), BF16)],
        compiler_params=_cparams(("arbitrary", "arbitrary")),
        name="ffn_up",
    )(h, w1_stack, w3_stack)


def _merge_body(*refs):
    nb = N_BRANCH
    h_ref = refs[0]
    o_refs = refs[1:1 + nb]
    wm_refs = refs[1 + nb:1 + 2 * nb]
    bm_refs = refs[1 + 2 * nb:1 + 3 * nb]
    wb_refs = refs[1 + 3 * nb:1 + 4 * nb]
    out_ref, wmb_ref, wbb_ref = refs[1 + 4 * nb:]

    @pl.when(pl.program_id(1) == 0)
    def _():
        for k in range(nb):
            wmb_ref[k] = wm_refs[k][...].astype(BF16)
            wbb_ref[k] = wb_refs[k][...].astype(BF16)

    h = h_ref[...]
    acc = None
    for k in range(nb):
        gate = jax.nn.sigmoid(_dot(h, wmb_ref[k]) + bm_refs[k][...])
        term = gate * _dot(o_refs[k][...], wbb_ref[k])
        acc = term if acc is None else acc + term
    out_ref[...] = acc.astype(out_ref.dtype)


def _merge(h, outs, w_merge, b_merge, w_branch, layer, tn, tm_cap, w_buffers):
    t, d = h.shape
    bw = outs[0].shape[1]
    depth = w_merge.shape[0]
    tm = _row_tile(t, tm_cap)
    nt = d // tn
    b_merge3 = b_merge.reshape(depth, 1, N_BRANCH * d)
    in_specs = [pl.BlockSpec((tm, d), lambda j, i: (i, 0))]
    in_specs += [pl.BlockSpec((tm, bw), lambda j, i: (i, 0)) for _ in range(N_BRANCH)]
    in_specs += [_weight_spec((None, d, tn), lambda j, i, k=k: (layer, 0, k * nt + j), w_buffers)
                 for k in range(N_BRANCH)]
    in_specs += [pl.BlockSpec((None, 1, tn), lambda j, i, k=k: (layer, 0, k * nt + j))
                 for k in range(N_BRANCH)]
    in_specs += [_weight_spec((None, None, bw, tn), lambda j, i, k=k: (layer, k, 0, j), w_buffers)
                 for k in range(N_BRANCH)]
    return pl.pallas_call(
        _merge_body,
        grid=(nt, t // tm),
        in_specs=in_specs,
        out_specs=pl.BlockSpec((tm, tn), lambda j, i: (i, j)),
        out_shape=jax.ShapeDtypeStruct((t, d), BF16),
        scratch_shapes=[pltpu.VMEM((N_BRANCH, d, tn), BF16),
                        pltpu.VMEM((N_BRANCH, bw, tn), BF16)],
        compiler_params=_cparams(("arbitrary", "arbitrary")),
        name="merge",
    )(h, *outs, *([w_merge] * N_BRANCH), *([b_merge3] * N_BRANCH), *([w_branch] * N_BRANCH))


def _block_order(step, n_blocks, n_ctx_blocks, reverse):
    if not reverse:
        return step
    return jnp.where(step < n_ctx_blocks, n_ctx_blocks - 1 - step,
                     n_blocks + n_ctx_blocks - 1 - step)


def _retention_body(qf_ref, kf_ref, vf_ref, cosf_ref, sinf_ref,
                    qb_ref, kb_ref, vb_ref, cosb_ref, sinb_ref, tab_ref,
                    of_ref, ob_ref, s_ref):
    @pl.when(pl.program_id(0) == 0)
    def _():
        s_ref[...] = jnp.zeros_like(s_ref)

    chunks = SEQ_BLOCK // CHUNK
    dirs = ((qf_ref, kf_ref, vf_ref, cosf_ref, sinf_ref, of_ref, range(chunks)),
            (qb_ref, kb_ref, vb_ref, cosb_ref, sinb_ref, ob_ref, range(chunks - 1, -1, -1)))
    for d, (q_ref, k_ref, v_ref, cos_ref, sin_ref, o_ref, order) in enumerate(dirs):
        for h in range(RET_HEADS):
            sl = slice(h * RET_DK, (h + 1) * RET_DK)
            dmat = tab_ref[d, 0, h]
            q_decay = tab_ref[d, 1, h]
            k_decay = tab_ref[d, 2, h]
            g_chunk = tab_ref[d, 3, h]
            state = s_ref[d, h]
            for ci in order:
                rows = slice(ci * CHUNK, (ci + 1) * CHUNK)
                cos = cos_ref[rows, :]
                sin = sin_ref[rows, :]
                q = q_ref[rows, sl].astype(F32)
                k = k_ref[rows, sl].astype(F32)
                v = v_ref[rows, sl].astype(BF16)
                qr = q * cos + pltpu.roll(q, RET_DK // 2, 1) * sin
                kr = k * cos + pltpu.roll(k, RET_DK // 2, 1) * sin
                scores = lax.dot_general(qr.astype(BF16), kr.astype(BF16),
                                         (((1,), (1,)), ((), ())),
                                         preferred_element_type=F32) * dmat
                intra = _dot(scores.astype(BF16), v)
                cross = _dot((qr * q_decay).astype(BF16), state.astype(BF16))
                kv = _dot((kr * k_decay).T.astype(BF16), v)
                o_ref[rows, sl] = (intra + cross).astype(o_ref.dtype)
                state = g_chunk * state + kv
            s_ref[d, h] = state


def _retention_tables(log_gamma):
    pos = jnp.arange(CHUNK, dtype=F32)
    i = pos[:, None]
    j = pos[None, :]
    full = log_gamma.shape[:1] + (RET_HEADS, CHUNK, RET_DK)
    scale = RET_DK ** -0.5

    def one_dir(lg, diff, mask, q_pow, k_pow):
        dmat = jnp.where(mask, jnp.exp(lg[:, :, None, None] * jnp.maximum(diff, 0.0)), 0.0) * scale
        q_decay = jnp.broadcast_to(jnp.exp(lg[:, :, None] * q_pow)[..., None], full) * scale
        k_decay = jnp.broadcast_to(jnp.exp(lg[:, :, None] * k_pow)[..., None], full)
        g_chunk = jnp.broadcast_to(jnp.exp(lg * CHUNK)[:, :, None, None], full)
        return jnp.stack([dmat, q_decay, k_decay, g_chunk], axis=1)

    fwd = one_dir(log_gamma[:, 0], i - j, (i - j) >= 0, pos + 1.0, CHUNK - 1.0 - pos)
    bwd = one_dir(log_gamma[:, 1], j - i, (j - i) > 0, CHUNK - pos, pos)
    return jnp.stack([fwd, bwd], axis=1)


def _retention(proj, cos2, sin2, tables, layer, n_ctx):
    t = proj.shape[0]
    nb = t // SEQ_BLOCK
    nbc = n_ctx // SEQ_BLOCK

    def specs(reverse):
        def blk(st):
            return _block_order(st, nb, nbc, reverse)
        sec = [pl.BlockSpec((SEQ_BLOCK, BRANCH_W), lambda st, s=s: (blk(st), s))
               for s in (SEC_Q, SEC_K, SEC_V)]
        row = pl.BlockSpec((SEQ_BLOCK, RET_DK), lambda st: (blk(st), 0))
        out = pl.BlockSpec((SEQ_BLOCK, BRANCH_W), lambda st: (blk(st), 0))
        return sec + [row, row], out

    in_f, out_f = specs(False)
    in_b, out_b = specs(True)
    tab_spec = pl.BlockSpec((None, 2, 4, RET_HEADS, CHUNK, RET_DK),
                            lambda st: (layer, 0, 0, 0, 0, 0))
    shape = jax.ShapeDtypeStruct((t, BRANCH_W), BF16)
    return pl.pallas_call(
        _retention_body,
        grid=(nb,),
        in_specs=in_f + in_b + [tab_spec],
        out_specs=[out_f, out_b],
        out_shape=[shape, shape],
        scratch_shapes=[pltpu.VMEM((2, RET_HEADS, RET_DK, RET_DK), F32)],
        compiler_params=_cparams(("arbitrary",)),
        name="retention",
    )(proj, proj, proj, cos2, sin2, proj, proj, proj, cos2, sin2, tables)


def _s5_body(uf_ref, ub_ref, wb_ref, a_ref, wc_ref, d_ref, yf_ref, yb_ref,
             x_ref, bu_ref, st_ref):
    @pl.when(pl.program_id(0) == 0)
    def _():
        x_ref[...] = jnp.zeros_like(x_ref)

    for d, u_ref in enumerate((uf_ref, ub_ref)):
        ub = u_ref[...].astype(BF16)
        for r in range(S5_LANE_BLOCKS):
            bu = _dot(ub[:, r * LANES:(r + 1) * LANES], wb_ref[d, r])
            cols = slice(r * S5_BLOCK_STATE, (r + 1) * S5_BLOCK_STATE)
            bu_ref[2 * d, :, cols] = bu[:, :S5_BLOCK_STATE]
            bu_ref[2 * d + 1, :, cols] = bu[:, S5_BLOCK_STATE:]

    af_re, af_im = a_ref[0, 0:1, :], a_ref[0, 1:2, :]
    ab_re, ab_im = a_ref[1, 0:1, :], a_ref[1, 1:2, :]

    def step(i, carry):
        f_re, f_im, b_re, b_im = carry
        tf = pl.ds(i, 1)
        tb = pl.ds(SEQ_BLOCK - 1 - i, 1)
        nf_re = af_re * f_re - af_im * f_im + bu_ref[0, tf, :]
        nf_im = af_re * f_im + af_im * f_re + bu_ref[1, tf, :]
        nb_re = ab_re * b_re - ab_im * b_im + bu_ref[2, tb, :]
        nb_im = ab_re * b_im + ab_im * b_re + bu_ref[3, tb, :]
        st_ref[0, tf, :] = nf_re
        st_ref[1, tf, :] = nf_im
        st_ref[2, tb, :] = nb_re
        st_ref[3, tb, :] = nb_im
        return nf_re, nf_im, nb_re, nb_im

    carry = lax.fori_loop(0, SEQ_BLOCK, step,
                          (x_ref[0:1, :], x_ref[1:2, :], x_ref[2:3, :], x_ref[3:4, :]),
                          unroll=8)
    for n, val in enumerate(carry):
        x_ref[n:n + 1, :] = val

    def readout(d):
        ys = []
        for r in range(S5_LANE_BLOCKS):
            cols = slice(r * S5_BLOCK_STATE, (r + 1) * S5_BLOCK_STATE)
            st = jnp.concatenate([st_ref[2 * d, :, cols], st_ref[2 * d + 1, :, cols]],
                                 axis=1).astype(BF16)
            ys.append(_dot(st, wc_ref[r]))
        return jnp.concatenate(ys, axis=1)

    yf_ref[...] = (readout(0) + d_ref[...] * uf_ref[...].astype(F32)).astype(yf_ref.dtype)
    yb_ref[...] = readout(1).astype(yb_ref.dtype)


def _s5_prepare(a_re, a_im, b_re, b_im, log_dt, c_re, c_im):
    depth = a_re.shape[0]
    a_re = a_re[:, None]
    a_im = a_im[:, None]
    dt = jnp.exp(log_dt)[..., None]
    mag = jnp.exp(dt * a_re)
    ang = dt * a_im
    ab_re, ab_im = mag * jnp.cos(ang), mag * jnp.sin(ang)
    nr, ni = ab_re - 1.0, ab_im
    den = a_re * a_re + a_im * a_im
    f_re = (nr * a_re + ni * a_im) / den
    f_im = (ni * a_re - nr * a_im) / den
    bb_re = f_re[..., None] * b_re - f_im[..., None] * b_im
    bb_im = f_re[..., None] * b_im + f_im[..., None] * b_re
    gpb = S5_GROUPS_PER_BLOCK
    row_group = jnp.arange(LANES) // S5_GROUP
    col_group = jnp.arange(S5_BLOCK_STATE) // S5_STATE

    def in_map(bb):
        rows = jnp.swapaxes(bb.reshape(depth, 2, S5_LANE_BLOCKS, gpb, S5_STATE, S5_GROUP), -1, -2)
        rows = rows.reshape(depth, 2, S5_LANE_BLOCKS, LANES, S5_STATE)
        mask = (row_group[:, None] == col_group[None, :]).astype(F32)
        return jnp.tile(rows, (1, 1, 1, 1, gpb)) * mask

    def out_map(c):
        rows = jnp.swapaxes(c.reshape(depth, S5_LANE_BLOCKS, gpb, S5_GROUP, S5_STATE), -1, -2)
        rows = rows.reshape(depth, S5_LANE_BLOCKS, S5_BLOCK_STATE, S5_GROUP)
        mask = (col_group[:, None] == row_group[None, :]).astype(F32)
        return jnp.tile(rows, (1, 1, 1, gpb)) * mask

    wb = jnp.concatenate([in_map(bb_re), in_map(bb_im)], axis=-1).astype(BF16)
    a = jnp.stack([ab_re.reshape(depth, 2, S5_STATE_W), ab_im.reshape(depth, 2, S5_STATE_W)],
                  axis=2)
    wc = jnp.concatenate([out_map(c_re), out_map(-c_im)], axis=-2).astype(BF16)
    return wb, a, wc


def _s5(proj, wb, a, wc, s5_d, layer, n_ctx):
    t = proj.shape[0]
    nb = t // SEQ_BLOCK
    nbc = n_ctx // SEQ_BLOCK
    depth = s5_d.shape[0]

    def io(reverse, sec):
        return pl.BlockSpec((SEQ_BLOCK, BRANCH_W),
                            lambda st: (_block_order(st, nb, nbc, reverse), sec))

    shape = jax.ShapeDtypeStruct((t, BRANCH_W), BF16)
    planes = pltpu.VMEM((4, SEQ_BLOCK, S5_STATE_W), F32)
    return pl.pallas_call(
        _s5_body,
        grid=(nb,),
        in_specs=[io(False, SEC_U), io(True, SEC_U),
                  pl.BlockSpec((None,) + wb.shape[1:], lambda st: (layer, 0, 0, 0, 0)),
                  pl.BlockSpec((None,) + a.shape[1:], lambda st: (layer, 0, 0, 0)),
                  pl.BlockSpec((None,) + wc.shape[1:], lambda st: (layer, 0, 0, 0)),
                  pl.BlockSpec((None, 1, BRANCH_W), lambda st: (layer, 0, 0))],
        out_specs=[io(False, 0), io(True, 0)],
        out_shape=[shape, shape],
        scratch_shapes=[pltpu.VMEM((4, S5_STATE_W), F32), planes, planes],
        compiler_params=_cparams(("arbitrary",)),
        name="s5",
    )(proj, proj, wb, a, wc, s5_d.reshape(depth, 1, BRANCH_W))


def _local_body(of_ref, ob_ref, g_ref, yf_ref, yb_ref, wglu_ref,
                cx_ref, cb_ref, cc_ref, cxp_ref, ccp_ref, cxn_ref, ccn_ref, cw_ref,
                gu_ref, gv_ref, lng_ref, lnb_ref, ws_ref, bs_ref,
                oret_ref, os5_ref, oc_ref, og_ref, *, n_blocks, n_ctx_blocks):
    for h in range(RET_HEADS):
        sl = slice(h * RET_DK, (h + 1) * RET_DK)
        o = of_ref[:, sl].astype(F32) + ob_ref[:, sl].astype(F32)
        mu = jnp.mean(o, axis=-1, keepdims=True)
        var = jnp.mean(jnp.square(o - mu), axis=-1, keepdims=True)
        o = (o - mu) * lax.rsqrt(var + EPS)
        g = g_ref[:, sl].astype(F32)
        oret_ref[:, sl] = (o * (g * jax.nn.sigmoid(g))).astype(oret_ref.dtype)

    y = jax.nn.gelu(yf_ref[...].astype(F32) + yb_ref[...].astype(F32))
    os5_ref[...] = (y * jax.nn.sigmoid(_dot(y.astype(BF16), wglu_ref[...]))).astype(os5_ref.dtype)

    c = pl.program_id(0)
    seq_start = jnp.logical_or(c == 0, c == n_ctx_blocks)
    seq_end = jnp.logical_or(c == n_ctx_blocks - 1, c == n_blocks - 1)
    z = cc_ref[...].astype(F32) * cx_ref[...].astype(F32)
    z_prev = (ccp_ref[...].astype(F32) * cxp_ref[...].astype(F32))[HALO_ROWS - 1:HALO_ROWS, :]
    z_next = (ccn_ref[...].astype(F32) * cxn_ref[...].astype(F32))[0:1, :]
    z_prev = jnp.where(seq_start, 0.0, z_prev)
    z_next = jnp.where(seq_end, 0.0, z_next)
    rows = lax.broadcasted_iota(jnp.int32, (SEQ_BLOCK, 1), 0)
    z_up = jnp.where(rows == 0, z_prev, pltpu.roll(z, 1, 0))
    z_dn = jnp.where(rows == SEQ_BLOCK - 1, z_next, pltpu.roll(z, SEQ_BLOCK - 1, 0))
    yc = cw_ref[0:1, :] * z_up + cw_ref[1:2, :] * z + cw_ref[2:3, :] * z_dn
    oc_ref[...] = (cb_ref[...].astype(F32) * yc).astype(oc_ref.dtype)

    u = jax.nn.gelu(gu_ref[...].astype(F32))
    v = jax.nn.gelu(gv_ref[...].astype(F32))
    mu = jnp.mean(v, axis=-1, keepdims=True)
    var = jnp.mean(jnp.square(v - mu), axis=-1, keepdims=True)
    v = ((v - mu) * lax.rsqrt(var + EPS) * lng_ref[...] + lnb_ref[...]).astype(BF16)
    for ci in range(SEQ_BLOCK // CHUNK):
        rws = slice(ci * CHUNK, (ci + 1) * CHUNK)
        for g in range(CMLP_GROUPS):
            sl = slice(g * CMLP_GW, (g + 1) * CMLP_GW)
            mixed = _dot(ws_ref[g], v[rws, sl]) + bs_ref[g]
            og_ref[rws, sl] = (u[rws, sl] * mixed).astype(og_ref.dtype)


def _local_mixers(proj, o_f, o_b, y_f, y_b, wglu_b, conv_w, ln_g, ln_b, ws_b, bs_full,
                  layer, n_ctx):
    t = proj.shape[0]
    nb = t // SEQ_BLOCK
    nbc = n_ctx // SEQ_BLOCK
    per = SEQ_BLOCK // HALO_ROWS
    last = t // HALO_ROWS - 1
    depth = conv_w.shape[0]

    def sec(s):
        return pl.BlockSpec((SEQ_BLOCK, BRANCH_W), lambda c: (c, s))

    def prev(s):
        return pl.BlockSpec((HALO_ROWS, BRANCH_W), lambda c: (jnp.maximum(c * per - 1, 0), s))

    def nxt(s):
        return pl.BlockSpec((HALO_ROWS, BRANCH_W), lambda c: (jnp.minimum((c + 1) * per, last), s))

    def layer_spec(x):
        nd = x.ndim - 1
        return pl.BlockSpec((None,) + x.shape[1:], lambda c: (layer,) + (0,) * nd)

    ln_g3 = ln_g.reshape(depth, 1, BRANCH_W)
    ln_b3 = ln_b.reshape(depth, 1, BRANCH_W)
    blk = sec(0)
    out_shape = jax.ShapeDtypeStruct((t, BRANCH_W), BF16)
    return pl.pallas_call(
        functools.partial(_local_body, n_blocks=nb, n_ctx_blocks=nbc),
        grid=(nb,),
        in_specs=[blk, blk, sec(SEC_G), blk, blk, layer_spec(wglu_b),
                  sec(SEC_CX), sec(SEC_CB), sec(SEC_CC),
                  prev(SEC_CX), prev(SEC_CC), nxt(SEC_CX), nxt(SEC_CC), layer_spec(conv_w),
                  sec(SEC_GU), sec(SEC_GV), layer_spec(ln_g3), layer_spec(ln_b3),
                  layer_spec(ws_b), layer_spec(bs_full)],
        out_specs=[blk, blk, blk, blk],
        out_shape=[out_shape] * 4,
        compiler_params=_cparams(("parallel",)),
        name="local_mixers",
    )(o_f, o_b, proj, y_f, y_b, wglu_b,
      proj, proj, proj, proj, proj, proj, proj, conv_w,
      proj, proj, ln_g3, ln_b3, ws_b, bs_full)


def _final_norm_body(x_ref, g_ref, o_ref, *, tm):
    g = g_ref[...]

    def slab(r, carry):
        start = pl.multiple_of(r * BF16_ROWS, BF16_ROWS)
        x = x_ref[pl.ds(start, BF16_ROWS), :]
        inv = lax.rsqrt(jnp.mean(x * x, axis=-1, keepdims=True) + EPS)
        o_ref[pl.ds(start, BF16_ROWS), :] = x * inv * g
        return carry

    lax.fori_loop(0, tm // BF16_ROWS, slab, 0, unroll=2)


def _final_norm(x, g, n_ctx):
    t, d = x.shape
    tm = math.gcd(n_ctx, 512)
    skip = n_ctx // tm
    n_lat = t - n_ctx
    return pl.pallas_call(
        functools.partial(_final_norm_body, tm=tm),
        grid=(n_lat // tm,),
        in_specs=[pl.BlockSpec((tm, d), lambda i: (i + skip, 0)),
                  pl.BlockSpec((1, d), lambda i: (0, 0))],
        out_specs=pl.BlockSpec((tm, d), lambda i: (i, 0)),
        out_shape=jax.ShapeDtypeStruct((n_lat, d), F32),
        compiler_params=_cparams(("parallel",)),
        name="final_norm",
    )(x, g.reshape(1, d))


def _rope_tables(n_lat, n_ctx):
    rows = n_lat // GRID_W
    row = jnp.broadcast_to(jnp.arange(rows)[:, None], (rows, GRID_W)).reshape(-1).astype(F32)
    col = jnp.broadcast_to(jnp.arange(GRID_W)[None, :], (rows, GRID_W)).reshape(-1).astype(F32)
    nf = RET_DK // 4
    freqs = ROPE_BASE ** (-jnp.arange(nf, dtype=F32) / nf)
    ang = jnp.concatenate([row[:, None] * freqs, col[:, None] * freqs], axis=-1)
    cos, sin = jnp.cos(ang), jnp.sin(ang)
    cos2 = jnp.concatenate([cos, cos], axis=-1)
    sin2 = jnp.concatenate([-sin, sin], axis=-1)
    cos2 = jnp.concatenate([jnp.ones((n_ctx, RET_DK), F32), cos2], axis=0)
    sin2 = jnp.concatenate([jnp.zeros((n_ctx, RET_DK), F32), sin2], axis=0)
    return cos2, sin2


def kernel(x, c, ctx, c_ctx, ada_w, ada_b, norm1_g, norm2_g, w_in, ret_decay_logit, s5_a_re, s5_a_im, s5_b_re, s5_b_im, s5_c_re, s5_c_im, s5_d, s5_log_dt, s5_w_glu, conv_w, cmlp_ln_g, cmlp_ln_b, cmlp_ws, cmlp_bs, w_branch, w_merge, b_merge, w_out, ffn_w1, ffn_w3, ffn_w2, final_norm_g):
    batch, n_lat, d = x.shape
    assert batch == 1 and c.shape[0] == 1 and ctx.shape[0] == 1
    n_ctx = ctx.shape[1]
    assert n_ctx % SEQ_BLOCK == 0 and n_lat % SEQ_BLOCK == 0 and n_lat % GRID_W == 0
    depth = ada_w.shape[0]

    cos2, sin2 = _rope_tables(n_lat, n_ctx)
    ret_tables = _retention_tables(jax.nn.log_sigmoid(ret_decay_logit.astype(F32)))
    s5_wb, s5_a, s5_wc = _s5_prepare(s5_a_re, s5_a_im, s5_b_re, s5_b_im, s5_log_dt,
                                     s5_c_re, s5_c_im)
    wglu_b = s5_w_glu.astype(BF16)
    ws_b = cmlp_ws.astype(BF16)
    bs_full = jnp.broadcast_to(cmlp_bs[..., None], cmlp_bs.shape + (CMLP_GW,))

    mod_all = _adaln(c[0], c_ctx, ada_w, ada_b)
    mod_all = mod_all[:, 0:2].reshape(depth, 2, 6, d)
    xs = None

    for l in range(depth):
        sh1, sc1, g1, sh2, sc2, g2 = [mod_all[l, :, i] for i in range(6)]
        if l == 0:
            xs, h = _modulate_concat(ctx[0], x[0], norm1_g[l], sc1, sh1)
        else:
            h = _modulate(xs, norm1_g[l], sc1, sh1, n_ctx)
        proj = _matmul(h, w_in, l, 1280, 528, 2, BF16)

        o_f, o_b = _retention(proj, cos2, sin2, ret_tables, l, n_ctx)
        y_f, y_b = _s5(proj, s5_wb, s5_a, s5_wc, s5_d, l, n_ctx)
        outs = _local_mixers(proj, o_f, o_b, y_f, y_b, wglu_b, conv_w, cmlp_ln_g, cmlp_ln_b,
                             ws_b, bs_full, l, n_ctx)

        merged = _merge(h, outs, w_merge, b_merge, w_branch, l, 256, 1056, 2)
        xs, h2 = _matmul_residual_modulate(merged, w_out, l, xs, g1, norm2_g[l], sc2, sh2,
                                           n_ctx, 352)
        act = _ffn_up(h2, ffn_w1, ffn_w3, l, 512, 1056, 2)
        xs = _matmul_residual(act, ffn_w2, l, xs, g2, n_ctx, 512, 528, 2)

    return _final_norm(xs, final_norm_g, n_ctx)[None]
```

```python
import functools
import math

import jax
import jax.numpy as jnp
from jax import lax
from jax.experimental import pallas as pl
from jax.experimental.pallas import tpu as pltpu

F32 = jnp.float32
BF16 = jnp.bfloat16

D_MODEL = 2048
GRID_W = 64
BRANCH_W = 512
N_BRANCH = 4
N_SECTIONS = 10
RET_HEADS = 4
RET_DK = BRANCH_W // RET_HEADS
ROPE_BASE = 10000.0
S5_GROUP = 16
S5_GROUPS = BRANCH_W // S5_GROUP
S5_STATE = 64
CMLP_GROUPS = 4
CMLP_GW = BRANCH_W // CMLP_GROUPS
EPS = 1e-6

CHUNK = 128
SEQ_BLOCK = 256
LANES = 128
SUBLANES = 8
MXU_TILE = 256
BF16_ROWS = 16
HALO_ROWS = BF16_ROWS
S5_LANE_BLOCKS = BRANCH_W // LANES
S5_GROUPS_PER_BLOCK = LANES // S5_GROUP
S5_BLOCK_STATE = S5_GROUPS_PER_BLOCK * S5_STATE
S5_STATE_W = S5_GROUPS * S5_STATE
MIB = 1024 * 1024
VMEM_BUDGET_MIB = 56

SEC_Q, SEC_K, SEC_V, SEC_G, SEC_U, SEC_CX, SEC_CB, SEC_CC, SEC_GU, SEC_GV = range(N_SECTIONS)


def _cparams(semantics, vmem_mib=VMEM_BUDGET_MIB):
    return pltpu.CompilerParams(dimension_semantics=semantics,
                                vmem_limit_bytes=vmem_mib * MIB)


def _dot(a, b):
    return jnp.dot(a, b, preferred_element_type=F32)


def _row_tile(t, cap=1056):
    for tm in (1056, 1024, 768, 704, 528, 512, 384, 352, 256, 128):
        if tm <= cap and t % tm == 0:
            return tm
    raise ValueError(f"unsupported token count {t}")


def _is_ctx_rows(tile_idx, tm, n_ctx):
    rows = tile_idx * tm + lax.broadcasted_iota(jnp.int32, (tm, 1), 0)
    return rows < n_ctx


def _weight_spec(block, index_map, buffers=1):
    return pl.BlockSpec(block, index_map, pipeline_mode=pl.Buffered(buffers))


def _whole(x):
    nd = x.ndim
    return pl.BlockSpec(x.shape, lambda *_: (0,) * nd)


def _adaln_body(c_ref, w_ref, b_ref, o_ref, *, d, tn):
    reps = tn // LANES

    def kblock(kb, accs):
        rows = pl.ds(pl.multiple_of(kb * SUBLANES, SUBLANES), SUBLANES)
        w = w_ref[rows, :]
        out = []
        for r, acc in enumerate(accs):
            c = c_ref[r, rows, :]
            s = jnp.concatenate([c * jax.nn.sigmoid(c)] * reps, axis=1)
            out.append(acc + s * w)
        return tuple(out)

    zero = jnp.zeros((SUBLANES, tn), F32)
    accs = lax.fori_loop(0, d // SUBLANES, kblock, (zero, zero), unroll=8)
    rows = [jnp.sum(acc, axis=0, keepdims=True) + b_ref[...] for acc in accs]
    o_ref[...] = jnp.concatenate(rows + [jnp.zeros((SUBLANES - len(rows), tn), F32)], axis=0)


def _adaln(c_rep, ada_w, ada_b, layers):
    depth, d, n = ada_w.shape
    tn = 1024
    return pl.pallas_call(
        functools.partial(_adaln_body, d=d, tn=tn),
        grid=(layers, n // tn),
        in_specs=[
            pl.BlockSpec((2, d, LANES), lambda l, j: (0, 0, 0)),
            pl.BlockSpec((None, d, tn), lambda l, j: (l, 0, j)),
            pl.BlockSpec((None, 1, tn), lambda l, j: (l, 0, j)),
        ],
        out_specs=pl.BlockSpec((None, SUBLANES, tn), lambda l, j: (l, 0, j)),
        out_shape=jax.ShapeDtypeStruct((layers, SUBLANES, n), F32),
        compiler_params=_cparams(("parallel", "parallel")),
        name="adaln",
    )(c_rep, ada_w, ada_b.reshape(depth, 1, n))


def _modulate_body(x_ref, g_ref, sc_ref, sh_ref, o_ref, *, tm, n_ctx):
    base = pl.program_id(0) * tm
    g = g_ref[...]
    gain_lat = g * (1.0 + sc_ref[0:1, :])
    gain_ctx = g * (1.0 + sc_ref[1:2, :])
    shift_lat = sh_ref[0:1, :]
    shift_ctx = sh_ref[1:2, :]

    def slab(r, carry):
        start = pl.multiple_of(r * BF16_ROWS, BF16_ROWS)
        x = x_ref[pl.ds(start, BF16_ROWS), :]
        inv = lax.rsqrt(jnp.mean(x * x, axis=-1, keepdims=True) + EPS)
        is_ctx = (base + start) < n_ctx
        gain = jnp.where(is_ctx, gain_ctx, gain_lat)
        shift = jnp.where(is_ctx, shift_ctx, shift_lat)
        o_ref[pl.ds(start, BF16_ROWS), :] = (x * inv * gain + shift).astype(o_ref.dtype)
        return carry

    lax.fori_loop(0, tm // BF16_ROWS, slab, 0, unroll=3)


def _modulate(x, g, sc, sh, n_ctx):
    t, d = x.shape
    tm = _row_tile(t, cap=528)
    return pl.pallas_call(
        functools.partial(_modulate_body, tm=tm, n_ctx=n_ctx),
        grid=(t // tm,),
        in_specs=[
            pl.BlockSpec((tm, d), lambda i: (i, 0)),
            pl.BlockSpec((1, d), lambda i: (0, 0)),
            pl.BlockSpec((2, d), lambda i: (0, 0)),
            pl.BlockSpec((2, d), lambda i: (0, 0)),
        ],
        out_specs=pl.BlockSpec((tm, d), lambda i: (i, 0)),
        out_shape=jax.ShapeDtypeStruct((t, d), BF16),
        compiler_params=_cparams(("parallel",)),
        name="modulate",
    )(x, g.reshape(1, d), sc, sh)


def _modulate_concat_body(ctx_ref, lat_ref, g_ref, sc_ref, sh_ref, xo_ref, ho_ref, *, tm, n_ctx):
    i = pl.program_id(0)
    g = g_ref[...]

    def tile(src_ref, row):
        gain = g * (1.0 + sc_ref[row:row + 1, :])
        shift = sh_ref[row:row + 1, :]

        def slab(r, carry):
            rows = pl.ds(pl.multiple_of(r * BF16_ROWS, BF16_ROWS), BF16_ROWS)
            x = src_ref[rows, :]
            xo_ref[rows, :] = x
            inv = lax.rsqrt(jnp.mean(x * x, axis=-1, keepdims=True) + EPS)
            ho_ref[rows, :] = (x * inv * gain + shift).astype(ho_ref.dtype)
            return carry

        lax.fori_loop(0, tm // BF16_ROWS, slab, 0, unroll=4)

    @pl.when(i * tm < n_ctx)
    def _():
        tile(ctx_ref, 1)

    @pl.when(i * tm >= n_ctx)
    def _():
        tile(lat_ref, 0)


def _modulate_concat(ctx, lat, g, sc, sh):
    n_ctx, d = ctx.shape
    n_lat = lat.shape[0]
    t = n_ctx + n_lat
    tm = math.gcd(n_ctx, 512)
    nbc = n_ctx // tm
    out = pl.BlockSpec((tm, d), lambda i: (i, 0))
    return pl.pallas_call(
        functools.partial(_modulate_concat_body, tm=tm, n_ctx=n_ctx),
        grid=(t // tm,),
        in_specs=[
            pl.BlockSpec((tm, d), lambda i: (jnp.minimum(i, nbc - 1), 0)),
            pl.BlockSpec((tm, d), lambda i: (jnp.maximum(i - nbc, 0), 0)),
            pl.BlockSpec((1, d), lambda i: (0, 0)),
            pl.BlockSpec((2, d), lambda i: (0, 0)),
            pl.BlockSpec((2, d), lambda i: (0, 0)),
        ],
        out_specs=[out, out],
        out_shape=[jax.ShapeDtypeStruct((t, d), F32), jax.ShapeDtypeStruct((t, d), BF16)],
        compiler_params=_cparams(("parallel",)),
        name="modulate_concat",
    )(ctx, lat, g.reshape(1, d), sc, sh)


def _mm_body(x_ref, w_ref, o_ref, wb_ref):
    @pl.when(pl.program_id(1) == 0)
    def _():
        wb_ref[...] = w_ref[...].astype(BF16)

    o_ref[...] = _dot(x_ref[...], wb_ref[...]).astype(o_ref.dtype)


def _matmul(x, w_stack, layer, tn, tm_cap, w_buffers, out_dtype):
    t, k = x.shape
    n = w_stack.shape[-1]
    tm = _row_tile(t, tm_cap)
    return pl.pallas_call(
        _mm_body,
        grid=(n // tn, t // tm),
        in_specs=[
            pl.BlockSpec((tm, k), lambda j, i: (i, 0)),
            _weight_spec((None, k, tn), lambda j, i: (layer, 0, j), w_buffers),
        ],
        out_specs=pl.BlockSpec((tm, tn), lambda j, i: (i, j)),
        out_shape=jax.ShapeDtypeStruct((t, n), out_dtype),
        scratch_shapes=[pltpu.VMEM((k, tn), BF16)],
        compiler_params=_cparams(("arbitrary", "arbitrary")),
        name="matmul",
    )(x, w_stack)


def _mm_residual_body(a_ref, w_ref, x_ref, g_ref, o_ref, wb_ref, *, tm, n_ctx):
    @pl.when(pl.program_id(1) == 0)
    def _():
        wb_ref[...] = w_ref[...].astype(BF16)

    y = _dot(a_ref[...], wb_ref[...])
    is_ctx = _is_ctx_rows(pl.program_id(1), tm, n_ctx)
    gate = jnp.where(is_ctx, g_ref[1:2, :], g_ref[0:1, :])
    o_ref[...] = x_ref[...] + gate * y


def _matmul_residual(a, w_stack, layer, x, gate, n_ctx, tn, tm_cap, w_buffers):
    t, k = a.shape
    n = w_stack.shape[-1]
    tm = _row_tile(t, tm_cap)
    return pl.pallas_call(
        functools.partial(_mm_residual_body, tm=tm, n_ctx=n_ctx),
        grid=(n // tn, t // tm),
        in_specs=[
            pl.BlockSpec((tm, k), lambda j, i: (i, 0)),
            _weight_spec((None, k, tn), lambda j, i: (layer, 0, j), w_buffers),
            pl.BlockSpec((tm, tn), lambda j, i: (i, j)),
            pl.BlockSpec((2, tn), lambda j, i: (0, j)),
        ],
        out_specs=pl.BlockSpec((tm, tn), lambda j, i: (i, j)),
        out_shape=jax.ShapeDtypeStruct((t, n), F32),
        scratch_shapes=[pltpu.VMEM((k, tn), BF16)],
        compiler_params=_cparams(("arbitrary", "arbitrary")),
        name="matmul_residual",
    )(a, w_stack, x, gate)


def _mm_residual_modulate_body(a_ref, w_ref, x_ref, gate_ref, g_ref, sc_ref, sh_ref,
                               xo_ref, ho_ref, wb_ref, y0_ref, y1_ref, *, tm, n_ctx):
    i = pl.program_id(0)

    @pl.when(i == 0)
    def _():
        wb_ref[...] = w_ref[...].astype(BF16)
        y1_ref[...] = jnp.zeros_like(y1_ref)

    base = jnp.maximum(i - 1, 0) * tm
    g = g_ref[...]
    gain_lat = g * (1.0 + sc_ref[0:1, :])
    gain_ctx = g * (1.0 + sc_ref[1:2, :])

    def run(y_write, y_read):
        y_write[...] = _dot(a_ref[...], wb_ref[...])
        for r in range(tm // BF16_ROWS):
            rows = slice(r * BF16_ROWS, (r + 1) * BF16_ROWS)
            is_ctx = (base + r * BF16_ROWS) < n_ctx
            gate = jnp.where(is_ctx, gate_ref[1:2, :], gate_ref[0:1, :])
            x = x_ref[rows, :] + gate * y_read[rows, :]
            xo_ref[rows, :] = x
            inv = lax.rsqrt(jnp.mean(x * x, axis=-1, keepdims=True) + EPS)
            gain = jnp.where(is_ctx, gain_ctx, gain_lat)
            shift = jnp.where(is_ctx, sh_ref[1:2, :], sh_ref[0:1, :])
            ho_ref[rows, :] = (x * inv * gain + shift).astype(ho_ref.dtype)

    @pl.when(lax.rem(i, 2) == 0)
    def _():
        run(y0_ref, y1_ref)

    @pl.when(lax.rem(i, 2) == 1)
    def _():
        run(y1_ref, y0_ref)


def _matmul_residual_modulate(a, w_stack, layer, x, gate, g, sc, sh, n_ctx, tm_cap):
    t, k = a.shape
    n = w_stack.shape[-1]
    tm = _row_tile(t, tm_cap)
    nt = t // tm
    lagged = pl.BlockSpec((tm, n), lambda i: (jnp.maximum(i - 1, 0), 0))
    vec2 = pl.BlockSpec((2, n), lambda i: (0, 0))
    product = pltpu.VMEM((tm, n), F32)
    return pl.pallas_call(
        functools.partial(_mm_residual_modulate_body, tm=tm, n_ctx=n_ctx),
        grid=(nt + 1,),
        in_specs=[
            pl.BlockSpec((tm, k), lambda i: (jnp.minimum(i, nt - 1), 0)),
            _weight_spec((None, k, n), lambda i: (layer, 0, 0)),
            lagged, vec2,
            pl.BlockSpec((1, n), lambda i: (0, 0)), vec2, vec2,
        ],
        out_specs=[lagged, lagged],
        out_shape=[jax.ShapeDtypeStruct((t, n), F32), jax.ShapeDtypeStruct((t, n), BF16)],
        scratch_shapes=[pltpu.VMEM((k, n), BF16), product, product],
        compiler_params=_cparams(("arbitrary",)),
        name="matmul_residual_modulate",
    )(a, w_stack, x, gate, g.reshape(1, n), sc, sh)


def _ffn_up_body(h_ref, w1_ref, w3_ref, o_ref, w1b_ref, w3b_ref):
    @pl.when(pl.program_id(1) == 0)
    def _():
        w1b_ref[...] = w1_ref[...].astype(BF16)
        w3b_ref[...] = w3_ref[...].astype(BF16)

    h = h_ref[...]
    for c0 in range(0, o_ref.shape[1], MXU_TILE):
        cols = slice(c0, c0 + MXU_TILE)
        a = _dot(h, w1b_ref[:, cols])
        b = _dot(h, w3b_ref[:, cols])
        o_ref[:, cols] = (a * jax.nn.sigmoid(a) * b).astype(o_ref.dtype)


def _ffn_up(h, w1_stack, w3_stack, layer, tn, tm_cap, w_buffers):
    t, k = h.shape
    n = w1_stack.shape[-1]
    tm = _row_tile(t, tm_cap)
    wspec = _weight_spec((None, k, tn), lambda j, i: (layer, 0, j), w_buffers)
    return pl.pallas_call(
        _ffn_up_body,
        grid=(n // tn, t // tm),
        in_specs=[pl.BlockSpec((tm, k), lambda j, i: (i, 0)), wspec, wspec],
        out_specs=pl.BlockSpec((tm, tn), lambda j, i: (i, j)),
        out_shape=jax.ShapeDtypeStruct((t, n), BF16),
        scratch_shapes=[pltpu.VMEM((k, tn), BF16), pltpu.VMEM((k, tn), BF16)],
        compiler_params=_cparams(("arbitrary", "arbitrary")),
        name="ffn_up",
    )(h, w1_stack, w3_stack)


def _merge_body(*refs):
    nb = N_BRANCH
    h_ref = refs[0]
    o_refs = refs[1:1 + nb]
    wm_refs = refs[1 + nb:1 + 2 * nb]
    bm_refs = refs[1 + 2 * nb:1 + 3 * nb]
    wb_refs = refs[1 + 3 * nb:1 + 4 * nb]
    out_ref, wmb_ref, wbb_ref = refs[1 + 4 * nb:]

    @pl.when(pl.program_id(1) == 0)
    def _():
        for k in range(nb):
            wmb_ref[k] = wm_refs[k][...].astype(BF16)
            wbb_ref[k] = wb_refs[k][...].astype(BF16)

    h = h_ref[...]
    acc = None
    for k in range(nb):
        gate = jax.nn.sigmoid(_dot(h, wmb_ref[k]) + bm_refs[k][...])
        term = gate * _dot(o_refs[k][...], wbb_ref[k])
        acc = term if acc is None else acc + term
    out_ref[...] = acc.astype(out_ref.dtype)


def _merge(h, outs, w_merge, b_merge, w_branch, layer, tn, tm_cap, w_buffers):
    t, d = h.shape
    bw = outs[0].shape[1]
    depth = w_merge.shape[0]
    tm = _row_tile(t, tm_cap)
    nt = d // tn
    b_merge3 = b_merge.reshape(depth, 1, N_BRANCH * d)
    in_specs = [pl.BlockSpec((tm, d), lambda j, i: (i, 0))]
    in_specs += [pl.BlockSpec((tm, bw), lambda j, i: (i, 0)) for _ in range(N_BRANCH)]
    in_specs += [_weight_spec((None, d, tn), lambda j, i, k=k: (layer, 0, k * nt + j), w_buffers)
                 for k in range(N_BRANCH)]
    in_specs += [pl.BlockSpec((None, 1, tn), lambda j, i, k=k: (layer, 0, k * nt + j))
                 for k in range(N_BRANCH)]
    in_specs += [_weight_spec((None, None, bw, tn), lambda j, i, k=k: (layer, k, 0, j), w_buffers)
                 for k in range(N_BRANCH)]
    return pl.pallas_call(
        _merge_body,
        grid=(nt, t // tm),
        in_specs=in_specs,
        out_specs=pl.BlockSpec((tm, tn), lambda j, i: (i, j)),
        out_shape=jax.ShapeDtypeStruct((t, d), BF16),
        scratch_shapes=[pltpu.VMEM((N_BRANCH, d, tn), BF16),
                        pltpu.VMEM((N_BRANCH, bw, tn), BF16)],
        compiler_params=_cparams(("arbitrary", "arbitrary")),
        name="merge",
    )(h, *outs, *([w_merge] * N_BRANCH), *([b_merge3] * N_BRANCH), *([w_branch] * N_BRANCH))


def _block_order(step, n_blocks, n_ctx_blocks, reverse):
    if not reverse:
        return step
    return jnp.where(step < n_ctx_blocks, n_ctx_blocks - 1 - step,
                     n_blocks + n_ctx_blocks - 1 - step)


def _retention_body(qf_ref, kf_ref, vf_ref, cosf_ref, sinf_ref,
                    qb_ref, kb_ref, vb_ref, cosb_ref, sinb_ref, tab_ref,
                    of_ref, ob_ref, s_ref):
    @pl.when(pl.program_id(0) == 0)
    def _():
        s_ref[...] = jnp.zeros_like(s_ref)

    chunks = SEQ_BLOCK // CHUNK
    dirs = ((qf_ref, kf_ref, vf_ref, cosf_ref, sinf_ref, of_ref, range(chunks)),
            (qb_ref, kb_ref, vb_ref, cosb_ref, sinb_ref, ob_ref, range(chunks - 1, -1, -1)))
    for d, (q_ref, k_ref, v_ref, cos_ref, sin_ref, o_ref, order) in enumerate(dirs):
        for h in range(RET_HEADS):
            sl = slice(h * RET_DK, (h + 1) * RET_DK)
            dmat = tab_ref[d, 0, h]
            q_decay = tab_ref[d, 1, h]
            k_decay = tab_ref[d, 2, h]
            g_chunk = tab_ref[d, 3, h]
            state = s_ref[d, h]
            for ci in order:
                rows = slice(ci * CHUNK, (ci + 1) * CHUNK)
                cos = cos_ref[rows, :]
                sin = sin_ref[rows, :]
                q = q_ref[rows, sl].astype(F32)
                k = k_ref[rows, sl].astype(F32)
                v = v_ref[rows, sl].astype(BF16)
                qr = q * cos + pltpu.roll(q, RET_DK // 2, 1) * sin
                kr = k * cos + pltpu.roll(k, RET_DK // 2, 1) * sin
                scores = lax.dot_general(qr.astype(BF16), kr.astype(BF16),
                                         (((1,), (1,)), ((), ())),
                                         preferred_element_type=F32) * dmat
                intra = _dot(scores.astype(BF16), v)
                cross = _dot((qr * q_decay).astype(BF16), state.astype(BF16))
                kv = _dot((kr * k_decay).T.astype(BF16), v)
                o_ref[rows, sl] = (intra + cross).astype(o_ref.dtype)
                state = g_chunk * state + kv
            s_ref[d, h] = state


def _retention_tables(log_gamma):
    pos = jnp.arange(CHUNK, dtype=F32)
    i = pos[:, None]
    j = pos[None, :]
    full = log_gamma.shape[:1] + (RET_HEADS, CHUNK, RET_DK)
    scale = RET_DK ** -0.5

    def one_dir(lg, diff, mask, q_pow, k_pow):
        dmat = jnp.where(mask, jnp.exp(lg[:, :, None, None] * jnp.maximum(diff, 0.0)), 0.0) * scale
        q_decay = jnp.broadcast_to(jnp.exp(lg[:, :, None] * q_pow)[..., None], full) * scale
        k_decay = jnp.broadcast_to(jnp.exp(lg[:, :, None] * k_pow)[..., None], full)
        g_chunk = jnp.broadcast_to(jnp.exp(lg * CHUNK)[:, :, None, None], full)
        return jnp.stack([dmat, q_decay, k_decay, g_chunk], axis=1)

    fwd = one_dir(log_gamma[:, 0], i - j, (i - j) >= 0, pos + 1.0, CHUNK - 1.0 - pos)
    bwd = one_dir(log_gamma[:, 1], j - i, (j - i) > 0, CHUNK - pos, pos)
    return jnp.stack([fwd, bwd], axis=1)


def _retention(proj, cos2, sin2, tables, layer, n_ctx):
    t = proj.shape[0]
    nb = t // SEQ_BLOCK
    nbc = n_ctx // SEQ_BLOCK

    def specs(reverse):
        def blk(st):
            return _block_order(st, nb, nbc, reverse)
        sec = [pl.BlockSpec((SEQ_BLOCK, BRANCH_W), lambda st, s=s: (blk(st), s))
               for s in (SEC_Q, SEC_K, SEC_V)]
        row = pl.BlockSpec((SEQ_BLOCK, RET_DK), lambda st: (blk(st), 0))
        out = pl.BlockSpec((SEQ_BLOCK, BRANCH_W), lambda st: (blk(st), 0))
        return sec + [row, row], out

    in_f, out_f = specs(False)
    in_b, out_b = specs(True)
    tab_spec = pl.BlockSpec((None, 2, 4, RET_HEADS, CHUNK, RET_DK),
                            lambda st: (layer, 0, 0, 0, 0, 0))
    shape = jax.ShapeDtypeStruct((t, BRANCH_W), BF16)
    return pl.pallas_call(
        _retention_body,
        grid=(nb,),
        in_specs=in_f + in_b + [tab_spec],
        out_specs=[out_f, out_b],
        out_shape=[shape, shape],
        scratch_shapes=[pltpu.VMEM((2, RET_HEADS, RET_DK, RET_DK), F32)],
        compiler_params=_cparams(("arbitrary",)),
        name="retention",
    )(proj, proj, proj, cos2, sin2, proj, proj, proj, cos2, sin2, tables)


def _s5_body(uf_ref, ub_ref, wb_ref, a_ref, wc_ref, d_ref, yf_ref, yb_ref,
             x_ref, bu_ref, st_ref):
    @pl.when(pl.program_id(0) == 0)
    def _():
        x_ref[...] = jnp.zeros_like(x_ref)

    for d, u_ref in enumerate((uf_ref, ub_ref)):
        ub = u_ref[...].astype(BF16)
        for r in range(S5_LANE_BLOCKS):
            bu = _dot(ub[:, r * LANES:(r + 1) * LANES], wb_ref[d, r])
            cols = slice(r * S5_BLOCK_STATE, (r + 1) * S5_BLOCK_STATE)
            bu_ref[2 * d, :, cols] = bu[:, :S5_BLOCK_STATE]
            bu_ref[2 * d + 1, :, cols] = bu[:, S5_BLOCK_STATE:]

    af_re, af_im = a_ref[0, 0:1, :], a_ref[0, 1:2, :]
    ab_re, ab_im = a_ref[1, 0:1, :], a_ref[1, 1:2, :]

    def step(i, carry):
        f_re, f_im, b_re, b_im = carry
        tf = pl.ds(i, 1)
        tb = pl.ds(SEQ_BLOCK - 1 - i, 1)
        nf_re = af_re * f_re - af_im * f_im + bu_ref[0, tf, :]
        nf_im = af_re * f_im + af_im * f_re + bu_ref[1, tf, :]
        nb_re = ab_re * b_re - ab_im * b_im + bu_ref[2, tb, :]
        nb_im = ab_re * b_im + ab_im * b_re + bu_ref[3, tb, :]
        st_ref[0, tf, :] = nf_re
        st_ref[1, tf, :] = nf_im
        st_ref[2, tb, :] = nb_re
        st_ref[3, tb, :] = nb_im
        return nf_re, nf_im, nb_re, nb_im

    carry = lax.fori_loop(0, SEQ_BLOCK, step,
                          (x_ref[0:1, :], x_ref[1:2, :], x_ref[2:3, :], x_ref[3:4, :]),
                          unroll=8)
    for n, val in enumerate(carry):
        x_ref[n:n + 1, :] = val

    def readout(d):
        ys = []
        for r in range(S5_LANE_BLOCKS):
            cols = slice(r * S5_BLOCK_STATE, (r + 1) * S5_BLOCK_STATE)
            st = jnp.concatenate([st_ref[2 * d, :, cols], st_ref[2 * d + 1, :, cols]],
                                 axis=1).astype(BF16)
            ys.append(_dot(st, wc_ref[r]))
        return jnp.concatenate(ys, axis=1)

    yf_ref[...] = (readout(0) + d_ref[...] * uf_ref[...].astype(F32)).astype(yf_ref.dtype)
    yb_ref[...] = readout(1).astype(yb_ref.dtype)


def _s5_prepare(a_re, a_im, b_re, b_im, log_dt, c_re, c_im):
    depth = a_re.shape[0]
    a_re = a_re[:, None]
    a_im = a_im[:, None]
    dt = jnp.exp(log_dt)[..., None]
    mag = jnp.exp(dt * a_re)
    ang = dt * a_im
    ab_re, ab_im = mag * jnp.cos(ang), mag * jnp.sin(ang)
    nr, ni = ab_re - 1.0, ab_im
    den = a_re * a_re + a_im * a_im
    f_re = (nr * a_re + ni * a_im) / den
    f_im = (ni * a_re - nr * a_im) / den
    bb_re = f_re[..., None] * b_re - f_im[..., None] * b_im
    bb_im = f_re[..., None] * b_im + f_im[..., None] * b_re
    gpb = S5_GROUPS_PER_BLOCK
    row_group = jnp.arange(LANES) // S5_GROUP
    col_group = jnp.arange(S5_BLOCK_STATE) // S5_STATE

    def in_map(bb):
        rows = jnp.swapaxes(bb.reshape(depth, 2, S5_LANE_BLOCKS, gpb, S5_STATE, S5_GROUP), -1, -2)
        rows = rows.reshape(depth, 2, S5_LANE_BLOCKS, LANES, S5_STATE)
        mask = (row_group[:, None] == col_group[None, :]).astype(F32)
        return jnp.tile(rows, (1, 1, 1, 1, gpb)) * mask

    def out_map(c):
        rows = jnp.swapaxes(c.reshape(depth, S5_LANE_BLOCKS, gpb, S5_GROUP, S5_STATE), -1, -2)
        rows = rows.reshape(depth, S5_LANE_BLOCKS, S5_BLOCK_STATE, S5_GROUP)
        mask = (col_group[:, None] == row_group[None, :]).astype(F32)
        return jnp.tile(rows, (1, 1, 1, gpb)) * mask

    wb = jnp.concatenate([in_map(bb_re), in_map(bb_im)], axis=-1).astype(BF16)
    a = jnp.stack([ab_re.reshape(depth, 2, S5_STATE_W), ab_im.reshape(depth, 2, S5_STATE_W)],
                  axis=2)
    wc = jnp.concatenate([out_map(c_re), out_map(-c_im)], axis=-2).astype(BF16)
    return wb, a, wc


def _s5(proj, wb, a, wc, s5_d, layer, n_ctx):
    t = proj.shape[0]
    nb = t // SEQ_BLOCK
    nbc = n_ctx // SEQ_BLOCK
    depth = s5_d.shape[0]

    def io(reverse, sec):
        return pl.BlockSpec((SEQ_BLOCK, BRANCH_W),
                            lambda st: (_block_order(st, nb, nbc, reverse), sec))

    shape = jax.ShapeDtypeStruct((t, BRANCH_W), BF16)
    planes = pltpu.VMEM((4, SEQ_BLOCK, S5_STATE_W), F32)
    return pl.pallas_call(
        _s5_body,
        grid=(nb,),
        in_specs=[io(False, SEC_U), io(True, SEC_U),
                  pl.BlockSpec((None,) + wb.shape[1:], lambda st: (layer, 0, 0, 0, 0)),
                  pl.BlockSpec((None,) + a.shape[1:], lambda st: (layer, 0, 0, 0)),
                  pl.BlockSpec((None,) + wc.shape[1:], lambda st: (layer, 0, 0, 0)),
                  pl.BlockSpec((None, 1, BRANCH_W), lambda st: (layer, 0, 0))],
        out_specs=[io(False, 0), io(True, 0)],
        out_shape=[shape, shape],
        scratch_shapes=[pltpu.VMEM((4, S5_STATE_W), F32), planes, planes],
        compiler_params=_cparams(("arbitrary",)),
        name="s5",
    )(proj, proj, wb, a, wc, s5_d.reshape(depth, 1, BRANCH_W))


def _local_body(of_ref, ob_ref, g_ref, yf_ref, yb_ref, wglu_ref,
                cx_ref, cb_ref, cc_ref, cxp_ref, ccp_ref, cxn_ref, ccn_ref, cw_ref,
                gu_ref, gv_ref, lng_ref, lnb_ref, ws_ref, bs_ref,
                oret_ref, os5_ref, oc_ref, og_ref, *, n_blocks, n_ctx_blocks):
    for h in range(RET_HEADS):
        sl = slice(h * RET_DK, (h + 1) * RET_DK)
        o = of_ref[:, sl].astype(F32) + ob_ref[:, sl].astype(F32)
        mu = jnp.mean(o, axis=-1, keepdims=True)
        var = jnp.mean(jnp.square(o - mu), axis=-1, keepdims=True)
        o = (o - mu) * lax.rsqrt(var + EPS)
        g = g_ref[:, sl].astype(F32)
        oret_ref[:, sl] = (o * (g * jax.nn.sigmoid(g))).astype(oret_ref.dtype)

    y = jax.nn.gelu(yf_ref[...].astype(F32) + yb_ref[...].astype(F32))
    os5_ref[...] = (y * jax.nn.sigmoid(_dot(y.astype(BF16), wglu_ref[...]))).astype(os5_ref.dtype)

    c = pl.program_id(0)
    seq_start = jnp.logical_or(c == 0, c == n_ctx_blocks)
    seq_end = jnp.logical_or(c == n_ctx_blocks - 1, c == n_blocks - 1)
    z = cc_ref[...].astype(F32) * cx_ref[...].astype(F32)
    z_prev = (ccp_ref[...].astype(F32) * cxp_ref[...].astype(F32))[HALO_ROWS - 1:HALO_ROWS, :]
    z_next = (ccn_ref[...].astype(F32) * cxn_ref[...].astype(F32))[0:1, :]
    z_prev = jnp.where(seq_start, 0.0, z_prev)
    z_next = jnp.where(seq_end, 0.0, z_next)
    rows = lax.broadcasted_iota(jnp.int32, (SEQ_BLOCK, 1), 0)
    z_up = jnp.where(rows == 0, z_prev, pltpu.roll(z, 1, 0))
    z_dn = jnp.where(rows == SEQ_BLOCK - 1, z_next, pltpu.roll(z, SEQ_BLOCK - 1, 0))
    yc = cw_ref[0:1, :] * z_up + cw_ref[1:2, :] * z + cw_ref[2:3, :] * z_dn
    oc_ref[...] = (cb_ref[...].astype(F32) * yc).astype(oc_ref.dtype)

    u = jax.nn.gelu(gu_ref[...].astype(F32))
    v = jax.nn.gelu(gv_ref[...].astype(F32))
    mu = jnp.mean(v, axis=-1, keepdims=True)
    var = jnp.mean(jnp.square(v - mu), axis=-1, keepdims=True)
    v = ((v - mu) * lax.rsqrt(var + EPS) * lng_ref[...] + lnb_ref[...]).astype(BF16)
    for ci in range(SEQ_BLOCK // CHUNK):
        rws = slice(ci * CHUNK, (ci + 1) * CHUNK)
        for g in range(CMLP_GROUPS):
            sl = slice(g * CMLP_GW, (g + 1) * CMLP_GW)
            mixed = _dot(ws_ref[g], v[rws, sl]) + bs_ref[g]
            og_ref[rws, sl] = (u[rws, sl] * mixed).astype(og_ref.dtype)


def _local_mixers(proj, o_f, o_b, y_f, y_b, wglu_b, conv_w, ln_g, ln_b, ws_b, bs_full,
                  layer, n_ctx):
    t = proj.shape[0]
    nb = t // SEQ_BLOCK
    nbc = n_ctx // SEQ_BLOCK
    per = SEQ_BLOCK // HALO_ROWS
    last = t // HALO_ROWS - 1
    depth = conv_w.shape[0]

    def sec(s):
        return pl.BlockSpec((SEQ_BLOCK, BRANCH_W), lambda c: (c, s))

    def prev(s):
        return pl.BlockSpec((HALO_ROWS, BRANCH_W), lambda c: (jnp.maximum(c * per - 1, 0), s))

    def nxt(s):
        return pl.BlockSpec((HALO_ROWS, BRANCH_W), lambda c: (jnp.minimum((c + 1) * per, last), s))

    def layer_spec(x):
        nd = x.ndim - 1
        return pl.BlockSpec((None,) + x.shape[1:], lambda c: (layer,) + (0,) * nd)

    ln_g3 = ln_g.reshape(depth, 1, BRANCH_W)
    ln_b3 = ln_b.reshape(depth, 1, BRANCH_W)
    blk = sec(0)
    out_shape = jax.ShapeDtypeStruct((t, BRANCH_W), BF16)
    return pl.pallas_call(
        functools.partial(_local_body, n_blocks=nb, n_ctx_blocks=nbc),
        grid=(nb,),
        in_specs=[blk, blk, sec(SEC_G), blk, blk, layer_spec(wglu_b),
                  sec(SEC_CX), sec(SEC_CB), sec(SEC_CC),
                  prev(SEC_CX), prev(SEC_CC), nxt(SEC_CX), nxt(SEC_CC), layer_spec(conv_w),
                  sec(SEC_GU), sec(SEC_GV), layer_spec(ln_g3), layer_spec(ln_b3),
                  layer_spec(ws_b), layer_spec(bs_full)],
        out_specs=[blk, blk, blk, blk],
        out_shape=[out_shape] * 4,
        compiler_params=_cparams(("parallel",)),
        name="local_mixers",
    )(o_f, o_b, proj, y_f, y_b, wglu_b,
      proj, proj, proj, proj, proj, proj, proj, conv_w,
      proj, proj, ln_g3, ln_b3, ws_b, bs_full)


def _final_norm_body(x_ref, g_ref, o_ref, *, tm):
    g = g_ref[...]

    def slab(r, carry):
        start = pl.multiple_of(r * BF16_ROWS, BF16_ROWS)
        x = x_ref[pl.ds(start, BF16_ROWS), :]
        inv = lax.rsqrt(jnp.mean(x * x, axis=-1, keepdims=True) + EPS)
        o_ref[pl.ds(start, BF16_ROWS), :] = x * inv * g
        return carry

    lax.fori_loop(0, tm // BF16_ROWS, slab, 0, unroll=2)


def _final_norm(x, g, n_ctx):
    t, d = x.shape
    tm = math.gcd(n_ctx, 512)
    skip = n_ctx // tm
    n_lat = t - n_ctx
    return pl.pallas_call(
        functools.partial(_final_norm_body, tm=tm),
        grid=(n_lat // tm,),
        in_specs=[pl.BlockSpec((tm, d), lambda i: (i + skip, 0)),
                  pl.BlockSpec((1, d), lambda i: (0, 0))],
        out_specs=pl.BlockSpec((tm, d), lambda i: (i, 0)),
        out_shape=jax.ShapeDtypeStruct((n_lat, d), F32),
        compiler_params=_cparams(("parallel",)),
        name="final_norm",
    )(x, g.reshape(1, d))


def _rope_tables(n_lat, n_ctx):
    rows = n_lat // GRID_W
    row = jnp.broadcast_to(jnp.arange(rows)[:, None], (rows, GRID_W)).reshape(-1).astype(F32)
    col = jnp.broadcast_to(jnp.arange(GRID_W)[None, :], (rows, GRID_W)).reshape(-1).astype(F32)
    nf = RET_DK // 4
    freqs = ROPE_BASE ** (-jnp.arange(nf, dtype=F32) / nf)
    ang = jnp.concatenate([row[:, None] * freqs, col[:, None] * freqs], axis=-1)
    cos, sin = jnp.cos(ang), jnp.sin(ang)
    cos2 = jnp.concatenate([cos, cos], axis=-1)
    sin2 = jnp.concatenate([-sin, sin], axis=-1)
    cos2 = jnp.concatenate([jnp.ones((n_ctx, RET_DK), F32), cos2], axis=0)
    sin2 = jnp.concatenate([jnp.zeros((n_ctx, RET_DK), F32), sin2], axis=0)
    return cos2, sin2


def kernel(x, c, ctx, c_ctx, ada_w, ada_b, norm1_g, norm2_g, w_in, ret_decay_logit, s5_a_re, s5_a_im, s5_b_re, s5_b_im, s5_c_re, s5_c_im, s5_d, s5_log_dt, s5_w_glu, conv_w, cmlp_ln_g, cmlp_ln_b, cmlp_ws, cmlp_bs, w_branch, w_merge, b_merge, w_out, ffn_w1, ffn_w3, ffn_w2, final_norm_g):
    batch, n_lat, d = x.shape
    assert batch == 1 and c.shape[0] == 1 and ctx.shape[0] == 1
    n_ctx = ctx.shape[1]
    assert n_ctx % SEQ_BLOCK == 0 and n_lat % SEQ_BLOCK == 0 and n_lat % GRID_W == 0
    depth = ada_w.shape[0]

    cos2, sin2 = _rope_tables(n_lat, n_ctx)
    ret_tables = _retention_tables(jax.nn.log_sigmoid(ret_decay_logit.astype(F32)))
    s5_wb, s5_a, s5_wc = _s5_prepare(s5_a_re, s5_a_im, s5_b_re, s5_b_im, s5_log_dt,
                                     s5_c_re, s5_c_im)
    wglu_b = s5_w_glu.astype(BF16)
    ws_b = cmlp_ws.astype(BF16)
    bs_full = jnp.broadcast_to(cmlp_bs[..., None], cmlp_bs.shape + (CMLP_GW,))

    c_rep = jnp.broadcast_to(jnp.stack([c[0], c_ctx])[:, :, None], (2, d, LANES))
    mod_all = _adaln(c_rep, ada_w, ada_b, depth)
    mod_all = mod_all[:, 0:2].reshape(depth, 2, 6, d)
    xs = None

    for l in range(depth):
        sh1, sc1, g1, sh2, sc2, g2 = [mod_all[l, :, i] for i in range(6)]
        if l == 0:
            xs, h = _modulate_concat(ctx[0], x[0], norm1_g[l], sc1, sh1)
        else:
            h = _modulate(xs, norm1_g[l], sc1, sh1, n_ctx)
        proj = _matmul(h, w_in, l, 1280, 528, 2, BF16)

        o_f, o_b = _retention(proj, cos2, sin2, ret_tables, l, n_ctx)
        y_f, y_b = _s5(proj, s5_wb, s5_a, s5_wc, s5_d, l, n_ctx)
        outs = _local_mixers(proj, o_f, o_b, y_f, y_b, wglu_b, conv_w, cmlp_ln_g, cmlp_ln_b,
                             ws_b, bs_full, l, n_ctx)

        merged = _merge(h, outs, w_merge, b_merge, w_branch, l, 256, 1056, 2)
        xs, h2 = _matmul_residual_modulate(merged, w_out, l, xs, g1, norm2_g[l], sc2, sh2,
                                           n_ctx, 352)
        act = _ffn_up(h2, ffn_w1, ffn_w3, l, 512, 1056, 2)
        xs = _matmul_residual(act, ffn_w2, l, xs, g2, n_ctx, 1024, 352, 1)

    return _final_norm(xs, final_norm_g, n_ctx)[None]
```

```python
import functools
import math

import jax
import jax.numpy as jnp
from jax import lax
from jax.experimental import pallas as pl
from jax.experimental.pallas import tpu as pltpu

F32 = jnp.float32
BF16 = jnp.bfloat16

D_MODEL = 2048
GRID_W = 64
BRANCH_W = 512
N_BRANCH = 4
N_SECTIONS = 10
RET_HEADS = 4
RET_DK = BRANCH_W // RET_HEADS
ROPE_BASE = 10000.0
S5_GROUP = 16
S5_GROUPS = BRANCH_W // S5_GROUP
S5_STATE = 64
CMLP_GROUPS = 4
CMLP_GW = BRANCH_W // CMLP_GROUPS
EPS = 1e-6

CHUNK = 128
SEQ_BLOCK = 256
LANES = 128
SUBLANES = 8
MXU_TILE = 256
BF16_ROWS = 16
HALO_ROWS = BF16_ROWS
S5_LANE_BLOCKS = BRANCH_W // LANES
S5_GROUPS_PER_BLOCK = LANES // S5_GROUP
S5_BLOCK_STATE = S5_GROUPS_PER_BLOCK * S5_STATE
S5_STATE_W = S5_GROUPS * S5_STATE
MIB = 1024 * 1024
VMEM_BUDGET_MIB = 56

SEC_Q, SEC_K, SEC_V, SEC_G, SEC_U, SEC_CX, SEC_CB, SEC_CC, SEC_GU, SEC_GV = range(N_SECTIONS)


def _cparams(semantics, vmem_mib=VMEM_BUDGET_MIB):
    return pltpu.CompilerParams(dimension_semantics=semantics,
                                vmem_limit_bytes=vmem_mib * MIB)


def _dot(a, b):
    return jnp.dot(a, b, preferred_element_type=F32)


def _row_tile(t, cap=1056):
    for tm in (1056, 1024, 768, 704, 528, 512, 384, 352, 256, 128):
        if tm <= cap and t % tm == 0:
            return tm
    raise ValueError(f"unsupported token count {t}")


def _is_ctx_rows(tile_idx, tm, n_ctx):
    rows = tile_idx * tm + lax.broadcasted_iota(jnp.int32, (tm, 1), 0)
    return rows < n_ctx


def _weight_spec(block, index_map, buffers=1):
    return pl.BlockSpec(block, index_map, pipeline_mode=pl.Buffered(buffers))


def _whole(x):
    nd = x.ndim
    return pl.BlockSpec(x.shape, lambda *_: (0,) * nd)


def _adaln_body(c_ref, w_ref, b_ref, o_ref, *, d, tn):
    reps = tn // LANES

    def kblock(kb, accs):
        rows = pl.ds(pl.multiple_of(kb * SUBLANES, SUBLANES), SUBLANES)
        w = w_ref[rows, :]
        out = []
        for r, acc in enumerate(accs):
            c = c_ref[r, rows, :]
            s = jnp.concatenate([c * jax.nn.sigmoid(c)] * reps, axis=1)
            out.append(acc + s * w)
        return tuple(out)

    zero = jnp.zeros((SUBLANES, tn), F32)
    accs = lax.fori_loop(0, d // SUBLANES, kblock, (zero, zero), unroll=8)
    rows = [jnp.sum(acc, axis=0, keepdims=True) + b_ref[...] for acc in accs]
    o_ref[...] = jnp.concatenate(rows + [jnp.zeros((SUBLANES - len(rows), tn), F32)], axis=0)


def _adaln(c_rep, ada_w, ada_b, layers):
    depth, d, n = ada_w.shape
    tn = 1024
    return pl.pallas_call(
        functools.partial(_adaln_body, d=d, tn=tn),
        grid=(layers, n // tn),
        in_specs=[
            pl.BlockSpec((2, d, LANES), lambda l, j: (0, 0, 0)),
            pl.BlockSpec((None, d, tn), lambda l, j: (l, 0, j)),
            pl.BlockSpec((None, 1, tn), lambda l, j: (l, 0, j)),
        ],
        out_specs=pl.BlockSpec((None, SUBLANES, tn), lambda l, j: (l, 0, j)),
        out_shape=jax.ShapeDtypeStruct((layers, SUBLANES, n), F32),
        compiler_params=_cparams(("parallel", "parallel")),
        name="adaln",
    )(c_rep, ada_w, ada_b.reshape(depth, 1, n))


def _modulate_body(x_ref, g_ref, sc_ref, sh_ref, o_ref, *, tm, n_ctx):
    base = pl.program_id(0) * tm
    g = g_ref[...]
    gain_lat = g * (1.0 + sc_ref[0:1, :])
    gain_ctx = g * (1.0 + sc_ref[1:2, :])
    shift_lat = sh_ref[0:1, :]
    shift_ctx = sh_ref[1:2, :]

    def slab(r, carry):
        start = pl.multiple_of(r * BF16_ROWS, BF16_ROWS)
        x = x_ref[pl.ds(start, BF16_ROWS), :]
        inv = lax.rsqrt(jnp.mean(x * x, axis=-1, keepdims=True) + EPS)
        is_ctx = (base + start) < n_ctx
        gain = jnp.where(is_ctx, gain_ctx, gain_lat)
        shift = jnp.where(is_ctx, shift_ctx, shift_lat)
        o_ref[pl.ds(start, BF16_ROWS), :] = (x * inv * gain + shift).astype(o_ref.dtype)
        return carry

    lax.fori_loop(0, tm // BF16_ROWS, slab, 0, unroll=11)


def _modulate(x, g, sc, sh, n_ctx):
    t, d = x.shape
    tm = _row_tile(t, cap=528)
    return pl.pallas_call(
        functools.partial(_modulate_body, tm=tm, n_ctx=n_ctx),
        grid=(t // tm,),
        in_specs=[
            pl.BlockSpec((tm, d), lambda i: (i, 0)),
            pl.BlockSpec((1, d), lambda i: (0, 0)),
            pl.BlockSpec((2, d), lambda i: (0, 0)),
            pl.BlockSpec((2, d), lambda i: (0, 0)),
        ],
        out_specs=pl.BlockSpec((tm, d), lambda i: (i, 0)),
        out_shape=jax.ShapeDtypeStruct((t, d), BF16),
        compiler_params=_cparams(("parallel",)),
        name="modulate",
    )(x, g.reshape(1, d), sc, sh)


def _modulate_concat_body(ctx_ref, lat_ref, g_ref, sc_ref, sh_ref, xo_ref, ho_ref, *, tm, n_ctx):
    i = pl.program_id(0)
    g = g_ref[...]

    def tile(src_ref, row):
        gain = g * (1.0 + sc_ref[row:row + 1, :])
        shift = sh_ref[row:row + 1, :]

        def slab(r, carry):
            rows = pl.ds(pl.multiple_of(r * BF16_ROWS, BF16_ROWS), BF16_ROWS)
            x = src_ref[rows, :]
            xo_ref[rows, :] = x
            inv = lax.rsqrt(jnp.mean(x * x, axis=-1, keepdims=True) + EPS)
            ho_ref[rows, :] = (x * inv * gain + shift).astype(ho_ref.dtype)
            return carry

        lax.fori_loop(0, tm // BF16_ROWS, slab, 0, unroll=8)

    @pl.when(i * tm < n_ctx)
    def _():
        tile(ctx_ref, 1)

    @pl.when(i * tm >= n_ctx)
    def _():
        tile(lat_ref, 0)


def _modulate_concat(ctx, lat, g, sc, sh):
    n_ctx, d = ctx.shape
    n_lat = lat.shape[0]
    t = n_ctx + n_lat
    tm = math.gcd(n_ctx, 512)
    nbc = n_ctx // tm
    out = pl.BlockSpec((tm, d), lambda i: (i, 0))
    return pl.pallas_call(
        functools.partial(_modulate_concat_body, tm=tm, n_ctx=n_ctx),
        grid=(t // tm,),
        in_specs=[
            pl.BlockSpec((tm, d), lambda i: (jnp.minimum(i, nbc - 1), 0)),
            pl.BlockSpec((tm, d), lambda i: (jnp.maximum(i - nbc, 0), 0)),
            pl.BlockSpec((1, d), lambda i: (0, 0)),
            pl.BlockSpec((2, d), lambda i: (0, 0)),
            pl.BlockSpec((2, d), lambda i: (0, 0)),
        ],
        out_specs=[out, out],
        out_shape=[jax.ShapeDtypeStruct((t, d), F32), jax.ShapeDtypeStruct((t, d), BF16)],
        compiler_params=_cparams(("parallel",)),
        name="modulate_concat",
    )(ctx, lat, g.reshape(1, d), sc, sh)


def _mm_rope_body(x_ref, w_ref, cos_ref, sin_ref, o_ref, wb_ref, *, rope_cols):
    @pl.when(pl.program_id(1) == 0)
    def _():
        wb_ref[...] = w_ref[...].astype(BF16)

    @pl.when(pl.program_id(0) == 0)
    def _():
        y = _dot(x_ref[...], wb_ref[...])
        cos = cos_ref[...]
        sin = sin_ref[...]
        for c0 in range(0, o_ref.shape[1], RET_DK):
            blk = y[:, c0:c0 + RET_DK]
            if c0 < rope_cols:
                blk = blk * cos + pltpu.roll(blk, RET_DK // 2, 1) * sin
            o_ref[:, c0:c0 + RET_DK] = blk.astype(o_ref.dtype)

    @pl.when(pl.program_id(0) != 0)
    def _():
        o_ref[...] = _dot(x_ref[...], wb_ref[...]).astype(o_ref.dtype)


def _matmul_rope(x, w_stack, layer, cos2, sin2, tn, tm_cap, w_buffers, out_dtype):
    t, k = x.shape
    n = w_stack.shape[-1]
    tm = _row_tile(t, tm_cap)
    rope_cols = 2 * BRANCH_W
    assert rope_cols <= tn and SEC_Q == 0 and SEC_K == 1
    table = pl.BlockSpec((tm, RET_DK), lambda j, i: (i, 0))
    return pl.pallas_call(
        functools.partial(_mm_rope_body, rope_cols=rope_cols),
        grid=(n // tn, t // tm),
        in_specs=[
            pl.BlockSpec((tm, k), lambda j, i: (i, 0)),
            _weight_spec((None, k, tn), lambda j, i: (layer, 0, j), w_buffers),
            table, table,
        ],
        out_specs=pl.BlockSpec((tm, tn), lambda j, i: (i, j)),
        out_shape=jax.ShapeDtypeStruct((t, n), out_dtype),
        scratch_shapes=[pltpu.VMEM((k, tn), BF16)],
        compiler_params=_cparams(("arbitrary", "arbitrary")),
        name="matmul_rope",
    )(x, w_stack, cos2, sin2)


def _mm_residual_body(a_ref, w_ref, x_ref, g_ref, o_ref, wb_ref, *, tm, n_ctx):
    @pl.when(pl.program_id(1) == 0)
    def _():
        wb_ref[...] = w_ref[...].astype(BF16)

    y = _dot(a_ref[...], wb_ref[...])
    is_ctx = _is_ctx_rows(pl.program_id(1), tm, n_ctx)
    gate = jnp.where(is_ctx, g_ref[1:2, :], g_ref[0:1, :])
    o_ref[...] = x_ref[...] + gate * y


def _matmul_residual(a, w_stack, layer, x, gate, n_ctx, tn, tm_cap, w_buffers):
    t, k = a.shape
    n = w_stack.shape[-1]
    tm = _row_tile(t, tm_cap)
    return pl.pallas_call(
        functools.partial(_mm_residual_body, tm=tm, n_ctx=n_ctx),
        grid=(n // tn, t // tm),
        in_specs=[
            pl.BlockSpec((tm, k), lambda j, i: (i, 0)),
            _weight_spec((None, k, tn), lambda j, i: (layer, 0, j), w_buffers),
            pl.BlockSpec((tm, tn), lambda j, i: (i, j)),
            pl.BlockSpec((2, tn), lambda j, i: (0, j)),
        ],
        out_specs=pl.BlockSpec((tm, tn), lambda j, i: (i, j)),
        out_shape=jax.ShapeDtypeStruct((t, n), F32),
        scratch_shapes=[pltpu.VMEM((k, tn), BF16)],
        compiler_params=_cparams(("arbitrary", "arbitrary")),
        name="matmul_residual",
    )(a, w_stack, x, gate)


def _mm_residual_modulate_body(a_ref, w_ref, x_ref, gate_ref, g_ref, sc_ref, sh_ref,
                               xo_ref, ho_ref, wb_ref, y0_ref, y1_ref, *, tm, n_ctx):
    i = pl.program_id(0)

    @pl.when(i == 0)
    def _():
        wb_ref[...] = w_ref[...].astype(BF16)
        y1_ref[...] = jnp.zeros_like(y1_ref)

    base = jnp.maximum(i - 1, 0) * tm
    g = g_ref[...]
    gain_lat = g * (1.0 + sc_ref[0:1, :])
    gain_ctx = g * (1.0 + sc_ref[1:2, :])

    def run(y_write, y_read):
        y_write[...] = _dot(a_ref[...], wb_ref[...])
        for r in range(tm // BF16_ROWS):
            rows = slice(r * BF16_ROWS, (r + 1) * BF16_ROWS)
            is_ctx = (base + r * BF16_ROWS) < n_ctx
            gate = jnp.where(is_ctx, gate_ref[1:2, :], gate_ref[0:1, :])
            x = x_ref[rows, :] + gate * y_read[rows, :]
            xo_ref[rows, :] = x
            inv = lax.rsqrt(jnp.mean(x * x, axis=-1, keepdims=True) + EPS)
            gain = jnp.where(is_ctx, gain_ctx, gain_lat)
            shift = jnp.where(is_ctx, sh_ref[1:2, :], sh_ref[0:1, :])
            ho_ref[rows, :] = (x * inv * gain + shift).astype(ho_ref.dtype)

    @pl.when(lax.rem(i, 2) == 0)
    def _():
        run(y0_ref, y1_ref)

    @pl.when(lax.rem(i, 2) == 1)
    def _():
        run(y1_ref, y0_ref)


def _matmul_residual_modulate(a, w_stack, layer, x, gate, g, sc, sh, n_ctx, tm_cap):
    t, k = a.shape
    n = w_stack.shape[-1]
    tm = _row_tile(t, tm_cap)
    nt = t // tm
    lagged = pl.BlockSpec((tm, n), lambda i: (jnp.maximum(i - 1, 0), 0))
    vec2 = pl.BlockSpec((2, n), lambda i: (0, 0))
    product = pltpu.VMEM((tm, n), F32)
    return pl.pallas_call(
        functools.partial(_mm_residual_modulate_body, tm=tm, n_ctx=n_ctx),
        grid=(nt + 1,),
        in_specs=[
            pl.BlockSpec((tm, k), lambda i: (jnp.minimum(i, nt - 1), 0)),
            _weight_spec((None, k, n), lambda i: (layer, 0, 0)),
            lagged, vec2,
            pl.BlockSpec((1, n), lambda i: (0, 0)), vec2, vec2,
        ],
        out_specs=[lagged, lagged],
        out_shape=[jax.ShapeDtypeStruct((t, n), F32), jax.ShapeDtypeStruct((t, n), BF16)],
        scratch_shapes=[pltpu.VMEM((k, n), BF16), product, product],
        compiler_params=_cparams(("arbitrary",)),
        name="matmul_residual_modulate",
    )(a, w_stack, x, gate, g.reshape(1, n), sc, sh)


def _ffn_up_body(h_ref, w1_ref, w3_ref, o_ref, w1b_ref, w3b_ref):
    @pl.when(pl.program_id(1) == 0)
    def _():
        w1b_ref[...] = w1_ref[...].astype(BF16)
        w3b_ref[...] = w3_ref[...].astype(BF16)

    h = h_ref[...]
    for c0 in range(0, o_ref.shape[1], MXU_TILE):
        cols = slice(c0, c0 + MXU_TILE)
        a = _dot(h, w1b_ref[:, cols])
        b = _dot(h, w3b_ref[:, cols])
        o_ref[:, cols] = (a * jax.nn.sigmoid(a) * b).astype(o_ref.dtype)


def _ffn_up(h, w1_stack, w3_stack, layer, tn, tm_cap, w_buffers):
    t, k = h.shape
    n = w1_stack.shape[-1]
    tm = _row_tile(t, tm_cap)
    wspec = _weight_spec((None, k, tn), lambda j, i: (layer, 0, j), w_buffers)
    return pl.pallas_call(
        _ffn_up_body,
        grid=(n // tn, t // tm),
        in_specs=[pl.BlockSpec((tm, k), lambda j, i: (i, 0)), wspec, wspec],
        out_specs=pl.BlockSpec((tm, tn), lambda j, i: (i, j)),
        out_shape=jax.ShapeDtypeStruct((t, n), BF16),
        scratch_shapes=[pltpu.VMEM((k, tn), BF16), pltpu.VMEM((k, tn), BF16)],
        compiler_params=_cparams(("arbitrary", "arbitrary")),
        name="ffn_up",
    )(h, w1_stack, w3_stack)


def _merge_body(*refs):
    nb = N_BRANCH
    h_ref = refs[0]
    o_refs = refs[1:1 + nb]
    wm_refs = refs[1 + nb:1 + 2 * nb]
    bm_refs = refs[1 + 2 * nb:1 + 3 * nb]
    wb_refs = refs[1 + 3 * nb:1 + 4 * nb]
    out_ref, wmb_ref, wbb_ref = refs[1 + 4 * nb:]

    @pl.when(pl.program_id(1) == 0)
    def _():
        for k in range(nb):
            wmb_ref[k] = wm_refs[k][...].astype(BF16)
            wbb_ref[k] = wb_refs[k][...].astype(BF16)

    h = h_ref[...]
    acc = None
    for k in range(nb):
        gate = jax.nn.sigmoid(_dot(h, wmb_ref[k]) + bm_refs[k][...])
        term = gate * _dot(o_refs[k][...], wbb_ref[k])
        acc = term if acc is None else acc + term
    out_ref[...] = acc.astype(out_ref.dtype)


def _merge(h, outs, w_merge, b_merge, w_branch, layer, tn, tm_cap, w_buffers):
    t, d = h.shape
    bw = outs[0].shape[1]
    depth = w_merge.shape[0]
    tm = _row_tile(t, tm_cap)
    nt = d // tn
    b_merge3 = b_merge.reshape(depth, 1, N_BRANCH * d)
    in_specs = [pl.BlockSpec((tm, d), lambda j, i: (i, 0))]
    in_specs += [pl.BlockSpec((tm, bw), lambda j, i: (i, 0)) for _ in range(N_BRANCH)]
    in_specs += [_weight_spec((None, d, tn), lambda j, i, k=k: (layer, 0, k * nt + j), w_buffers)
                 for k in range(N_BRANCH)]
    in_specs += [pl.BlockSpec((None, 1, tn), lambda j, i, k=k: (layer, 0, k * nt + j))
                 for k in range(N_BRANCH)]
    in_specs += [_weight_spec((None, None, bw, tn), lambda j, i, k=k: (layer, k, 0, j), w_buffers)
                 for k in range(N_BRANCH)]
    return pl.pallas_call(
        _merge_body,
        grid=(nt, t // tm),
        in_specs=in_specs,
        out_specs=pl.BlockSpec((tm, tn), lambda j, i: (i, j)),
        out_shape=jax.ShapeDtypeStruct((t, d), BF16),
        scratch_shapes=[pltpu.VMEM((N_BRANCH, d, tn), BF16),
                        pltpu.VMEM((N_BRANCH, bw, tn), BF16)],
        compiler_params=_cparams(("arbitrary", "arbitrary")),
        name="merge",
    )(h, *outs, *([w_merge] * N_BRANCH), *([b_merge3] * N_BRANCH), *([w_branch] * N_BRANCH))


def _block_order(step, n_blocks, n_ctx_blocks, reverse):
    if not reverse:
        return step
    return jnp.where(step < n_ctx_blocks, n_ctx_blocks - 1 - step,
                     n_blocks + n_ctx_blocks - 1 - step)


def _retention_body(qf_ref, kf_ref, vf_ref, qb_ref, kb_ref, vb_ref, tab_ref,
                    of_ref, ob_ref, s_ref):
    @pl.when(pl.program_id(0) == 0)
    def _():
        s_ref[...] = jnp.zeros_like(s_ref)

    chunks = SEQ_BLOCK // CHUNK
    dirs = ((qf_ref, kf_ref, vf_ref, of_ref, range(chunks)),
            (qb_ref, kb_ref, vb_ref, ob_ref, range(chunks - 1, -1, -1)))
    for d, (q_ref, k_ref, v_ref, o_ref, order) in enumerate(dirs):
        for h in range(RET_HEADS):
            sl = slice(h * RET_DK, (h + 1) * RET_DK)
            dmat = tab_ref[d, 0, h]
            q_decay = tab_ref[d, 1, h]
            k_decay = tab_ref[d, 2, h]
            g_chunk = tab_ref[d, 3, h]
            state = s_ref[d, h]
            for ci in order:
                rows = slice(ci * CHUNK, (ci + 1) * CHUNK)
                q = q_ref[rows, sl].astype(BF16)
                k = k_ref[rows, sl].astype(BF16)
                v = v_ref[rows, sl].astype(BF16)
                scores = lax.dot_general(q, k, (((1,), (1,)), ((), ())),
                                         preferred_element_type=F32) * dmat
                intra = _dot(scores.astype(BF16), v)
                cross = _dot((q.astype(F32) * q_decay).astype(BF16), state.astype(BF16))
                kv = _dot((k.astype(F32) * k_decay).T.astype(BF16), v)
                o_ref[rows, sl] = (intra + cross).astype(o_ref.dtype)
                state = g_chunk * state + kv
            s_ref[d, h] = state


def _retention_tables(log_gamma):
    pos = jnp.arange(CHUNK, dtype=F32)
    i = pos[:, None]
    j = pos[None, :]
    full = log_gamma.shape[:1] + (RET_HEADS, CHUNK, RET_DK)
    scale = RET_DK ** -0.5

    def one_dir(lg, diff, mask, q_pow, k_pow):
        dmat = jnp.where(mask, jnp.exp(lg[:, :, None, None] * jnp.maximum(diff, 0.0)), 0.0) * scale
        q_decay = jnp.broadcast_to(jnp.exp(lg[:, :, None] * q_pow)[..., None], full) * scale
        k_decay = jnp.broadcast_to(jnp.exp(lg[:, :, None] * k_pow)[..., None], full)
        g_chunk = jnp.broadcast_to(jnp.exp(lg * CHUNK)[:, :, None, None], full)
        return jnp.stack([dmat, q_decay, k_decay, g_chunk], axis=1)

    fwd = one_dir(log_gamma[:, 0], i - j, (i - j) >= 0, pos + 1.0, CHUNK - 1.0 - pos)
    bwd = one_dir(log_gamma[:, 1], j - i, (j - i) > 0, CHUNK - pos, pos)
    return jnp.stack([fwd, bwd], axis=1)


def _retention(proj, tables, layer, n_ctx):
    t = proj.shape[0]
    nb = t // SEQ_BLOCK
    nbc = n_ctx // SEQ_BLOCK

    def specs(reverse):
        def blk(st):
            return _block_order(st, nb, nbc, reverse)
        sec = [pl.BlockSpec((SEQ_BLOCK, BRANCH_W), lambda st, s=s: (blk(st), s))
               for s in (SEC_Q, SEC_K, SEC_V)]
        out = pl.BlockSpec((SEQ_BLOCK, BRANCH_W), lambda st: (blk(st), 0))
        return sec, out

    in_f, out_f = specs(False)
    in_b, out_b = specs(True)
    tab_spec = pl.BlockSpec((None, 2, 4, RET_HEADS, CHUNK, RET_DK),
                            lambda st: (layer, 0, 0, 0, 0, 0))
    shape = jax.ShapeDtypeStruct((t, BRANCH_W), BF16)
    return pl.pallas_call(
        _retention_body,
        grid=(nb,),
        in_specs=in_f + in_b + [tab_spec],
        out_specs=[out_f, out_b],
        out_shape=[shape, shape],
        scratch_shapes=[pltpu.VMEM((2, RET_HEADS, RET_DK, RET_DK), F32)],
        compiler_params=_cparams(("arbitrary",)),
        name="retention",
    )(proj, proj, proj, proj, proj, proj, tables)


def _s5_body(uf_ref, ub_ref, wb_ref, a_ref, wc_ref, d_ref, yf_ref, yb_ref,
             x_ref, bu_ref, st_ref):
    @pl.when(pl.program_id(0) == 0)
    def _():
        x_ref[...] = jnp.zeros_like(x_ref)

    for d, u_ref in enumerate((uf_ref, ub_ref)):
        ub = u_ref[...].astype(BF16)
        for r in range(S5_LANE_BLOCKS):
            bu = _dot(ub[:, r * LANES:(r + 1) * LANES], wb_ref[d, r])
            cols = slice(r * S5_BLOCK_STATE, (r + 1) * S5_BLOCK_STATE)
            bu_ref[2 * d, :, cols] = bu[:, :S5_BLOCK_STATE]
            bu_ref[2 * d + 1, :, cols] = bu[:, S5_BLOCK_STATE:]

    af_re, af_im = a_ref[0, 0:1, :], a_ref[0, 1:2, :]
    ab_re, ab_im = a_ref[1, 0:1, :], a_ref[1, 1:2, :]

    def step(i, carry):
        f_re, f_im, b_re, b_im = carry
        tf = pl.ds(i, 1)
        tb = pl.ds(SEQ_BLOCK - 1 - i, 1)
        nf_re = af_re * f_re - af_im * f_im + bu_ref[0, tf, :]
        nf_im = af_re * f_im + af_im * f_re + bu_ref[1, tf, :]
        nb_re = ab_re * b_re - ab_im * b_im + bu_ref[2, tb, :]
        nb_im = ab_re * b_im + ab_im * b_re + bu_ref[3, tb, :]
        st_ref[0, tf, :] = nf_re
        st_ref[1, tf, :] = nf_im
        st_ref[2, tb, :] = nb_re
        st_ref[3, tb, :] = nb_im
        return nf_re, nf_im, nb_re, nb_im

    carry = lax.fori_loop(0, SEQ_BLOCK, step,
                          (x_ref[0:1, :], x_ref[1:2, :], x_ref[2:3, :], x_ref[3:4, :]),
                          unroll=8)
    for n, val in enumerate(carry):
        x_ref[n:n + 1, :] = val

    def readout(d):
        ys = []
        for r in range(S5_LANE_BLOCKS):
            cols = slice(r * S5_BLOCK_STATE, (r + 1) * S5_BLOCK_STATE)
            st = jnp.concatenate([st_ref[2 * d, :, cols], st_ref[2 * d + 1, :, cols]],
                                 axis=1).astype(BF16)
            ys.append(_dot(st, wc_ref[r]))
        return jnp.concatenate(ys, axis=1)

    yf_ref[...] = (readout(0) + d_ref[...] * uf_ref[...].astype(F32)).astype(yf_ref.dtype)
    yb_ref[...] = readout(1).astype(yb_ref.dtype)


def _s5_prepare(a_re, a_im, b_re, b_im, log_dt, c_re, c_im):
    depth = a_re.shape[0]
    a_re = a_re[:, None]
    a_im = a_im[:, None]
    dt = jnp.exp(log_dt)[..., None]
    mag = jnp.exp(dt * a_re)
    ang = dt * a_im
    ab_re, ab_im = mag * jnp.cos(ang), mag * jnp.sin(ang)
    nr, ni = ab_re - 1.0, ab_im
    den = a_re * a_re + a_im * a_im
    f_re = (nr * a_re + ni * a_im) / den
    f_im = (ni * a_re - nr * a_im) / den
    bb_re = f_re[..., None] * b_re - f_im[..., None] * b_im
    bb_im = f_re[..., None] * b_im + f_im[..., None] * b_re
    gpb = S5_GROUPS_PER_BLOCK
    row_group = jnp.arange(LANES) // S5_GROUP
    col_group = jnp.arange(S5_BLOCK_STATE) // S5_STATE

    def in_map(bb):
        rows = jnp.swapaxes(bb.reshape(depth, 2, S5_LANE_BLOCKS, gpb, S5_STATE, S5_GROUP), -1, -2)
        rows = rows.reshape(depth, 2, S5_LANE_BLOCKS, LANES, S5_STATE)
        mask = (row_group[:, None] == col_group[None, :]).astype(F32)
        return jnp.tile(rows, (1, 1, 1, 1, gpb)) * mask

    def out_map(c):
        rows = jnp.swapaxes(c.reshape(depth, S5_LANE_BLOCKS, gpb, S5_GROUP, S5_STATE), -1, -2)
        rows = rows.reshape(depth, S5_LANE_BLOCKS, S5_BLOCK_STATE, S5_GROUP)
        mask = (col_group[:, None] == row_group[None, :]).astype(F32)
        return jnp.tile(rows, (1, 1, 1, gpb)) * mask

    wb = jnp.concatenate([in_map(bb_re), in_map(bb_im)], axis=-1).astype(BF16)
    a = jnp.stack([ab_re.reshape(depth, 2, S5_STATE_W), ab_im.reshape(depth, 2, S5_STATE_W)],
                  axis=2)
    wc = jnp.concatenate([out_map(c_re), out_map(-c_im)], axis=-2).astype(BF16)
    return wb, a, wc


def _s5(proj, wb, a, wc, s5_d, layer, n_ctx):
    t = proj.shape[0]
    nb = t // SEQ_BLOCK
    nbc = n_ctx // SEQ_BLOCK
    depth = s5_d.shape[0]

    def io(reverse, sec):
        return pl.BlockSpec((SEQ_BLOCK, BRANCH_W),
                            lambda st: (_block_order(st, nb, nbc, reverse), sec))

    shape = jax.ShapeDtypeStruct((t, BRANCH_W), BF16)
    planes = pltpu.VMEM((4, SEQ_BLOCK, S5_STATE_W), F32)
    return pl.pallas_call(
        _s5_body,
        grid=(nb,),
        in_specs=[io(False, SEC_U), io(True, SEC_U),
                  pl.BlockSpec((None,) + wb.shape[1:], lambda st: (layer, 0, 0, 0, 0)),
                  pl.BlockSpec((None,) + a.shape[1:], lambda st: (layer, 0, 0, 0)),
                  pl.BlockSpec((None,) + wc.shape[1:], lambda st: (layer, 0, 0, 0)),
                  pl.BlockSpec((None, 1, BRANCH_W), lambda st: (layer, 0, 0))],
        out_specs=[io(False, 0), io(True, 0)],
        out_shape=[shape, shape],
        scratch_shapes=[pltpu.VMEM((4, S5_STATE_W), F32), planes, planes],
        compiler_params=_cparams(("arbitrary",)),
        name="s5",
    )(proj, proj, wb, a, wc, s5_d.reshape(depth, 1, BRANCH_W))


def _local_body(of_ref, ob_ref, g_ref, yf_ref, yb_ref, wglu_ref,
                cx_ref, cb_ref, cc_ref, cxp_ref, ccp_ref, cxn_ref, ccn_ref, cw_ref,
                gu_ref, gv_ref, lng_ref, lnb_ref, ws_ref, bs_ref,
                oret_ref, os5_ref, oc_ref, og_ref, *, n_blocks, n_ctx_blocks):
    for h in range(RET_HEADS):
        sl = slice(h * RET_DK, (h + 1) * RET_DK)
        o = of_ref[:, sl].astype(F32) + ob_ref[:, sl].astype(F32)
        mu = jnp.mean(o, axis=-1, keepdims=True)
        var = jnp.mean(jnp.square(o - mu), axis=-1, keepdims=True)
        o = (o - mu) * lax.rsqrt(var + EPS)
        g = g_ref[:, sl].astype(F32)
        oret_ref[:, sl] = (o * (g * jax.nn.sigmoid(g))).astype(oret_ref.dtype)

    y = jax.nn.gelu(yf_ref[...].astype(F32) + yb_ref[...].astype(F32))
    os5_ref[...] = (y * jax.nn.sigmoid(_dot(y.astype(BF16), wglu_ref[...]))).astype(os5_ref.dtype)

    c = pl.program_id(0)
    seq_start = jnp.logical_or(c == 0, c == n_ctx_blocks)
    seq_end = jnp.logical_or(c == n_ctx_blocks - 1, c == n_blocks - 1)
    z = cc_ref[...].astype(F32) * cx_ref[...].astype(F32)
    z_prev = (ccp_ref[...].astype(F32) * cxp_ref[...].astype(F32))[HALO_ROWS - 1:HALO_ROWS, :]
    z_next = (ccn_ref[...].astype(F32) * cxn_ref[...].astype(F32))[0:1, :]
    z_prev = jnp.where(seq_start, 0.0, z_prev)
    z_next = jnp.where(seq_end, 0.0, z_next)
    rows = lax.broadcasted_iota(jnp.int32, (SEQ_BLOCK, 1), 0)
    z_up = jnp.where(rows == 0, z_prev, pltpu.roll(z, 1, 0))
    z_dn = jnp.where(rows == SEQ_BLOCK - 1, z_next, pltpu.roll(z, SEQ_BLOCK - 1, 0))
    yc = cw_ref[0:1, :] * z_up + cw_ref[1:2, :] * z + cw_ref[2:3, :] * z_dn
    oc_ref[...] = (cb_ref[...].astype(F32) * yc).astype(oc_ref.dtype)

    u = jax.nn.gelu(gu_ref[...].astype(F32))
    v = jax.nn.gelu(gv_ref[...].astype(F32))
    mu = jnp.mean(v, axis=-1, keepdims=True)
    var = jnp.mean(jnp.square(v - mu), axis=-1, keepdims=True)
    v = ((v - mu) * lax.rsqrt(var + EPS) * lng_ref[...] + lnb_ref[...]).astype(BF16)
    for ci in range(SEQ_BLOCK // CHUNK):
        rws = slice(ci * CHUNK, (ci + 1) * CHUNK)
        for g in range(CMLP_GROUPS):
            sl = slice(g * CMLP_GW, (g + 1) * CMLP_GW)
            mixed = _dot(ws_ref[g], v[rws, sl]) + bs_ref[g]
            og_ref[rws, sl] = (u[rws, sl] * mixed).astype(og_ref.dtype)


def _local_mixers(proj, o_f, o_b, y_f, y_b, wglu_b, conv_w, ln_g, ln_b, ws_b, bs_full,
                  layer, n_ctx):
    t = proj.shape[0]
    nb = t // SEQ_BLOCK
    nbc = n_ctx // SEQ_BLOCK
    per = SEQ_BLOCK // HALO_ROWS
    last = t // HALO_ROWS - 1
    depth = conv_w.shape[0]

    def sec(s):
        return pl.BlockSpec((SEQ_BLOCK, BRANCH_W), lambda c: (c, s))

    def prev(s):
        return pl.BlockSpec((HALO_ROWS, BRANCH_W), lambda c: (jnp.maximum(c * per - 1, 0), s))

    def nxt(s):
        return pl.BlockSpec((HALO_ROWS, BRANCH_W), lambda c: (jnp.minimum((c + 1) * per, last), s))

    def layer_spec(x):
        nd = x.ndim - 1
        return pl.BlockSpec((None,) + x.shape[1:], lambda c: (layer,) + (0,) * nd)

    ln_g3 = ln_g.reshape(depth, 1, BRANCH_W)
    ln_b3 = ln_b.reshape(depth, 1, BRANCH_W)
    blk = sec(0)
    out_shape = jax.ShapeDtypeStruct((t, BRANCH_W), BF16)
    return pl.pallas_call(
        functools.partial(_local_body, n_blocks=nb, n_ctx_blocks=nbc),
        grid=(nb,),
        in_specs=[blk, blk, sec(SEC_G), blk, blk, layer_spec(wglu_b),
                  sec(SEC_CX), sec(SEC_CB), sec(SEC_CC),
                  prev(SEC_CX), prev(SEC_CC), nxt(SEC_CX), nxt(SEC_CC), layer_spec(conv_w),
                  sec(SEC_GU), sec(SEC_GV), layer_spec(ln_g3), layer_spec(ln_b3),
                  layer_spec(ws_b), layer_spec(bs_full)],
        out_specs=[blk, blk, blk, blk],
        out_shape=[out_shape] * 4,
        compiler_params=_cparams(("parallel",)),
        name="local_mixers",
    )(o_f, o_b, proj, y_f, y_b, wglu_b,
      proj, proj, proj, proj, proj, proj, proj, conv_w,
      proj, proj, ln_g3, ln_b3, ws_b, bs_full)


def _final_norm_body(x_ref, g_ref, o_ref, *, tm):
    g = g_ref[...]

    def slab(r, carry):
        start = pl.multiple_of(r * BF16_ROWS, BF16_ROWS)
        x = x_ref[pl.ds(start, BF16_ROWS), :]
        inv = lax.rsqrt(jnp.mean(x * x, axis=-1, keepdims=True) + EPS)
        o_ref[pl.ds(start, BF16_ROWS), :] = x * inv * g
        return carry

    lax.fori_loop(0, tm // BF16_ROWS, slab, 0, unroll=8)


def _final_norm(x, g, n_ctx):
    t, d = x.shape
    tm = math.gcd(n_ctx, 512)
    skip = n_ctx // tm
    n_lat = t - n_ctx
    return pl.pallas_call(
        functools.partial(_final_norm_body, tm=tm),
        grid=(n_lat // tm,),
        in_specs=[pl.BlockSpec((tm, d), lambda i: (i + skip, 0)),
                  pl.BlockSpec((1, d), lambda i: (0, 0))],
        out_specs=pl.BlockSpec((tm, d), lambda i: (i, 0)),
        out_shape=jax.ShapeDtypeStruct((n_lat, d), F32),
        compiler_params=_cparams(("parallel",)),
        name="final_norm",
    )(x, g.reshape(1, d))


def _rope_tables(n_lat, n_ctx):
    rows = n_lat // GRID_W
    row = jnp.broadcast_to(jnp.arange(rows)[:, None], (rows, GRID_W)).reshape(-1).astype(F32)
    col = jnp.broadcast_to(jnp.arange(GRID_W)[None, :], (rows, GRID_W)).reshape(-1).astype(F32)
    nf = RET_DK // 4
    freqs = ROPE_BASE ** (-jnp.arange(nf, dtype=F32) / nf)
    ang = jnp.concatenate([row[:, None] * freqs, col[:, None] * freqs], axis=-1)
    cos, sin = jnp.cos(ang), jnp.sin(ang)
    cos2 = jnp.concatenate([cos, cos], axis=-1)
    sin2 = jnp.concatenate([-sin, sin], axis=-1)
    cos2 = jnp.concatenate([jnp.ones((n_ctx, RET_DK), F32), cos2], axis=0)
    sin2 = jnp.concatenate([jnp.zeros((n_ctx, RET_DK), F32), sin2], axis=0)
    return cos2, sin2


def kernel(x, c, ctx, c_ctx, ada_w, ada_b, norm1_g, norm2_g, w_in, ret_decay_logit, s5_a_re, s5_a_im, s5_b_re, s5_b_im, s5_c_re, s5_c_im, s5_d, s5_log_dt, s5_w_glu, conv_w, cmlp_ln_g, cmlp_ln_b, cmlp_ws, cmlp_bs, w_branch, w_merge, b_merge, w_out, ffn_w1, ffn_w3, ffn_w2, final_norm_g):
    batch, n_lat, d = x.shape
    assert batch == 1 and c.shape[0] == 1 and ctx.shape[0] == 1
    n_ctx = ctx.shape[1]
    assert n_ctx % SEQ_BLOCK == 0 and n_lat % SEQ_BLOCK == 0 and n_lat % GRID_W == 0
    depth = ada_w.shape[0]

    cos2, sin2 = _rope_tables(n_lat, n_ctx)
    ret_tables = _retention_tables(jax.nn.log_sigmoid(ret_decay_logit.astype(F32)))
    s5_wb, s5_a, s5_wc = _s5_prepare(s5_a_re, s5_a_im, s5_b_re, s5_b_im, s5_log_dt,
                                     s5_c_re, s5_c_im)
    wglu_b = s5_w_glu.astype(BF16)
    ws_b = cmlp_ws.astype(BF16)
    bs_full = jnp.broadcast_to(cmlp_bs[..., None], cmlp_bs.shape + (CMLP_GW,))

    c_rep = jnp.broadcast_to(jnp.stack([c[0], c_ctx])[:, :, None], (2, d, LANES))
    mod_all = _adaln(c_rep, ada_w, ada_b, depth)
    mod_all = mod_all[:, 0:2].reshape(depth, 2, 6, d)
    xs = None

    for l in range(depth):
        sh1, sc1, g1, sh2, sc2, g2 = [mod_all[l, :, i] for i in range(6)]
        if l == 0:
            xs, h = _modulate_concat(ctx[0], x[0], norm1_g[l], sc1, sh1)
        else:
            h = _modulate(xs, norm1_g[l], sc1, sh1, n_ctx)
        proj = _matmul_rope(h, w_in, l, cos2, sin2, 1280, 528, 2, BF16)

        o_f, o_b = _retention(proj, ret_tables, l, n_ctx)
        y_f, y_b = _s5(proj, s5_wb, s5_a, s5_wc, s5_d, l, n_ctx)
        outs = _local_mixers(proj, o_f, o_b, y_f, y_b, wglu_b, conv_w, cmlp_ln_g, cmlp_ln_b,
                             ws_b, bs_full, l, n_ctx)

        merged = _merge(h, outs, w_merge, b_merge, w_branch, l, 256, 1056, 2)
        xs, h2 = _matmul_residual_modulate(merged, w_out, l, xs, g1, norm2_g[l], sc2, sh2,
                                           n_ctx, 352)
        act = _ffn_up(h2, ffn_w1, ffn_w3, l, 512, 1056, 2)
        xs = _matmul_residual(act, ffn_w2, l, xs, g2, n_ctx, 1024, 352, 1)

    return _final_norm(xs, final_norm_g, n_ctx)[None]
```

```python
import functools
import math

import jax
import jax.numpy as jnp
from jax import lax
from jax.experimental import pallas as pl
from jax.experimental.pallas import tpu as pltpu

F32 = jnp.float32
BF16 = jnp.bfloat16

D_MODEL = 2048
GRID_W = 64
BRANCH_W = 512
N_BRANCH = 4
N_SECTIONS = 10
RET_HEADS = 4
RET_DK = BRANCH_W // RET_HEADS
ROPE_BASE = 10000.0
S5_GROUP = 16
S5_GROUPS = BRANCH_W // S5_GROUP
S5_STATE = 64
CMLP_GROUPS = 4
CMLP_GW = BRANCH_W // CMLP_GROUPS
EPS = 1e-6

CHUNK = 128
SEQ_BLOCK = 256
LANES = 128
SUBLANES = 8
MXU_TILE = 256
BF16_ROWS = 16
HALO_ROWS = BF16_ROWS
S5_LANE_BLOCKS = BRANCH_W // LANES
S5_GROUPS_PER_BLOCK = LANES // S5_GROUP
S5_BLOCK_STATE = S5_GROUPS_PER_BLOCK * S5_STATE
S5_STATE_W = S5_GROUPS * S5_STATE
MIB = 1024 * 1024
VMEM_BUDGET_MIB = 56

SEC_Q, SEC_K, SEC_V, SEC_G, SEC_U, SEC_CX, SEC_CB, SEC_CC, SEC_GU, SEC_GV = range(N_SECTIONS)


def _cparams(semantics, vmem_mib=VMEM_BUDGET_MIB):
    return pltpu.CompilerParams(dimension_semantics=semantics,
                                vmem_limit_bytes=vmem_mib * MIB)


def _dot(a, b):
    return jnp.dot(a, b, preferred_element_type=F32)


def _row_tile(t, cap=1056):
    for tm in (1056, 1024, 768, 704, 528, 512, 384, 352, 256, 128):
        if tm <= cap and t % tm == 0:
            return tm
    raise ValueError(f"unsupported token count {t}")


def _is_ctx_rows(tile_idx, tm, n_ctx):
    rows = tile_idx * tm + lax.broadcasted_iota(jnp.int32, (tm, 1), 0)
    return rows < n_ctx


def _weight_spec(block, index_map, buffers=1):
    return pl.BlockSpec(block, index_map, pipeline_mode=pl.Buffered(buffers))


def _whole(x):
    nd = x.ndim
    return pl.BlockSpec(x.shape, lambda *_: (0,) * nd)


def _adaln_body(c_ref, w_ref, b_ref, o_ref, *, d, tn):
    reps = tn // LANES

    def kblock(kb, accs):
        rows = pl.ds(pl.multiple_of(kb * SUBLANES, SUBLANES), SUBLANES)
        w = w_ref[rows, :]
        out = []
        for r, acc in enumerate(accs):
            c = c_ref[r, rows, :]
            s = jnp.concatenate([c * jax.nn.sigmoid(c)] * reps, axis=1)
            out.append(acc + s * w)
        return tuple(out)

    zero = jnp.zeros((SUBLANES, tn), F32)
    accs = lax.fori_loop(0, d // SUBLANES, kblock, (zero, zero), unroll=8)
    rows = [jnp.sum(acc, axis=0, keepdims=True) + b_ref[...] for acc in accs]
    o_ref[...] = jnp.concatenate(rows + [jnp.zeros((SUBLANES - len(rows), tn), F32)], axis=0)


def _adaln(c_rep, ada_w, ada_b, layers):
    depth, d, n = ada_w.shape
    tn = 1024
    return pl.pallas_call(
        functools.partial(_adaln_body, d=d, tn=tn),
        grid=(layers, n // tn),
        in_specs=[
            pl.BlockSpec((2, d, LANES), lambda l, j: (0, 0, 0)),
            pl.BlockSpec((None, d, tn), lambda l, j: (l, 0, j)),
            pl.BlockSpec((None, 1, tn), lambda l, j: (l, 0, j)),
        ],
        out_specs=pl.BlockSpec((None, SUBLANES, tn), lambda l, j: (l, 0, j)),
        out_shape=jax.ShapeDtypeStruct((layers, SUBLANES, n), F32),
        compiler_params=_cparams(("parallel", "parallel")),
        name="adaln",
    )(c_rep, ada_w, ada_b.reshape(depth, 1, n))


def _modulate_body(x_ref, g_ref, sc_ref, sh_ref, o_ref, *, tm, n_ctx):
    base = pl.program_id(0) * tm
    g = g_ref[...]
    gain_lat = g * (1.0 + sc_ref[0:1, :])
    gain_ctx = g * (1.0 + sc_ref[1:2, :])
    shift_lat = sh_ref[0:1, :]
    shift_ctx = sh_ref[1:2, :]

    def slab(r, carry):
        start = pl.multiple_of(r * BF16_ROWS, BF16_ROWS)
        x = x_ref[pl.ds(start, BF16_ROWS), :]
        inv = lax.rsqrt(jnp.mean(x * x, axis=-1, keepdims=True) + EPS)
        is_ctx = (base + start) < n_ctx
        gain = jnp.where(is_ctx, gain_ctx, gain_lat)
        shift = jnp.where(is_ctx, shift_ctx, shift_lat)
        o_ref[pl.ds(start, BF16_ROWS), :] = (x * inv * gain + shift).astype(o_ref.dtype)
        return carry

    lax.fori_loop(0, tm // BF16_ROWS, slab, 0, unroll=11)


def _modulate(x, g, sc, sh, n_ctx):
    t, d = x.shape
    tm = _row_tile(t, cap=528)
    return pl.pallas_call(
        functools.partial(_modulate_body, tm=tm, n_ctx=n_ctx),
        grid=(t // tm,),
        in_specs=[
            pl.BlockSpec((tm, d), lambda i: (i, 0)),
            pl.BlockSpec((1, d), lambda i: (0, 0)),
            pl.BlockSpec((2, d), lambda i: (0, 0)),
            pl.BlockSpec((2, d), lambda i: (0, 0)),
        ],
        out_specs=pl.BlockSpec((tm, d), lambda i: (i, 0)),
        out_shape=jax.ShapeDtypeStruct((t, d), BF16),
        compiler_params=_cparams(("parallel",)),
        name="modulate",
    )(x, g.reshape(1, d), sc, sh)


def _modulate_concat_body(ctx_ref, lat_ref, g_ref, sc_ref, sh_ref, xo_ref, ho_ref, *, tm, n_ctx):
    i = pl.program_id(0)
    g = g_ref[...]

    def tile(src_ref, row):
        gain = g * (1.0 + sc_ref[row:row + 1, :])
        shift = sh_ref[row:row + 1, :]

        def slab(r, carry):
            rows = pl.ds(pl.multiple_of(r * BF16_ROWS, BF16_ROWS), BF16_ROWS)
            x = src_ref[rows, :]
            xo_ref[rows, :] = x
            inv = lax.rsqrt(jnp.mean(x * x, axis=-1, keepdims=True) + EPS)
            ho_ref[rows, :] = (x * inv * gain + shift).astype(ho_ref.dtype)
            return carry

        lax.fori_loop(0, tm // BF16_ROWS, slab, 0, unroll=8)

    @pl.when(i * tm < n_ctx)
    def _():
        tile(ctx_ref, 1)

    @pl.when(i * tm >= n_ctx)
    def _():
        tile(lat_ref, 0)


def _modulate_concat(ctx, lat, g, sc, sh):
    n_ctx, d = ctx.shape
    n_lat = lat.shape[0]
    t = n_ctx + n_lat
    tm = math.gcd(n_ctx, 512)
    nbc = n_ctx // tm
    out = pl.BlockSpec((tm, d), lambda i: (i, 0))
    return pl.pallas_call(
        functools.partial(_modulate_concat_body, tm=tm, n_ctx=n_ctx),
        grid=(t // tm,),
        in_specs=[
            pl.BlockSpec((tm, d), lambda i: (jnp.minimum(i, nbc - 1), 0)),
            pl.BlockSpec((tm, d), lambda i: (jnp.maximum(i - nbc, 0), 0)),
            pl.BlockSpec((1, d), lambda i: (0, 0)),
            pl.BlockSpec((2, d), lambda i: (0, 0)),
            pl.BlockSpec((2, d), lambda i: (0, 0)),
        ],
        out_specs=[out, out],
        out_shape=[jax.ShapeDtypeStruct((t, d), F32), jax.ShapeDtypeStruct((t, d), BF16)],
        compiler_params=_cparams(("parallel",)),
        name="modulate_concat",
    )(ctx, lat, g.reshape(1, d), sc, sh)


def _mm_rope_body(x_ref, w_ref, cos_ref, sin_ref, o_ref, wb_ref, *, rope_cols):
    @pl.when(pl.program_id(1) == 0)
    def _():
        wb_ref[...] = w_ref[...].astype(BF16)

    @pl.when(pl.program_id(0) == 0)
    def _():
        y = _dot(x_ref[...], wb_ref[...])
        cos = cos_ref[...]
        sin = sin_ref[...]
        for c0 in range(0, o_ref.shape[1], RET_DK):
            blk = y[:, c0:c0 + RET_DK]
            if c0 < rope_cols:
                blk = blk * cos + pltpu.roll(blk, RET_DK // 2, 1) * sin
            o_ref[:, c0:c0 + RET_DK] = blk.astype(o_ref.dtype)

    @pl.when(pl.program_id(0) != 0)
    def _():
        o_ref[...] = _dot(x_ref[...], wb_ref[...]).astype(o_ref.dtype)


def _matmul_rope(x, w_stack, layer, cos2, sin2, tn, tm_cap, w_buffers, out_dtype):
    t, k = x.shape
    n = w_stack.shape[-1]
    tm = _row_tile(t, tm_cap)
    rope_cols = 2 * BRANCH_W
    assert rope_cols <= tn and SEC_Q == 0 and SEC_K == 1
    table = pl.BlockSpec((tm, RET_DK), lambda j, i: (jnp.where(j == 0, i, 0), 0))
    return pl.pallas_call(
        functools.partial(_mm_rope_body, rope_cols=rope_cols),
        grid=(n // tn, t // tm),
        in_specs=[
            pl.BlockSpec((tm, k), lambda j, i: (i, 0)),
            _weight_spec((None, k, tn), lambda j, i: (layer, 0, j), w_buffers),
            table, table,
        ],
        out_specs=pl.BlockSpec((tm, tn), lambda j, i: (i, j)),
        out_shape=jax.ShapeDtypeStruct((t, n), out_dtype),
        scratch_shapes=[pltpu.VMEM((k, tn), BF16)],
        compiler_params=_cparams(("arbitrary", "arbitrary")),
        name="matmul_rope",
    )(x, w_stack, cos2, sin2)


def _mm_residual_body(a_ref, w_ref, x_ref, g_ref, o_ref, wb_ref, *, tm, n_ctx):
    @pl.when(pl.program_id(1) == 0)
    def _():
        wb_ref[...] = w_ref[...].astype(BF16)

    y = _dot(a_ref[...], wb_ref[...])
    is_ctx = _is_ctx_rows(pl.program_id(1), tm, n_ctx)
    gate = jnp.where(is_ctx, g_ref[1:2, :], g_ref[0:1, :])
    o_ref[...] = x_ref[...] + gate * y


def _matmul_residual(a, w_stack, layer, x, gate, n_ctx, tn, tm_cap, w_buffers):
    t, k = a.shape
    n = w_stack.shape[-1]
    tm = _row_tile(t, tm_cap)
    return pl.pallas_call(
        functools.partial(_mm_residual_body, tm=tm, n_ctx=n_ctx),
        grid=(n // tn, t // tm),
        in_specs=[
            pl.BlockSpec((tm, k), lambda j, i: (i, 0)),
            _weight_spec((None, k, tn), lambda j, i: (layer, 0, j), w_buffers),
            pl.BlockSpec((tm, tn), lambda j, i: (i, j)),
            pl.BlockSpec((2, tn), lambda j, i: (0, j)),
        ],
        out_specs=pl.BlockSpec((tm, tn), lambda j, i: (i, j)),
        out_shape=jax.ShapeDtypeStruct((t, n), F32),
        scratch_shapes=[pltpu.VMEM((k, tn), BF16)],
        compiler_params=_cparams(("arbitrary", "arbitrary")),
        name="matmul_residual",
    )(a, w_stack, x, gate)


def _mm_residual_modulate_body(a_ref, w_ref, x_ref, gate_ref, g_ref, sc_ref, sh_ref,
                               xo_ref, ho_ref, wb_ref, y0_ref, y1_ref, *, tm, n_ctx):
    i = pl.program_id(0)

    @pl.when(i == 0)
    def _():
        wb_ref[...] = w_ref[...].astype(BF16)
        y1_ref[...] = jnp.zeros_like(y1_ref)

    base = jnp.maximum(i - 1, 0) * tm
    g = g_ref[...]
    gain_lat = g * (1.0 + sc_ref[0:1, :])
    gain_ctx = g * (1.0 + sc_ref[1:2, :])

    def run(y_write, y_read):
        y_write[...] = _dot(a_ref[...], wb_ref[...])
        for r in range(tm // BF16_ROWS):
            rows = slice(r * BF16_ROWS, (r + 1) * BF16_ROWS)
            is_ctx = (base + r * BF16_ROWS) < n_ctx
            gate = jnp.where(is_ctx, gate_ref[1:2, :], gate_ref[0:1, :])
            x = x_ref[rows, :] + gate * y_read[rows, :]
            xo_ref[rows, :] = x
            inv = lax.rsqrt(jnp.mean(x * x, axis=-1, keepdims=True) + EPS)
            gain = jnp.where(is_ctx, gain_ctx, gain_lat)
            shift = jnp.where(is_ctx, sh_ref[1:2, :], sh_ref[0:1, :])
            ho_ref[rows, :] = (x * inv * gain + shift).astype(ho_ref.dtype)

    @pl.when(lax.rem(i, 2) == 0)
    def _():
        run(y0_ref, y1_ref)

    @pl.when(lax.rem(i, 2) == 1)
    def _():
        run(y1_ref, y0_ref)


def _matmul_residual_modulate(a, w_stack, layer, x, gate, g, sc, sh, n_ctx, tm_cap):
    t, k = a.shape
    n = w_stack.shape[-1]
    tm = _row_tile(t, tm_cap)
    nt = t // tm
    lagged = pl.BlockSpec((tm, n), lambda i: (jnp.maximum(i - 1, 0), 0))
    vec2 = pl.BlockSpec((2, n), lambda i: (0, 0))
    product = pltpu.VMEM((tm, n), F32)
    return pl.pallas_call(
        functools.partial(_mm_residual_modulate_body, tm=tm, n_ctx=n_ctx),
        grid=(nt + 1,),
        in_specs=[
            pl.BlockSpec((tm, k), lambda i: (jnp.minimum(i, nt - 1), 0)),
            _weight_spec((None, k, n), lambda i: (layer, 0, 0)),
            lagged, vec2,
            pl.BlockSpec((1, n), lambda i: (0, 0)), vec2, vec2,
        ],
        out_specs=[lagged, lagged],
        out_shape=[jax.ShapeDtypeStruct((t, n), F32), jax.ShapeDtypeStruct((t, n), BF16)],
        scratch_shapes=[pltpu.VMEM((k, n), BF16), product, product],
        compiler_params=_cparams(("arbitrary",)),
        name="matmul_residual_modulate",
    )(a, w_stack, x, gate, g.reshape(1, n), sc, sh)


def _ffn_up_body(h_ref, w1_ref, w3_ref, o_ref, w1b_ref, w3b_ref):
    @pl.when(pl.program_id(1) == 0)
    def _():
        w1b_ref[...] = w1_ref[...].astype(BF16)
        w3b_ref[...] = w3_ref[...].astype(BF16)

    h = h_ref[...]
    for c0 in range(0, o_ref.shape[1], MXU_TILE):
        cols = slice(c0, c0 + MXU_TILE)
        a = _dot(h, w1b_ref[:, cols])
        b = _dot(h, w3b_ref[:, cols])
        o_ref[:, cols] = (a * jax.nn.sigmoid(a) * b).astype(o_ref.dtype)


def _ffn_up(h, w1_stack, w3_stack, layer, tn, tm_cap, w_buffers):
    t, k = h.shape
    n = w1_stack.shape[-1]
    tm = _row_tile(t, tm_cap)
    wspec = _weight_spec((None, k, tn), lambda j, i: (layer, 0, j), w_buffers)
    return pl.pallas_call(
        _ffn_up_body,
        grid=(n // tn, t // tm),
        in_specs=[pl.BlockSpec((tm, k), lambda j, i: (i, 0)), wspec, wspec],
        out_specs=pl.BlockSpec((tm, tn), lambda j, i: (i, j)),
        out_shape=jax.ShapeDtypeStruct((t, n), BF16),
        scratch_shapes=[pltpu.VMEM((k, tn), BF16), pltpu.VMEM((k, tn), BF16)],
        compiler_params=_cparams(("arbitrary", "arbitrary")),
        name="ffn_up",
    )(h, w1_stack, w3_stack)


def _merge_body(*refs):
    nb = N_BRANCH
    h_ref = refs[0]
    o_refs = refs[1:1 + nb]
    wm_refs = refs[1 + nb:1 + 2 * nb]
    bm_refs = refs[1 + 2 * nb:1 + 3 * nb]
    wb_refs = refs[1 + 3 * nb:1 + 4 * nb]
    out_ref, wmb_ref, wbb_ref = refs[1 + 4 * nb:]

    @pl.when(pl.program_id(1) == 0)
    def _():
        for k in range(nb):
            wmb_ref[k] = wm_refs[k][...].astype(BF16)
            wbb_ref[k] = wb_refs[k][...].astype(BF16)

    h = h_ref[...]
    acc = None
    for k in range(nb):
        gate = jax.nn.sigmoid(_dot(h, wmb_ref[k]) + bm_refs[k][...])
        term = gate * _dot(o_refs[k][...], wbb_ref[k])
        acc = term if acc is None else acc + term
    out_ref[...] = acc.astype(out_ref.dtype)


def _merge(h, outs, w_merge, b_merge, w_branch, layer, tn, tm_cap, w_buffers):
    t, d = h.shape
    bw = outs[0].shape[1]
    depth = w_merge.shape[0]
    tm = _row_tile(t, tm_cap)
    nt = d // tn
    b_merge3 = b_merge.reshape(depth, 1, N_BRANCH * d)
    in_specs = [pl.BlockSpec((tm, d), lambda j, i: (i, 0))]
    in_specs += [pl.BlockSpec((tm, bw), lambda j, i: (i, 0)) for _ in range(N_BRANCH)]
    in_specs += [_weight_spec((None, d, tn), lambda j, i, k=k: (layer, 0, k * nt + j), w_buffers)
                 for k in range(N_BRANCH)]
    in_specs += [pl.BlockSpec((None, 1, tn), lambda j, i, k=k: (layer, 0, k * nt + j))
                 for k in range(N_BRANCH)]
    in_specs += [_weight_spec((None, None, bw, tn), lambda j, i, k=k: (layer, k, 0, j), w_buffers)
                 for k in range(N_BRANCH)]
    return pl.pallas_call(
        _merge_body,
        grid=(nt, t // tm),
        in_specs=in_specs,
        out_specs=pl.BlockSpec((tm, tn), lambda j, i: (i, j)),
        out_shape=jax.ShapeDtypeStruct((t, d), BF16),
        scratch_shapes=[pltpu.VMEM((N_BRANCH, d, tn), BF16),
                        pltpu.VMEM((N_BRANCH, bw, tn), BF16)],
        compiler_params=_cparams(("arbitrary", "arbitrary")),
        name="merge",
    )(h, *outs, *([w_merge] * N_BRANCH), *([b_merge3] * N_BRANCH), *([w_branch] * N_BRANCH))


def _block_order(step, n_blocks, n_ctx_blocks, reverse):
    if not reverse:
        return step
    return jnp.where(step < n_ctx_blocks, n_ctx_blocks - 1 - step,
                     n_blocks + n_ctx_blocks - 1 - step)


def _retention_body(qf_ref, kf_ref, vf_ref, qb_ref, kb_ref, vb_ref, tab_ref,
                    of_ref, ob_ref, s_ref):
    @pl.when(pl.program_id(0) == 0)
    def _():
        s_ref[...] = jnp.zeros_like(s_ref)

    chunks = SEQ_BLOCK // CHUNK
    dirs = ((qf_ref, kf_ref, vf_ref, of_ref, range(chunks)),
            (qb_ref, kb_ref, vb_ref, ob_ref, range(chunks - 1, -1, -1)))
    for d, (q_ref, k_ref, v_ref, o_ref, order) in enumerate(dirs):
        for h in range(RET_HEADS):
            sl = slice(h * RET_DK, (h + 1) * RET_DK)
            dmat = tab_ref[d, 0, h]
            q_decay = tab_ref[d, 1, h]
            k_decay = tab_ref[d, 2, h]
            g_chunk = tab_ref[d, 3, h]
            state = s_ref[d, h]
            for ci in order:
                rows = slice(ci * CHUNK, (ci + 1) * CHUNK)
                q = q_ref[rows, sl].astype(BF16)
                k = k_ref[rows, sl].astype(BF16)
                v = v_ref[rows, sl].astype(BF16)
                scores = lax.dot_general(q, k, (((1,), (1,)), ((), ())),
                                         preferred_element_type=F32) * dmat
                intra = _dot(scores.astype(BF16), v)
                cross = _dot((q.astype(F32) * q_decay).astype(BF16), state.astype(BF16))
                kv = _dot((k.astype(F32) * k_decay).T.astype(BF16), v)
                o_ref[rows, sl] = (intra + cross).astype(o_ref.dtype)
                state = g_chunk * state + kv
            s_ref[d, h] = state


def _retention_tables(log_gamma):
    pos = jnp.arange(CHUNK, dtype=F32)
    i = pos[:, None]
    j = pos[None, :]
    full = log_gamma.shape[:1] + (RET_HEADS, CHUNK, RET_DK)
    scale = RET_DK ** -0.5

    def one_dir(lg, diff, mask, q_pow, k_pow):
        dmat = jnp.where(mask, jnp.exp(lg[:, :, None, None] * jnp.maximum(diff, 0.0)), 0.0) * scale
        q_decay = jnp.broadcast_to(jnp.exp(lg[:, :, None] * q_pow)[..., None], full) * scale
        k_decay = jnp.broadcast_to(jnp.exp(lg[:, :, None] * k_pow)[..., None], full)
        g_chunk = jnp.broadcast_to(jnp.exp(lg * CHUNK)[:, :, None, None], full)
        return jnp.stack([dmat, q_decay, k_decay, g_chunk], axis=1)

    fwd = one_dir(log_gamma[:, 0], i - j, (i - j) >= 0, pos + 1.0, CHUNK - 1.0 - pos)
    bwd = one_dir(log_gamma[:, 1], j - i, (j - i) > 0, CHUNK - pos, pos)
    return jnp.stack([fwd, bwd], axis=1)


def _retention(proj, tables, layer, n_ctx):
    t = proj.shape[0]
    nb = t // SEQ_BLOCK
    nbc = n_ctx // SEQ_BLOCK

    def specs(reverse):
        def blk(st):
            return _block_order(st, nb, nbc, reverse)
        sec = [pl.BlockSpec((SEQ_BLOCK, BRANCH_W), lambda st, s=s: (blk(st), s))
               for s in (SEC_Q, SEC_K, SEC_V)]
        out = pl.BlockSpec((SEQ_BLOCK, BRANCH_W), lambda st: (blk(st), 0))
        return sec, out

    in_f, out_f = specs(False)
    in_b, out_b = specs(True)
    tab_spec = pl.BlockSpec((None, 2, 4, RET_HEADS, CHUNK, RET_DK),
                            lambda st: (layer, 0, 0, 0, 0, 0))
    shape = jax.ShapeDtypeStruct((t, BRANCH_W), BF16)
    return pl.pallas_call(
        _retention_body,
        grid=(nb,),
        in_specs=in_f + in_b + [tab_spec],
        out_specs=[out_f, out_b],
        out_shape=[shape, shape],
        scratch_shapes=[pltpu.VMEM((2, RET_HEADS, RET_DK, RET_DK), F32)],
        compiler_params=_cparams(("arbitrary",)),
        name="retention",
    )(proj, proj, proj, proj, proj, proj, tables)


def _s5_body(uf_ref, ub_ref, wb_ref, a_ref, wc_ref, d_ref, yf_ref, yb_ref,
             x_ref, bu_ref, st_ref):
    @pl.when(pl.program_id(0) == 0)
    def _():
        x_ref[...] = jnp.zeros_like(x_ref)

    for d, u_ref in enumerate((uf_ref, ub_ref)):
        ub = u_ref[...].astype(BF16)
        for r in range(S5_LANE_BLOCKS):
            bu = _dot(ub[:, r * LANES:(r + 1) * LANES], wb_ref[d, r])
            cols = slice(r * S5_BLOCK_STATE, (r + 1) * S5_BLOCK_STATE)
            bu_ref[2 * d, :, cols] = bu[:, :S5_BLOCK_STATE]
            bu_ref[2 * d + 1, :, cols] = bu[:, S5_BLOCK_STATE:]

    af_re, af_im = a_ref[0, 0:1, :], a_ref[0, 1:2, :]
    ab_re, ab_im = a_ref[1, 0:1, :], a_ref[1, 1:2, :]

    def step(i, carry):
        f_re, f_im, b_re, b_im = carry
        tf = pl.ds(i, 1)
        tb = pl.ds(SEQ_BLOCK - 1 - i, 1)
        nf_re = af_re * f_re - af_im * f_im + bu_ref[0, tf, :]
        nf_im = af_re * f_im + af_im * f_re + bu_ref[1, tf, :]
        nb_re = ab_re * b_re - ab_im * b_im + bu_ref[2, tb, :]
        nb_im = ab_re * b_im + ab_im * b_re + bu_ref[3, tb, :]
        st_ref[0, tf, :] = nf_re
        st_ref[1, tf, :] = nf_im
        st_ref[2, tb, :] = nb_re
        st_ref[3, tb, :] = nb_im
        return nf_re, nf_im, nb_re, nb_im

    carry = lax.fori_loop(0, SEQ_BLOCK, step,
                          (x_ref[0:1, :], x_ref[1:2, :], x_ref[2:3, :], x_ref[3:4, :]),
                          unroll=8)
    for n, val in enumerate(carry):
        x_ref[n:n + 1, :] = val

    def readout(d):
        ys = []
        for r in range(S5_LANE_BLOCKS):
            cols = slice(r * S5_BLOCK_STATE, (r + 1) * S5_BLOCK_STATE)
            st = jnp.concatenate([st_ref[2 * d, :, cols], st_ref[2 * d + 1, :, cols]],
                                 axis=1).astype(BF16)
            ys.append(_dot(st, wc_ref[r]))
        return jnp.concatenate(ys, axis=1)

    yf_ref[...] = (readout(0) + d_ref[...] * uf_ref[...].astype(F32)).astype(yf_ref.dtype)
    yb_ref[...] = readout(1).astype(yb_ref.dtype)


def _s5_prepare(a_re, a_im, b_re, b_im, log_dt, c_re, c_im):
    depth = a_re.shape[0]
    a_re = a_re[:, None]
    a_im = a_im[:, None]
    dt = jnp.exp(log_dt)[..., None]
    mag = jnp.exp(dt * a_re)
    ang = dt * a_im
    ab_re, ab_im = mag * jnp.cos(ang), mag * jnp.sin(ang)
    nr, ni = ab_re - 1.0, ab_im
    den = a_re * a_re + a_im * a_im
    f_re = (nr * a_re + ni * a_im) / den
    f_im = (ni * a_re - nr * a_im) / den
    bb_re = f_re[..., None] * b_re - f_im[..., None] * b_im
    bb_im = f_re[..., None] * b_im + f_im[..., None] * b_re
    gpb = S5_GROUPS_PER_BLOCK
    row_group = jnp.arange(LANES) // S5_GROUP
    col_group = jnp.arange(S5_BLOCK_STATE) // S5_STATE

    def in_map(bb):
        rows = jnp.swapaxes(bb.reshape(depth, 2, S5_LANE_BLOCKS, gpb, S5_STATE, S5_GROUP), -1, -2)
        rows = rows.reshape(depth, 2, S5_LANE_BLOCKS, LANES, S5_STATE)
        mask = (row_group[:, None] == col_group[None, :]).astype(F32)
        return jnp.tile(rows, (1, 1, 1, 1, gpb)) * mask

    def out_map(c):
        rows = jnp.swapaxes(c.reshape(depth, S5_LANE_BLOCKS, gpb, S5_GROUP, S5_STATE), -1, -2)
        rows = rows.reshape(depth, S5_LANE_BLOCKS, S5_BLOCK_STATE, S5_GROUP)
        mask = (col_group[:, None] == row_group[None, :]).astype(F32)
        return jnp.tile(rows, (1, 1, 1, gpb)) * mask

    wb = jnp.concatenate([in_map(bb_re), in_map(bb_im)], axis=-1).astype(BF16)
    a = jnp.stack([ab_re.reshape(depth, 2, S5_STATE_W), ab_im.reshape(depth, 2, S5_STATE_W)],
                  axis=2)
    wc = jnp.concatenate([out_map(c_re), out_map(-c_im)], axis=-2).astype(BF16)
    return wb, a, wc


def _s5(proj, wb, a, wc, s5_d, layer, n_ctx):
    t = proj.shape[0]
    nb = t // SEQ_BLOCK
    nbc = n_ctx // SEQ_BLOCK
    depth = s5_d.shape[0]

    def io(reverse, sec):
        return pl.BlockSpec((SEQ_BLOCK, BRANCH_W),
                            lambda st: (_block_order(st, nb, nbc, reverse), sec))

    shape = jax.ShapeDtypeStruct((t, BRANCH_W), BF16)
    planes = pltpu.VMEM((4, SEQ_BLOCK, S5_STATE_W), F32)
    return pl.pallas_call(
        _s5_body,
        grid=(nb,),
        in_specs=[io(False, SEC_U), io(True, SEC_U),
                  pl.BlockSpec((None,) + wb.shape[1:], lambda st: (layer, 0, 0, 0, 0)),
                  pl.BlockSpec((None,) + a.shape[1:], lambda st: (layer, 0, 0, 0)),
                  pl.BlockSpec((None,) + wc.shape[1:], lambda st: (layer, 0, 0, 0)),
                  pl.BlockSpec((None, 1, BRANCH_W), lambda st: (layer, 0, 0))],
        out_specs=[io(False, 0), io(True, 0)],
        out_shape=[shape, shape],
        scratch_shapes=[pltpu.VMEM((4, S5_STATE_W), F32), planes, planes],
        compiler_params=_cparams(("arbitrary",)),
        name="s5",
    )(proj, proj, wb, a, wc, s5_d.reshape(depth, 1, BRANCH_W))


def _local_body(of_ref, ob_ref, g_ref, yf_ref, yb_ref, wglu_ref,
                cx_ref, cb_ref, cc_ref, cxp_ref, ccp_ref, cxn_ref, ccn_ref, cw_ref,
                gu_ref, gv_ref, lng_ref, lnb_ref, ws_ref, bs_ref,
                oret_ref, os5_ref, oc_ref, og_ref, *, n_blocks, n_ctx_blocks):
    for h in range(RET_HEADS):
        sl = slice(h * RET_DK, (h + 1) * RET_DK)
        o = of_ref[:, sl].astype(F32) + ob_ref[:, sl].astype(F32)
        mu = jnp.mean(o, axis=-1, keepdims=True)
        var = jnp.mean(jnp.square(o - mu), axis=-1, keepdims=True)
        o = (o - mu) * lax.rsqrt(var + EPS)
        g = g_ref[:, sl].astype(F32)
        oret_ref[:, sl] = (o * (g * jax.nn.sigmoid(g))).astype(oret_ref.dtype)

    y = jax.nn.gelu(yf_ref[...].astype(F32) + yb_ref[...].astype(F32))
    os5_ref[...] = (y * jax.nn.sigmoid(_dot(y.astype(BF16), wglu_ref[...]))).astype(os5_ref.dtype)

    c = pl.program_id(0)
    seq_start = jnp.logical_or(c == 0, c == n_ctx_blocks)
    seq_end = jnp.logical_or(c == n_ctx_blocks - 1, c == n_blocks - 1)
    z = cc_ref[...].astype(F32) * cx_ref[...].astype(F32)
    z_prev = (ccp_ref[...].astype(F32) * cxp_ref[...].astype(F32))[HALO_ROWS - 1:HALO_ROWS, :]
    z_next = (ccn_ref[...].astype(F32) * cxn_ref[...].astype(F32))[0:1, :]
    z_prev = jnp.where(seq_start, 0.0, z_prev)
    z_next = jnp.where(seq_end, 0.0, z_next)
    rows = lax.broadcasted_iota(jnp.int32, (SEQ_BLOCK, 1), 0)
    z_up = jnp.where(rows == 0, z_prev, pltpu.roll(z, 1, 0))
    z_dn = jnp.where(rows == SEQ_BLOCK - 1, z_next, pltpu.roll(z, SEQ_BLOCK - 1, 0))
    yc = cw_ref[0:1, :] * z_up + cw_ref[1:2, :] * z + cw_ref[2:3, :] * z_dn
    oc_ref[...] = (cb_ref[...].astype(F32) * yc).astype(oc_ref.dtype)

    u = jax.nn.gelu(gu_ref[...].astype(F32))
    v = jax.nn.gelu(gv_ref[...].astype(F32))
    mu = jnp.mean(v, axis=-1, keepdims=True)
    var = jnp.mean(jnp.square(v - mu), axis=-1, keepdims=True)
    v = ((v - mu) * lax.rsqrt(var + EPS) * lng_ref[...] + lnb_ref[...]).astype(BF16)
    for ci in range(SEQ_BLOCK // CHUNK):
        rws = slice(ci * CHUNK, (ci + 1) * CHUNK)
        for g in range(CMLP_GROUPS):
            sl = slice(g * CMLP_GW, (g + 1) * CMLP_GW)
            mixed = _dot(ws_ref[g], v[rws, sl]) + bs_ref[g]
            og_ref[rws, sl] = (u[rws, sl] * mixed).astype(og_ref.dtype)


def _local_mixers(proj, o_f, o_b, y_f, y_b, wglu_b, conv_w, ln_g, ln_b, ws_b, bs_full,
                  layer, n_ctx):
    t = proj.shape[0]
    nb = t // SEQ_BLOCK
    nbc = n_ctx // SEQ_BLOCK
    per = SEQ_BLOCK // HALO_ROWS
    last = t // HALO_ROWS - 1
    depth = conv_w.shape[0]

    def sec(s):
        return pl.BlockSpec((SEQ_BLOCK, BRANCH_W), lambda c: (c, s))

    def prev(s):
        return pl.BlockSpec((HALO_ROWS, BRANCH_W), lambda c: (jnp.maximum(c * per - 1, 0), s))

    def nxt(s):
        return pl.BlockSpec((HALO_ROWS, BRANCH_W), lambda c: (jnp.minimum((c + 1) * per, last), s))

    def layer_spec(x):
        nd = x.ndim - 1
        return pl.BlockSpec((None,) + x.shape[1:], lambda c: (layer,) + (0,) * nd)

    ln_g3 = ln_g.reshape(depth, 1, BRANCH_W)
    ln_b3 = ln_b.reshape(depth, 1, BRANCH_W)
    blk = sec(0)
    out_shape = jax.ShapeDtypeStruct((t, BRANCH_W), BF16)
    return pl.pallas_call(
        functools.partial(_local_body, n_blocks=nb, n_ctx_blocks=nbc),
        grid=(nb,),
        in_specs=[blk, blk, sec(SEC_G), blk, blk, layer_spec(wglu_b),
                  sec(SEC_CX), sec(SEC_CB), sec(SEC_CC),
                  prev(SEC_CX), prev(SEC_CC), nxt(SEC_CX), nxt(SEC_CC), layer_spec(conv_w),
                  sec(SEC_GU), sec(SEC_GV), layer_spec(ln_g3), layer_spec(ln_b3),
                  layer_spec(ws_b), layer_spec(bs_full)],
        out_specs=[blk, blk, blk, blk],
        out_shape=[out_shape] * 4,
        compiler_params=_cparams(("parallel",)),
        name="local_mixers",
    )(o_f, o_b, proj, y_f, y_b, wglu_b,
      proj, proj, proj, proj, proj, proj, proj, conv_w,
      proj, proj, ln_g3, ln_b3, ws_b, bs_full)


def _final_norm_body(x_ref, g_ref, o_ref, *, tm):
    g = g_ref[...]

    def slab(r, carry):
        start = pl.multiple_of(r * BF16_ROWS, BF16_ROWS)
        x = x_ref[pl.ds(start, BF16_ROWS), :]
        inv = lax.rsqrt(jnp.mean(x * x, axis=-1, keepdims=True) + EPS)
        o_ref[pl.ds(start, BF16_ROWS), :] = x * inv * g
        return carry

    lax.fori_loop(0, tm // BF16_ROWS, slab, 0, unroll=8)


def _final_norm(x, g, n_ctx):
    t, d = x.shape
    tm = math.gcd(n_ctx, 512)
    skip = n_ctx // tm
    n_lat = t - n_ctx
    return pl.pallas_call(
        functools.partial(_final_norm_body, tm=tm),
        grid=(n_lat // tm,),
        in_specs=[pl.BlockSpec((tm, d), lambda i: (i + skip, 0)),
                  pl.BlockSpec((1, d), lambda i: (0, 0))],
        out_specs=pl.BlockSpec((tm, d), lambda i: (i, 0)),
        out_shape=jax.ShapeDtypeStruct((n_lat, d), F32),
        compiler_params=_cparams(("parallel",)),
        name="final_norm",
    )(x, g.reshape(1, d))


def _rope_tables(n_lat, n_ctx):
    rows = n_lat // GRID_W
    row = jnp.broadcast_to(jnp.arange(rows)[:, None], (rows, GRID_W)).reshape(-1).astype(F32)
    col = jnp.broadcast_to(jnp.arange(GRID_W)[None, :], (rows, GRID_W)).reshape(-1).astype(F32)
    nf = RET_DK // 4
    freqs = ROPE_BASE ** (-jnp.arange(nf, dtype=F32) / nf)
    ang = jnp.concatenate([row[:, None] * freqs, col[:, None] * freqs], axis=-1)
    cos, sin = jnp.cos(ang), jnp.sin(ang)
    cos2 = jnp.concatenate([cos, cos], axis=-1)
    sin2 = jnp.concatenate([-sin, sin], axis=-1)
    cos2 = jnp.concatenate([jnp.ones((n_ctx, RET_DK), F32), cos2], axis=0)
    sin2 = jnp.concatenate([jnp.zeros((n_ctx, RET_DK), F32), sin2], axis=0)
    return cos2, sin2


def kernel(x, c, ctx, c_ctx, ada_w, ada_b, norm1_g, norm2_g, w_in, ret_decay_logit, s5_a_re, s5_a_im, s5_b_re, s5_b_im, s5_c_re, s5_c_im, s5_d, s5_log_dt, s5_w_glu, conv_w, cmlp_ln_g, cmlp_ln_b, cmlp_ws, cmlp_bs, w_branch, w_merge, b_merge, w_out, ffn_w1, ffn_w3, ffn_w2, final_norm_g):
    batch, n_lat, d = x.shape
    assert batch == 1 and c.shape[0] == 1 and ctx.shape[0] == 1
    n_ctx = ctx.shape[1]
    assert n_ctx % SEQ_BLOCK == 0 and n_lat % SEQ_BLOCK == 0 and n_lat % GRID_W == 0
    depth = ada_w.shape[0]

    cos2, sin2 = _rope_tables(n_lat, n_ctx)
    ret_tables = _retention_tables(jax.nn.log_sigmoid(ret_decay_logit.astype(F32)))
    s5_wb, s5_a, s5_wc = _s5_prepare(s5_a_re, s5_a_im, s5_b_re, s5_b_im, s5_log_dt,
                                     s5_c_re, s5_c_im)
    wglu_b = s5_w_glu.astype(BF16)
    ws_b = cmlp_ws.astype(BF16)
    bs_full = jnp.broadcast_to(cmlp_bs[..., None], cmlp_bs.shape + (CMLP_GW,))

    c_rep = jnp.broadcast_to(jnp.stack([c[0], c_ctx])[:, :, None], (2, d, LANES))
    mod_all = _adaln(c_rep, ada_w, ada_b, depth)
    mod_all = mod_all[:, 0:2].reshape(depth, 2, 6, d)
    xs = None

    for l in range(depth):
        sh1, sc1, g1, sh2, sc2, g2 = [mod_all[l, :, i] for i in range(6)]
        if l == 0:
            xs, h = _modulate_concat(ctx[0], x[0], norm1_g[l], sc1, sh1)
        else:
            h = _modulate(xs, norm1_g[l], sc1, sh1, n_ctx)
        proj = _matmul_rope(h, w_in, l, cos2, sin2, 1280, 528, 2, BF16)

        o_f, o_b = _retention(proj, ret_tables, l, n_ctx)
        y_f, y_b = _s5(proj, s5_wb, s5_a, s5_wc, s5_d, l, n_ctx)
        outs = _local_mixers(proj, o_f, o_b, y_f, y_b, wglu_b, conv_w, cmlp_ln_g, cmlp_ln_b,
                             ws_b, bs_full, l, n_ctx)

        merged = _merge(h, outs, w_merge, b_merge, w_branch, l, 256, 1056, 2)
        xs, h2 = _matmul_residual_modulate(merged, w_out, l, xs, g1, norm2_g[l], sc2, sh2,
                                           n_ctx, 352)
        act = _ffn_up(h2, ffn_w1, ffn_w3, l, 512, 1056, 2)
        xs = _matmul_residual(act, ffn_w2, l, xs, g2, n_ctx, 1024, 352, 1)

    return _final_norm(xs, final_norm_g, n_ctx)[None]
```
